```python
import math
import jax
import jax.numpy as jnp
from jax import lax
import numpy as np

D_MODEL = 1024
BATCH = 8
SEQ = 2048
DEPTH = 2

MEM_LEN = 256
N_EVEN = (DEPTH + 1) // 2
N_ODD = DEPTH // 2
ALPHA = (2.0 * DEPTH) ** 0.25
BETA = (8.0 * DEPTH) ** -0.25
LN_EPS = 1e-5

GMLP_WIDTH = D_MODEL // 2
GMLP_GROUPS = 4
GMLP_GDIM = GMLP_WIDTH // GMLP_GROUPS
GMLP_CHUNK = 128
HGRN_WIDTH = D_MODEL // 2
HGRN_HEADS = 4
HGRN_DK = HGRN_WIDTH // HGRN_HEADS
HGRN_CHUNK = 64
EVEN_IN_WIDTH = 2 * GMLP_WIDTH + 4 * HGRN_WIDTH

MOBA_HEADS = 16
MOBA_HDIM = D_MODEL // MOBA_HEADS
MOBA_BLOCK = 256
MOBA_TOPK = 3
MOBA_QCHUNK = 16

MEM_HEADS = 4
MEM_HDIM = D_MODEL // MEM_HEADS

D_FF = int(math.ceil(8 * D_MODEL / 3 / 256)) * 256

kernel_name = 'hybrid_gmlp_hgrn2_moba_deepnorm'


def layer_norm(x, g, b):
    xf = x.astype(jnp.float32)
    mu = jnp.mean(xf, axis=-1, keepdims=True)
    var = jnp.mean(jnp.square(xf - mu), axis=-1, keepdims=True)
    return ((xf - mu) * lax.rsqrt(var + LN_EPS)).astype(x.dtype) * g + b


def rms_norm(x, g):
    xf = x.astype(jnp.float32)
    return (xf * lax.rsqrt(jnp.mean(jnp.square(xf), axis=-1, keepdims=True) + LN_EPS)).astype(x.dtype) * g


def alibi_slopes(n_heads):
    return jnp.asarray(2.0 ** (-8.0 * np.arange(1, n_heads + 1) / n_heads), dtype=jnp.float32)


def spatial_gating_unit(u, v, ws, bs, ln_g, ln_b):
    bsz, t_len, _ = u.shape
    v = layer_norm(v.reshape(bsz, t_len, GMLP_GROUPS, GMLP_GDIM),
                   ln_g.reshape(GMLP_GROUPS, GMLP_GDIM), ln_b.reshape(GMLP_GROUPS, GMLP_GDIM))
    v = v.reshape(bsz, t_len // GMLP_CHUNK, GMLP_CHUNK, GMLP_GROUPS, GMLP_GDIM)
    causal = jnp.tril(jnp.ones((GMLP_CHUNK, GMLP_CHUNK), dtype=bool))
    w = jnp.where(causal[None], ws, jnp.zeros_like(ws))
    s = jnp.einsum('gts,bnsgc->bntgc', w, v) + bs.T[:, :, None]
    return u * s.reshape(bsz, t_len, GMLP_WIDTH)


def hgrn2_recurrence(q, f_logit, i, lb):
    bsz, t_len, _ = q.shape
    n_chunks = t_len // HGRN_CHUNK

    def heads(z):
        return z.astype(jnp.float32).reshape(bsz, n_chunks, HGRN_CHUNK, HGRN_HEADS, HGRN_DK).transpose(0, 3, 1, 2, 4)

    f = lb + (1.0 - lb) * jax.nn.sigmoid(f_logit.astype(jnp.float32))
    q_c, k_c, v_c = heads(q), heads(1.0 - f), heads(i)
    cum = jnp.cumsum(heads(jnp.log(f)), axis=3)
    q_dec = q_c * jnp.exp(cum)
    k_dec = k_c * jnp.exp(-cum)
    k_tail = k_c * jnp.exp(cum[..., -1:, :] - cum)
    causal = jnp.tril(jnp.ones((HGRN_CHUNK, HGRN_CHUNK), dtype=bool))
    attn = jnp.where(causal, jnp.einsum('bhntk,bhnsk->bhnts', q_dec, k_dec), 0.0)
    o_intra = jnp.einsum('bhnts,bhnsv->bhntv', attn, v_c)
    chunk_decay = jnp.exp(cum[..., -1, :])
    chunk_update = jnp.einsum('bhnsk,bhnsv->bhnkv', k_tail, v_c)

    def step(state, inp):
        dec, upd = inp
        return dec[..., None] * state + upd, state

    s0 = jnp.zeros((bsz, HGRN_HEADS, HGRN_DK, HGRN_DK), jnp.float32)
    _, s_in = lax.scan(step, s0, (jnp.moveaxis(chunk_decay, 2, 0), jnp.moveaxis(chunk_update, 2, 0)))
    s_in = jnp.moveaxis(s_in, 0, 2)
    o = o_intra + jnp.einsum('bhntk,bhnkv->bhntv', q_dec, s_in)
    return o.transpose(0, 2, 3, 1, 4).reshape(bsz, t_len, HGRN_HEADS, HGRN_DK)


def even_mixer(x, w_in, w_out, a_ws, a_bs, a_ln_g, a_ln_b, b_norm_g, lb):
    bsz, t_len, _ = x.shape
    h = x @ w_in
    cuts = [GMLP_WIDTH, 2 * GMLP_WIDTH, 2 * GMLP_WIDTH + HGRN_WIDTH,
            2 * GMLP_WIDTH + 2 * HGRN_WIDTH, 2 * GMLP_WIDTH + 3 * HGRN_WIDTH]
    a_u, a_v, b_q, b_f, b_i, b_g = jnp.split(h, cuts, axis=-1)
    y_a = spatial_gating_unit(jax.nn.gelu(a_u), jax.nn.gelu(a_v), a_ws, a_bs, a_ln_g, a_ln_b)
    o = hgrn2_recurrence(b_q, b_f, jax.nn.silu(b_i), lb)
    gate = jax.nn.silu(b_g.astype(jnp.float32)).reshape(bsz, t_len, HGRN_HEADS, HGRN_DK)
    y_b = (rms_norm(o, b_norm_g.reshape(HGRN_HEADS, HGRN_DK).astype(jnp.float32)) * gate).astype(x.dtype)
    y = jnp.concatenate([y_a, y_b.reshape(bsz, t_len, HGRN_WIDTH)], axis=-1)
    return y @ w_out


def moba_attention(x, w_qkv, w_out):
    bsz, t_len, _ = x.shape
    n_blocks = -(-t_len // MOBA_BLOCK)
    pad = n_blocks * MOBA_BLOCK - t_len
    top_k = min(MOBA_TOPK, max(n_blocks - 1, 1))
    scale = MOBA_HDIM ** -0.5
    slopes = alibi_slopes(MOBA_HEADS)[None, :, None]
    qkv = (x @ w_qkv).reshape(bsz, t_len, 3, MOBA_HEADS, MOBA_HDIM).transpose(2, 0, 3, 1, 4)
    q, k, v = qkv[0], qkv[1], qkv[2]
    pad_cfg = ((0, 0), (0, 0), (0, pad), (0, 0))
    k_blocks = jnp.pad(k, pad_cfg).reshape(bsz, MOBA_HEADS, n_blocks, MOBA_BLOCK, MOBA_HDIM)
    v_blocks = jnp.pad(v, pad_cfg).reshape(bsz, MOBA_HEADS, n_blocks, MOBA_BLOCK, MOBA_HDIM)
    k_mean = jnp.mean(k_blocks.astype(jnp.float32), axis=3)
    q_block = jnp.arange(t_len) // MOBA_BLOCK
    fully_past = jnp.arange(n_blocks)[None, :] < q_block[:, None]
    affinity = jnp.einsum('bhtd,bhnd->bhtn', q.astype(jnp.float32), k_mean)
    affinity = jnp.where(fully_past, affinity, -jnp.inf)
    _, sel = lax.top_k(affinity, top_k)
    sel_valid = sel < q_block[:, None]
    b_idx = jnp.arange(bsz)[:, None, None, None]
    h_idx = jnp.arange(MOBA_HEADS)[None, :, None, None]
    key_offsets = jnp.arange(MOBA_BLOCK)

    def query_chunk(c):
        t0 = c * MOBA_QCHUNK
        q_c = lax.dynamic_slice_in_dim(q, t0, MOBA_QCHUNK, axis=2)
        sel_c = lax.dynamic_slice_in_dim(sel, t0, MOBA_QCHUNK, axis=2)
        valid_c = lax.dynamic_slice_in_dim(sel_valid, t0, MOBA_QCHUNK, axis=2)
        own = t0 // MOBA_BLOCK
        tq = t0 + jnp.arange(MOBA_QCHUNK)
        k_sel = k_blocks[b_idx, h_idx, sel_c]
        v_sel = v_blocks[b_idx, h_idx, sel_c]
        k_own = lax.dynamic_index_in_dim(k_blocks, own, axis=2, keepdims=False)
        v_own = lax.dynamic_index_in_dim(v_blocks, own, axis=2, keepdims=False)
        dist_sel = (tq[:, None, None] - (sel_c[..., None] * MOBA_BLOCK + key_offsets)).astype(jnp.float32)
        s_sel = (jnp.einsum('bhqd,bhqjsd->bhqjs', q_c, k_sel).astype(jnp.float32) * scale
                 - slopes[..., None, None] * dist_sel)
        s_sel = jnp.where(valid_c[..., None], s_sel, -jnp.inf).reshape(bsz, MOBA_HEADS, MOBA_QCHUNK, top_k * MOBA_BLOCK)
        dist_own = (tq[:, None] - (own * MOBA_BLOCK + key_offsets)[None, :]).astype(jnp.float32)
        s_own = (jnp.einsum('bhqd,bhsd->bhqs', q_c, k_own).astype(jnp.float32) * scale
                 - slopes[..., None] * dist_own)
        s_own = jnp.where(dist_own >= 0, s_own, -jnp.inf)
        p = jax.nn.softmax(jnp.concatenate([s_sel, s_own], axis=-1), axis=-1).astype(v.dtype)
        p_sel = p[..., :top_k * MOBA_BLOCK].reshape(bsz, MOBA_HEADS, MOBA_QCHUNK, top_k, MOBA_BLOCK)
        p_own = p[..., top_k * MOBA_BLOCK:]
        return (jnp.einsum('bhqjs,bhqjsd->bhqd', p_sel, v_sel)
                + jnp.einsum('bhqs,bhsd->bhqd', p_own, v_own))

    out = lax.map(query_chunk, jnp.arange(t_len // MOBA_QCHUNK))
    out = out.transpose(1, 0, 3, 2, 4).reshape(bsz, t_len, D_MODEL)
    return out @ w_out


def memory_cross_attention(x, mem, w_q, w_kv, w_o):
    bsz, t_len, _ = x.shape
    q = (x @ w_q).reshape(bsz, t_len, MEM_HEADS, MEM_HDIM)
    kv = (mem @ w_kv).reshape(bsz, mem.shape[1], 2, MEM_HEADS, MEM_HDIM)
    s = jnp.einsum('bthd,bmhd->bhtm', q, kv[:, :, 0]).astype(jnp.float32) * (MEM_HDIM ** -0.5)
    p = jax.nn.softmax(s, axis=-1).astype(x.dtype)
    o = jnp.einsum('bhtm,bmhd->bthd', p, kv[:, :, 1]).reshape(bsz, t_len, D_MODEL)
    return o @ w_o


def swiglu_ffn(x, w_in, w_out):
    gate, up = jnp.split(x @ w_in, 2, axis=-1)
    return (jax.nn.silu(gate) * up) @ w_out


def setup_inputs(seed: int = 0) -> dict:
    key = jax.random.key(seed)
    ks = jax.random.split(key, 24)

    def nrm(k, shape, std):
        return jax.random.normal(k, shape, jnp.float32) * std

    d_inv = D_MODEL ** -0.5
    mix_w = GMLP_WIDTH + HGRN_WIDTH
    return {
        'x': nrm(ks[0], (BATCH, SEQ, D_MODEL), 1.0),
        'mem': nrm(ks[1], (BATCH, MEM_LEN, D_MODEL), 1.0),
        'ln_g': 1.0 + nrm(ks[2], (DEPTH, 3, D_MODEL), 0.02),
        'ln_b': nrm(ks[3], (DEPTH, 3, D_MODEL), 0.02),
        'x_wq': nrm(ks[4], (DEPTH, D_MODEL, D_MODEL), d_inv),
        'x_wkv': jnp.concatenate([nrm(ks[5], (DEPTH, D_MODEL, D_MODEL), d_inv),
                                  nrm(ks[6], (DEPTH, D_MODEL, D_MODEL), d_inv * BETA)], axis=-1),
        'x_wo': nrm(ks[7], (DEPTH, D_MODEL, D_MODEL), d_inv * BETA),
        'ffn_w_in': nrm(ks[8], (DEPTH, D_MODEL, 2 * D_FF), d_inv),
        'ffn_w_out': nrm(ks[9], (DEPTH, D_FF, D_MODEL), D_FF ** -0.5 * BETA),
        'ev_w_in': nrm(ks[10], (N_EVEN, D_MODEL, EVEN_IN_WIDTH), d_inv),
        'ev_w_out': nrm(ks[11], (N_EVEN, mix_w, D_MODEL), mix_w ** -0.5 * BETA),
        'a_ws': nrm(ks[12], (N_EVEN, GMLP_GROUPS, GMLP_CHUNK, GMLP_CHUNK), GMLP_CHUNK ** -0.5),
        'a_bs': 1.0 + nrm(ks[13], (N_EVEN, GMLP_GROUPS, GMLP_CHUNK), 0.1),
        'a_ln_g': 1.0 + nrm(ks[14], (N_EVEN, GMLP_WIDTH), 0.02),
        'a_ln_b': nrm(ks[15], (N_EVEN, GMLP_WIDTH), 0.02),
        'b_norm_g': 1.0 + nrm(ks[16], (N_EVEN, HGRN_WIDTH), 0.02),
        'hgrn_lb_logits': nrm(ks[17], (N_EVEN + 1, HGRN_WIDTH), 0.1),
        'od_w_qkv': jnp.concatenate([nrm(ks[18], (N_ODD, D_MODEL, 2 * D_MODEL), d_inv),
                                     nrm(ks[19], (N_ODD, D_MODEL, D_MODEL), d_inv * BETA)], axis=-1),
        'od_w_out': nrm(ks[20], (N_ODD, D_MODEL, D_MODEL), d_inv * BETA),
    }


def reference(x, mem, ln_g, ln_b, x_wq, x_wkv, x_wo, ffn_w_in, ffn_w_out, ev_w_in, ev_w_out,
              a_ws, a_bs, a_ln_g, a_ln_b, b_norm_g, hgrn_lb_logits, od_w_qkv, od_w_out):
    lb_all = jnp.cumsum(jax.nn.softmax(hgrn_lb_logits.astype(jnp.float32), axis=0), axis=0)
    for l in range(DEPTH):
        j = l // 2
        if l % 2 == 0:
            y = even_mixer(x, ev_w_in[j], ev_w_out[j], a_ws[j], a_bs[j], a_ln_g[j], a_ln_b[j],
                           b_norm_g[j], lb_all[j])
        else:
            y = moba_attention(x, od_w_qkv[j], od_w_out[j])
        x = layer_norm(ALPHA * x + y, ln_g[l, 0], ln_b[l, 0])
        x = layer_norm(ALPHA * x + memory_cross_attention(x, mem, x_wq[l], x_wkv[l], x_wo[l]),
                       ln_g[l, 1], ln_b[l, 1])
        x = layer_norm(ALPHA * x + swiglu_ffn(x, ffn_w_in[l], ffn_w_out[l]), ln_g[l, 2], ln_b[l, 2])
    return x
```

```python
import functools
import math

import jax
import jax.numpy as jnp
from jax import lax
from jax.experimental import pallas as pl
from jax.experimental.pallas import tpu as pltpu

D_MODEL = 1024
DEPTH = 2
ALPHA = (2.0 * DEPTH) ** 0.25
LN_EPS = 1e-5

GMLP_WIDTH = D_MODEL // 2
GMLP_GROUPS = 4
GMLP_GDIM = GMLP_WIDTH // GMLP_GROUPS
GMLP_CHUNK = 128
HGRN_WIDTH = D_MODEL // 2
HGRN_HEADS = 4
HGRN_DK = HGRN_WIDTH // HGRN_HEADS
HGRN_CHUNK = 64
EVEN_IN_WIDTH = 2 * GMLP_WIDTH + 4 * HGRN_WIDTH

MOBA_HEADS = 16
MOBA_HDIM = D_MODEL // MOBA_HEADS
MOBA_BLOCK = 256
MOBA_TOPK = 3
MOBA_PAIR = 2 * MOBA_HDIM

MEM_HEADS = 4
MEM_HDIM = D_MODEL // MEM_HEADS

D_FF = int(math.ceil(8 * D_MODEL / 3 / 256)) * 256
FFN_CHUNK = 256

ROW_TILE = 512
V7X_VMEM_LIMIT = 56 * 1024 * 1024

MASKED = -1e30

BF16 = jnp.bfloat16
F32 = jnp.float32


def _dot(a, b):
    return jnp.dot(a, b, preferred_element_type=F32)


def _dot_nt(a, b):
    return lax.dot_general(a, b, (((1,), (1,)), ((), ())), preferred_element_type=F32)


def _dot_tn(a, b):
    return lax.dot_general(a, b, (((0,), (0,)), ((), ())), preferred_element_type=F32)


def _layer_norm(z, g, b):
    mu = jnp.mean(z, axis=-1, keepdims=True)
    zc = z - mu
    var = jnp.mean(zc * zc, axis=-1, keepdims=True)
    return zc * lax.rsqrt(var + LN_EPS) * g + b


def _resident(shape):
    return pl.BlockSpec(shape, lambda *_: (0,) * len(shape), pipeline_mode=pl.Buffered(1))


def _params(*semantics):
    return pltpu.CompilerParams(dimension_semantics=semantics, vmem_limit_bytes=V7X_VMEM_LIMIT)


def _even_mixer_kernel(x_ref, w_in_ref, w_out_ref, ws_ref, bs_ref, aln_g_ref, aln_b_ref, bnorm_ref,
                       lb_logits_ref, ln_g_ref, ln_b_ref, o_ref,
                       q_s, f_s, i_s, g_s, y_s, state_s, *, lb_index):
    tm = x_ref.shape[1]
    x = x_ref[0]
    xb = x.astype(BF16)

    @pl.when(pl.program_id(1) == 0)
    def _():
        state_s[...] = jnp.zeros_like(state_s)

    u = jax.nn.gelu(_dot(xb, w_in_ref[:, 0:GMLP_WIDTH]))
    v = jax.nn.gelu(_dot(xb, w_in_ref[:, GMLP_WIDTH:2 * GMLP_WIDTH]))
    row = lax.broadcasted_iota(jnp.int32, (GMLP_CHUNK, GMLP_CHUNK), 0)
    col = lax.broadcasted_iota(jnp.int32, (GMLP_CHUNK, GMLP_CHUNK), 1)
    for g in range(GMLP_GROUPS):
        lanes = slice(g * GMLP_GDIM, (g + 1) * GMLP_GDIM)
        vn = _layer_norm(v[:, lanes], aln_g_ref[:, lanes], aln_b_ref[:, lanes]).astype(BF16)
        wg = jnp.where(col <= row, ws_ref[g], 0.0).astype(BF16)
        bias = bs_ref[:, g:g + 1]
        for c in range(tm // GMLP_CHUNK):
            rows = slice(c * GMLP_CHUNK, (c + 1) * GMLP_CHUNK)
            s = _dot(wg, vn[rows]) + bias
            y_s[rows, lanes] = (u[rows, lanes] * s).astype(BF16)

    base = 2 * GMLP_WIDTH
    q_s[...] = _dot(xb, w_in_ref[:, base:base + HGRN_WIDTH])
    f_s[...] = _dot(xb, w_in_ref[:, base + HGRN_WIDTH:base + 2 * HGRN_WIDTH])
    i_s[...] = jax.nn.silu(_dot(xb, w_in_ref[:, base + 2 * HGRN_WIDTH:base + 3 * HGRN_WIDTH]))
    g_s[...] = jax.nn.silu(_dot(xb, w_in_ref[:, base + 3 * HGRN_WIDTH:base + 4 * HGRN_WIDTH]))

    logits = lb_logits_ref[...]
    e = jnp.exp(logits - jnp.max(logits, axis=0, keepdims=True))
    lb = jnp.sum(e[0:lb_index + 1], axis=0, keepdims=True) / jnp.sum(e, axis=0, keepdims=True)

    crow = lax.broadcasted_iota(jnp.int32, (HGRN_CHUNK, HGRN_CHUNK), 0)
    ccol = lax.broadcasted_iota(jnp.int32, (HGRN_CHUNK, HGRN_CHUNK), 1)
    causal = ccol <= crow
    tril_ones = jnp.where(causal, 1.0, 0.0).astype(BF16)
    norm_g = bnorm_ref[...]

    def chunk_body(c, carry):
        rows = pl.ds(pl.multiple_of(c * HGRN_CHUNK, HGRN_CHUNK), HGRN_CHUNK)
        f = lb + (1.0 - lb) * jax.nn.sigmoid(f_s[rows, :])
        logf = jnp.log(f)
        hi = logf.astype(BF16)
        r1 = logf - hi.astype(F32)
        mid = r1.astype(BF16)
        lo = (r1 - mid.astype(F32)).astype(BF16)
        cum = _dot(tril_ones, hi) + _dot(tril_ones, mid) + _dot(tril_ones, lo)
        cum_last = cum[HGRN_CHUNK - 1:HGRN_CHUNK, :]
        k = 1.0 - f
        q_dec = (q_s[rows, :] * jnp.exp(cum)).astype(BF16)
        k_dec = (k * jnp.exp(-cum)).astype(BF16)
        k_tail = (k * jnp.exp(cum_last - cum)).astype(BF16)
        chunk_decay = jnp.exp(cum_last)
        val = i_s[rows, :].astype(BF16)
        gate = g_s[rows, :]
        for h in range(HGRN_HEADS):
            lanes = slice(h * HGRN_DK, (h + 1) * HGRN_DK)
            attn = jnp.where(causal, _dot_nt(q_dec[:, lanes], k_dec[:, lanes]), 0.0).astype(BF16)
            state_t = state_s[h]
            o = _dot(attn, val[:, lanes]) + _dot_nt(q_dec[:, lanes], state_t.astype(BF16))
            state_s[h] = state_t * chunk_decay[:, lanes] + _dot_tn(val[:, lanes], k_tail[:, lanes])
            rms = lax.rsqrt(jnp.mean(o * o, axis=-1, keepdims=True) + LN_EPS)
            y_b = o * rms * norm_g[:, lanes] * gate[:, lanes]
            y_s[rows, GMLP_WIDTH + h * HGRN_DK:GMLP_WIDTH + (h + 1) * HGRN_DK] = y_b.astype(BF16)
        return carry

    lax.fori_loop(0, tm // HGRN_CHUNK, chunk_body, 0)

    y = _dot(y_s[...], w_out_ref[...])
    o_ref[0] = _layer_norm(ALPHA * x + y, ln_g_ref[...], ln_b_ref[...])


def _even_mixer(x, w_in, w_out, ws, bs_t, aln_g, aln_b, bnorm_g, lb_logits, ln_g, ln_b, lb_index):
    bsz, t_len, d = x.shape
    tm = ROW_TILE
    kern = functools.partial(_even_mixer_kernel, lb_index=lb_index)
    return pl.pallas_call(
        kern,
        grid=(bsz, t_len // tm),
        in_specs=[
            pl.BlockSpec((1, tm, d), lambda b, t: (b, t, 0)),
            _resident(w_in.shape), _resident(w_out.shape), _resident(ws.shape), _resident(bs_t.shape),
            _resident(aln_g.shape), _resident(aln_b.shape), _resident(bnorm_g.shape),
            _resident(lb_logits.shape), _resident(ln_g.shape), _resident(ln_b.shape),
        ],
        out_specs=pl.BlockSpec((1, tm, d), lambda b, t: (b, t, 0)),
        out_shape=jax.ShapeDtypeStruct(x.shape, F32),
        scratch_shapes=[
            pltpu.VMEM((tm, HGRN_WIDTH), F32), pltpu.VMEM((tm, HGRN_WIDTH), F32),
            pltpu.VMEM((tm, HGRN_WIDTH), F32), pltpu.VMEM((tm, HGRN_WIDTH), F32),
            pltpu.VMEM((tm, GMLP_WIDTH + HGRN_WIDTH), BF16),
            pltpu.VMEM((HGRN_HEADS, HGRN_DK, HGRN_DK), F32),
        ],
        compiler_params=_params("parallel", "arbitrary"),
        name="even_mixer",
    )(x, w_in, w_out, ws, bs_t, aln_g, aln_b, bnorm_g, lb_logits, ln_g, ln_b)


def _mem_kv_kernel(mem_ref, w_ref, o_ref):
    o_ref[...] = _dot(mem_ref[...].astype(BF16), w_ref[...]).astype(BF16)


def _mem_kv(mem2d, w_kv):
    n, d = mem2d.shape
    tm = ROW_TILE
    return pl.pallas_call(
        _mem_kv_kernel,
        grid=(n // tm,),
        in_specs=[pl.BlockSpec((tm, d), lambda i: (i, 0)), _resident(w_kv.shape)],
        out_specs=pl.BlockSpec((tm, w_kv.shape[1]), lambda i: (i, 0)),
        out_shape=jax.ShapeDtypeStruct((n, w_kv.shape[1]), BF16),
        compiler_params=_params("parallel"),
        name="mem_kv",
    )(mem2d, w_kv)


def _cross_attn_kernel(x_ref, kv_ref, wq_ref, wo_ref, ln_g_ref, ln_b_ref, o_ref, att_s):
    x = x_ref[0]
    q = (_dot(x.astype(BF16), wq_ref[...]) * (MEM_HDIM ** -0.5)).astype(BF16)
    for h in range(MEM_HEADS):
        lanes = slice(h * MEM_HDIM, (h + 1) * MEM_HDIM)
        k_h = kv_ref[0, :, h * MEM_HDIM:(h + 1) * MEM_HDIM]
        v_h = kv_ref[0, :, D_MODEL + h * MEM_HDIM:D_MODEL + (h + 1) * MEM_HDIM]
        s = _dot_nt(q[:, lanes], k_h)
        p = jnp.exp(s - jnp.max(s, axis=-1, keepdims=True))
        denom = jnp.sum(p, axis=-1, keepdims=True)
        att_s[:, lanes] = (_dot(p.astype(BF16), v_h) / denom).astype(BF16)
    y = _dot(att_s[...], wo_ref[...])
    o_ref[0] = _layer_norm(ALPHA * x + y, ln_g_ref[...], ln_b_ref[...])


def _cross_attn(x, kv, wq, wo, ln_g, ln_b):
    bsz, t_len, d = x.shape
    tm = ROW_TILE
    return pl.pallas_call(
        _cross_attn_kernel,
        grid=(bsz, t_len // tm),
        in_specs=[
            pl.BlockSpec((1, tm, d), lambda b, t: (b, t, 0)),
            pl.BlockSpec((1,) + kv.shape[1:], lambda b, t: (b, 0, 0)),
            _resident(wq.shape), _resident(wo.shape), _resident(ln_g.shape), _resident(ln_b.shape),
        ],
        out_specs=pl.BlockSpec((1, tm, d), lambda b, t: (b, t, 0)),
        out_shape=jax.ShapeDtypeStruct(x.shape, F32),
        scratch_shapes=[pltpu.VMEM((tm, d), BF16)],
        compiler_params=_params("parallel", "parallel"),
        name="cross_attn",
    )(x, kv, wq, wo, ln_g, ln_b)


def _ffn_kernel(x_ref, w_in_ref, w_out_ref, ln_g_ref, ln_b_ref, o_ref, acc_s):
    x = x_ref[...]
    xb = x.astype(BF16)
    for c in range(D_FF // FFN_CHUNK):
        cols = slice(c * FFN_CHUNK, (c + 1) * FFN_CHUNK)
        gate = _dot(xb, w_in_ref[:, cols])
        up = _dot(xb, w_in_ref[:, D_FF + c * FFN_CHUNK:D_FF + (c + 1) * FFN_CHUNK])
        act = (jax.nn.silu(gate) * up).astype(BF16)
        part = _dot(act, w_out_ref[cols, :])
        if c == 0:
            acc_s[...] = part
        else:
            acc_s[...] += part
    o_ref[...] = _layer_norm(ALPHA * x + acc_s[...], ln_g_ref[...], ln_b_ref[...])


def _ffn(x2d, w_in, w_out, ln_g, ln_b):
    n, d = x2d.shape
    tm = ROW_TILE
    return pl.pallas_call(
        _ffn_kernel,
        grid=(n // tm,),
        in_specs=[
            pl.BlockSpec((tm, d), lambda i: (i, 0)),
            _resident(w_in.shape), _resident(w_out.shape), _resident(ln_g.shape), _resident(ln_b.shape),
        ],
        out_specs=pl.BlockSpec((tm, d), lambda i: (i, 0)),
        out_shape=jax.ShapeDtypeStruct((n, d), F32),
        scratch_shapes=[pltpu.VMEM((tm, d), F32)],
        compiler_params=_params("parallel"),
        name="ffn",
    )(x2d, w_in, w_out, ln_g, ln_b)


def _moba_qkv_kernel(x_ref, w_qk_ref, w_vt_ref, q_ref, k_ref, vt_ref, kmean_ref):
    tm = x_ref.shape[0]
    xb = x_ref[...].astype(BF16)
    q_ref[...] = (_dot(xb, w_qk_ref[:, 0:D_MODEL]) * (MOBA_HDIM ** -0.5)).astype(BF16)
    k = _dot(xb, w_qk_ref[:, D_MODEL:2 * D_MODEL])
    k_ref[...] = k.astype(BF16)
    for i in range(tm // MOBA_BLOCK):
        kmean_ref[i] = jnp.mean(k[i * MOBA_BLOCK:(i + 1) * MOBA_BLOCK], axis=0, keepdims=True)
    vt_ref[0] = _dot_nt(w_vt_ref[...], xb).astype(BF16)


def _moba_qkv(x2d, w_qk, w_vt, bsz, t_len):
    n, d = x2d.shape
    tm = ROW_TILE
    tiles_per_seq = t_len // tm
    return pl.pallas_call(
        _moba_qkv_kernel,
        grid=(n // tm,),
        in_specs=[pl.BlockSpec((tm, d), lambda i: (i, 0)), _resident(w_qk.shape), _resident(w_vt.shape)],
        out_specs=[
            pl.BlockSpec((tm, d), lambda i: (i, 0)),
            pl.BlockSpec((tm, d), lambda i: (i, 0)),
            pl.BlockSpec((1, d, tm), lambda i: (i // tiles_per_seq, 0, i % tiles_per_seq)),
            pl.BlockSpec((tm // MOBA_BLOCK, 1, d), lambda i: (i, 0, 0)),
        ],
        out_shape=[
            jax.ShapeDtypeStruct((n, d), BF16),
            jax.ShapeDtypeStruct((n, d), BF16),
            jax.ShapeDtypeStruct((bsz, d, t_len), BF16),
            jax.ShapeDtypeStruct((n // MOBA_BLOCK, 1, d), F32),
        ],
        compiler_params=_params("parallel"),
        name="moba_qkv",
    )(x2d, w_qk, w_vt)


def _moba_attn_kernel(x_ref, q_ref, k_ref, vt_ref, kmean_ref, w_out_ref, ln_g_ref, ln_b_ref, o_ref,
                      sel_s, att_s):
    qb = pl.program_id(1)
    n_blocks = kmean_ref.shape[1]
    blk = MOBA_BLOCK
    lane = lax.broadcasted_iota(jnp.int32, (blk, MOBA_PAIR), 1)
    key_i = lax.broadcasted_iota(jnp.int32, (blk, blk), 0)
    qry_i = lax.broadcasted_iota(jnp.int32, (blk, blk), 1)
    rel = (key_i - qry_i).astype(F32)
    causal = key_i <= qry_i
    blk_i = lax.broadcasted_iota(jnp.int32, (n_blocks, blk), 0)
    own_rows = pl.ds(pl.multiple_of(qb * blk, blk), blk)

    for hp in range(MOBA_HEADS // 2):
        lanes = slice(hp * MOBA_PAIR, (hp + 1) * MOBA_PAIR)
        q_pair = q_ref[0, :, lanes]
        q_head = [jnp.where(lane < MOBA_HDIM, q_pair, jnp.zeros_like(q_pair)),
                  jnp.where(lane >= MOBA_HDIM, q_pair, jnp.zeros_like(q_pair))]
        kmean = kmean_ref[0, :, lanes]
        kmean_hi = kmean.astype(BF16)
        kmean_lo = (kmean - kmean_hi.astype(F32)).astype(BF16)
        slopes = [2.0 ** (-8.0 * (2 * hp + e + 1) / MOBA_HEADS) for e in range(2)]

        k_own = k_ref[0, own_rows, lanes]
        vt_own = vt_ref[0, lanes, own_rows]
        carry = []
        for e in range(2):
            aff = _dot_nt(kmean_hi, q_head[e]) + _dot_nt(kmean_lo, q_head[e])
            rank = jnp.zeros((n_blocks, blk), F32)
            for jp in range(n_blocks):
                other = aff[jp:jp + 1, :]
                beats = (other > aff) | ((other == aff) & (jp < blk_i))
                rank = rank + jnp.where(beats, 1.0, 0.0) * (jp < qb).astype(F32)
            chosen = (rank < MOBA_TOPK) & (blk_i < qb)
            sel_s[e] = jnp.where(chosen, 1.0, 0.0)
            s = jnp.where(causal, _dot_nt(k_own, q_head[e]) + slopes[e] * rel, MASKED)
            m = jnp.max(s, axis=0, keepdims=True)
            p = jnp.exp(s - m)
            denom = jnp.sum(p, axis=0, keepdims=True)
            acc = _dot(vt_own[e * MOBA_HDIM:(e + 1) * MOBA_HDIM, :], p.astype(BF16))
            carry += [m, denom, acc]

        def past_block(j, carry):
            rows = pl.ds(pl.multiple_of(j * blk, blk), blk)
            k_blk = k_ref[0, rows, lanes]
            vt_blk = vt_ref[0, lanes, rows]
            offset = ((j - qb) * blk).astype(F32)
            out = []
            for e in range(2):
                m, denom, acc = carry[3 * e:3 * e + 3]
                picked = sel_s[e, pl.ds(j, 1), :] > 0.0
                s = _dot_nt(k_blk, q_head[e]) + slopes[e] * rel
                shift = slopes[e] * offset
                m_blk = jnp.where(picked, jnp.max(s, axis=0, keepdims=True) + shift, MASKED)
                m_new = jnp.maximum(m, m_blk)
                scale_old = jnp.exp(m - m_new)
                p = jnp.exp(s - jnp.where(picked, m_new - shift, -MASKED))
                denom = scale_old * denom + jnp.sum(p, axis=0, keepdims=True)
                acc = scale_old * acc + _dot(vt_blk[e * MOBA_HDIM:(e + 1) * MOBA_HDIM, :], p.astype(BF16))
                out += [m_new, denom, acc]
            return tuple(out)

        carry = lax.fori_loop(0, qb, past_block, tuple(carry))
        for e in range(2):
            _, denom, acc = carry[3 * e:3 * e + 3]
            head = 2 * hp + e
            att_s[head * MOBA_HDIM:(head + 1) * MOBA_HDIM, :] = acc / denom

    att = att_s[...].T.astype(BF16)
    y = _dot(att, w_out_ref[...])
    o_ref[0] = _layer_norm(ALPHA * x_ref[0] + y, ln_g_ref[...], ln_b_ref[...])


def _moba_attn(x, q, k, vt, kmean, w_out, ln_g, ln_b):
    bsz, t_len, d = x.shape
    blk = MOBA_BLOCK
    n_blocks = t_len // blk
    return pl.pallas_call(
        _moba_attn_kernel,
        grid=(bsz, n_blocks),
        in_specs=[
            pl.BlockSpec((1, blk, d), lambda b, t: (b, t, 0)),
            pl.BlockSpec((1, blk, d), lambda b, t: (b, t, 0)),
            pl.BlockSpec((1, t_len, d), lambda b, t: (b, 0, 0)),
            pl.BlockSpec((1, d, t_len), lambda b, t: (b, 0, 0)),
            pl.BlockSpec((1, n_blocks, d), lambda b, t: (b, 0, 0)),
            _resident(w_out.shape), _resident(ln_g.shape), _resident(ln_b.shape),
        ],
        out_specs=pl.BlockSpec((1, blk, d), lambda b, t: (b, t, 0)),
        out_shape=jax.ShapeDtypeStruct(x.shape, F32),
        scratch_shapes=[pltpu.VMEM((2, n_blocks, blk), F32), pltpu.VMEM((d, blk), F32)],
        compiler_params=_params("parallel", "arbitrary"),
        name="moba_attn",
    )(x, q, k, vt, kmean, w_out, ln_g, ln_b)


def kernel(x, mem, ln_g, ln_b, x_wq, x_wkv, x_wo, ffn_w_in, ffn_w_out, ev_w_in, ev_w_out, a_ws, a_bs,
           a_ln_g, a_ln_b, b_norm_g, hgrn_lb_logits, od_w_qkv, od_w_out):
    bsz, t_len, d = x.shape
    assert d == D_MODEL and t_len % ROW_TILE == 0 and t_len % MOBA_BLOCK == 0
    assert ROW_TILE % MOBA_BLOCK == 0 and t_len // MOBA_BLOCK > 1
    n = bsz * t_len
    mem2d = mem.reshape(bsz * mem.shape[1], d)

    def row(v):
        return v.reshape(1, -1)

    for layer in range(DEPTH):
        j = layer // 2
        if layer % 2 == 0:
            x = _even_mixer(
                x, ev_w_in[j].astype(BF16), ev_w_out[j].astype(BF16), a_ws[j], a_bs[j].T,
                row(a_ln_g[j]), row(a_ln_b[j]), row(b_norm_g[j]), hgrn_lb_logits,
                row(ln_g[layer, 0]), row(ln_b[layer, 0]), j)
        else:
            w_qkv = od_w_qkv[j]
            q, k, vt, kmean = _moba_qkv(
                x.reshape(n, d), w_qkv[:, :2 * d].astype(BF16), w_qkv[:, 2 * d:].T.astype(BF16), bsz, t_len)
            x = _moba_attn(
                x, q.reshape(bsz, t_len, d), k.reshape(bsz, t_len, d), vt,
                kmean.reshape(bsz, t_len // MOBA_BLOCK, d), od_w_out[j].astype(BF16),
                row(ln_g[layer, 0]), row(ln_b[layer, 0]))
        kv = _mem_kv(mem2d, x_wkv[layer].astype(BF16)).reshape(bsz, mem.shape[1], 2 * d)
        x = _cross_attn(x, kv, x_wq[layer].astype(BF16), x_wo[layer].astype(BF16),
                        row(ln_g[layer, 1]), row(ln_b[layer, 1]))
        x = _ffn(x.reshape(n, d), ffn_w_in[layer].astype(BF16), ffn_w_out[layer].astype(BF16),
                 row(ln_g[layer, 2]), row(ln_b[layer, 2])).reshape(bsz, t_len, d)
    return x
```

```python
import functools
import math

import jax
import jax.numpy as jnp
from jax import lax
from jax.experimental import pallas as pl
from jax.experimental.pallas import tpu as pltpu

D_MODEL = 1024
DEPTH = 2
ALPHA = (2.0 * DEPTH) ** 0.25
LN_EPS = 1e-5

GMLP_WIDTH = D_MODEL // 2
GMLP_GROUPS = 4
GMLP_GDIM = GMLP_WIDTH // GMLP_GROUPS
GMLP_CHUNK = 128
HGRN_WIDTH = D_MODEL // 2
HGRN_HEADS = 4
HGRN_DK = HGRN_WIDTH // HGRN_HEADS
HGRN_CHUNK = 64
EVEN_IN_WIDTH = 2 * GMLP_WIDTH + 4 * HGRN_WIDTH

MOBA_HEADS = 16
MOBA_HDIM = D_MODEL // MOBA_HEADS
MOBA_BLOCK = 256
MOBA_TOPK = 3
MOBA_PAIR = 2 * MOBA_HDIM

MEM_HEADS = 4
MEM_HDIM = D_MODEL // MEM_HEADS

D_FF = int(math.ceil(8 * D_MODEL / 3 / 256)) * 256
FFN_CHUNK = 256

ROW_TILE = 512
V7X_VMEM_LIMIT = 56 * 1024 * 1024

MASKED = -1e30

BF16 = jnp.bfloat16
F32 = jnp.float32


def _dot(a, b):
    return jnp.dot(a, b, preferred_element_type=F32)


def _dot_nt(a, b):
    return lax.dot_general(a, b, (((1,), (1,)), ((), ())), preferred_element_type=F32)


def _dot_tn(a, b):
    return lax.dot_general(a, b, (((0,), (0,)), ((), ())), preferred_element_type=F32)


def _layer_norm(z, g, b):
    mu = jnp.mean(z, axis=-1, keepdims=True)
    zc = z - mu
    var = jnp.mean(zc * zc, axis=-1, keepdims=True)
    return zc * lax.rsqrt(var + LN_EPS) * g + b


def _resident(shape):
    return pl.BlockSpec(shape, lambda *_: (0,) * len(shape), pipeline_mode=pl.Buffered(1))


def _params(*semantics):
    return pltpu.CompilerParams(dimension_semantics=semantics, vmem_limit_bytes=V7X_VMEM_LIMIT)


def _even_mixer_kernel(x_ref, w_in_ref, w_out_ref, ws_ref, bs_ref, aln_g_ref, aln_b_ref, bnorm_ref,
                       lb_logits_ref, ln_g_ref, ln_b_ref, o_ref,
                       q_s, f_s, i_s, g_s, y_s, state_s, *, lb_index):
    tm = x_ref.shape[1]
    x = x_ref[0]
    xb = x.astype(BF16)

    @pl.when(pl.program_id(1) == 0)
    def _():
        state_s[...] = jnp.zeros_like(state_s)

    u = jax.nn.gelu(_dot(xb, w_in_ref[:, 0:GMLP_WIDTH]))
    v = jax.nn.gelu(_dot(xb, w_in_ref[:, GMLP_WIDTH:2 * GMLP_WIDTH]))
    row = lax.broadcasted_iota(jnp.int32, (GMLP_CHUNK, GMLP_CHUNK), 0)
    col = lax.broadcasted_iota(jnp.int32, (GMLP_CHUNK, GMLP_CHUNK), 1)
    for g in range(GMLP_GROUPS):
        lanes = slice(g * GMLP_GDIM, (g + 1) * GMLP_GDIM)
        vn = _layer_norm(v[:, lanes], aln_g_ref[:, lanes], aln_b_ref[:, lanes]).astype(BF16)
        wg = jnp.where(col <= row, ws_ref[g], 0.0).astype(BF16)
        bias = bs_ref[:, g:g + 1]
        for c in range(tm // GMLP_CHUNK):
            rows = slice(c * GMLP_CHUNK, (c + 1) * GMLP_CHUNK)
            s = _dot(wg, vn[rows]) + bias
            y_s[rows, lanes] = (u[rows, lanes] * s).astype(BF16)

    base = 2 * GMLP_WIDTH
    q_s[...] = _dot(xb, w_in_ref[:, base:base + HGRN_WIDTH])
    f_s[...] = _dot(xb, w_in_ref[:, base + HGRN_WIDTH:base + 2 * HGRN_WIDTH])
    i_s[...] = jax.nn.silu(_dot(xb, w_in_ref[:, base + 2 * HGRN_WIDTH:base + 3 * HGRN_WIDTH]))
    g_s[...] = jax.nn.silu(_dot(xb, w_in_ref[:, base + 3 * HGRN_WIDTH:base + 4 * HGRN_WIDTH]))

    logits = lb_logits_ref[...]
    e = jnp.exp(logits - jnp.max(logits, axis=0, keepdims=True))
    lb = jnp.sum(e[0:lb_index + 1], axis=0, keepdims=True) / jnp.sum(e, axis=0, keepdims=True)

    crow = lax.broadcasted_iota(jnp.int32, (HGRN_CHUNK, HGRN_CHUNK), 0)
    ccol = lax.broadcasted_iota(jnp.int32, (HGRN_CHUNK, HGRN_CHUNK), 1)
    causal = ccol <= crow
    tril_ones = jnp.where(causal, 1.0, 0.0).astype(BF16)
    norm_g = bnorm_ref[...]

    def chunk_body(c, carry):
        rows = pl.ds(pl.multiple_of(c * HGRN_CHUNK, HGRN_CHUNK), HGRN_CHUNK)
        f = lb + (1.0 - lb) * jax.nn.sigmoid(f_s[rows, :])
        logf = jnp.log(f)
        hi = logf.astype(BF16)
        r1 = logf - hi.astype(F32)
        mid = r1.astype(BF16)
        lo = (r1 - mid.astype(F32)).astype(BF16)
        cum = _dot(tril_ones, hi) + _dot(tril_ones, mid) + _dot(tril_ones, lo)
        cum_last = cum[HGRN_CHUNK - 1:HGRN_CHUNK, :]
        k = 1.0 - f
        q_dec = (q_s[rows, :] * jnp.exp(cum)).astype(BF16)
        k_dec = (k * jnp.exp(-cum)).astype(BF16)
        k_tail = (k * jnp.exp(cum_last - cum)).astype(BF16)
        chunk_decay = jnp.exp(cum_last)
        val = i_s[rows, :].astype(BF16)
        gate = g_s[rows, :]
        for h in range(HGRN_HEADS):
            lanes = slice(h * HGRN_DK, (h + 1) * HGRN_DK)
            attn = jnp.where(causal, _dot_nt(q_dec[:, lanes], k_dec[:, lanes]), 0.0).astype(BF16)
            state_t = state_s[h]
            o = _dot(attn, val[:, lanes]) + _dot_nt(q_dec[:, lanes], state_t.astype(BF16))
            state_s[h] = state_t * chunk_decay[:, lanes] + _dot_tn(val[:, lanes], k_tail[:, lanes])
            rms = lax.rsqrt(jnp.mean(o * o, axis=-1, keepdims=True) + LN_EPS)
            y_b = o * rms * norm_g[:, lanes] * gate[:, lanes]
            y_s[rows, GMLP_WIDTH + h * HGRN_DK:GMLP_WIDTH + (h + 1) * HGRN_DK] = y_b.astype(BF16)
        return carry

    lax.fori_loop(0, tm // HGRN_CHUNK, chunk_body, 0)

    y = _dot(y_s[...], w_out_ref[...])
    o_ref[0] = _layer_norm(ALPHA * x + y, ln_g_ref[...], ln_b_ref[...])


def _even_mixer(x, w_in, w_out, ws, bs_t, aln_g, aln_b, bnorm_g, lb_logits, ln_g, ln_b, lb_index):
    bsz, t_len, d = x.shape
    tm = ROW_TILE
    kern = functools.partial(_even_mixer_kernel, lb_index=lb_index)
    return pl.pallas_call(
        kern,
        grid=(bsz, t_len // tm),
        in_specs=[
            pl.BlockSpec((1, tm, d), lambda b, t: (b, t, 0)),
            _resident(w_in.shape), _resident(w_out.shape), _resident(ws.shape), _resident(bs_t.shape),
            _resident(aln_g.shape), _resident(aln_b.shape), _resident(bnorm_g.shape),
            _resident(lb_logits.shape), _resident(ln_g.shape), _resident(ln_b.shape),
        ],
        out_specs=pl.BlockSpec((1, tm, d), lambda b, t: (b, t, 0)),
        out_shape=jax.ShapeDtypeStruct(x.shape, F32),
        scratch_shapes=[
            pltpu.VMEM((tm, HGRN_WIDTH), F32), pltpu.VMEM((tm, HGRN_WIDTH), F32),
            pltpu.VMEM((tm, HGRN_WIDTH), F32), pltpu.VMEM((tm, HGRN_WIDTH), F32),
            pltpu.VMEM((tm, GMLP_WIDTH + HGRN_WIDTH), BF16),
            pltpu.VMEM((HGRN_HEADS, HGRN_DK, HGRN_DK), F32),
        ],
        compiler_params=_params("parallel", "arbitrary"),
        name="even_mixer",
    )(x, w_in, w_out, ws, bs_t, aln_g, aln_b, bnorm_g, lb_logits, ln_g, ln_b)


def _mem_kv_kernel(mem_ref, w_ref, o_ref):
    o_ref[...] = _dot(mem_ref[...].astype(BF16), w_ref[...]).astype(BF16)


def _mem_kv(mem2d, w_kv):
    n, d = mem2d.shape
    tm = ROW_TILE
    return pl.pallas_call(
        _mem_kv_kernel,
        grid=(n // tm,),
        in_specs=[pl.BlockSpec((tm, d), lambda i: (i, 0)), _resident(w_kv.shape)],
        out_specs=pl.BlockSpec((tm, w_kv.shape[1]), lambda i: (i, 0)),
        out_shape=jax.ShapeDtypeStruct((n, w_kv.shape[1]), BF16),
        compiler_params=_params("parallel"),
        name="mem_kv",
    )(mem2d, w_kv)


def _cross_attn_kernel(x_ref, kv_ref, wq_ref, wo_ref, ln_g_ref, ln_b_ref, o_ref, att_s):
    x = x_ref[0]
    q = (_dot(x.astype(BF16), wq_ref[...]) * (MEM_HDIM ** -0.5)).astype(BF16)
    for h in range(MEM_HEADS):
        lanes = slice(h * MEM_HDIM, (h + 1) * MEM_HDIM)
        k_h = kv_ref[0, :, h * MEM_HDIM:(h + 1) * MEM_HDIM]
        v_h = kv_ref[0, :, D_MODEL + h * MEM_HDIM:D_MODEL + (h + 1) * MEM_HDIM]
        s = _dot_nt(q[:, lanes], k_h)
        p = jnp.exp(s - jnp.max(s, axis=-1, keepdims=True))
        denom = jnp.sum(p, axis=-1, keepdims=True)
        att_s[:, lanes] = (_dot(p.astype(BF16), v_h) / denom).astype(BF16)
    y = _dot(att_s[...], wo_ref[...])
    o_ref[0] = _layer_norm(ALPHA * x + y, ln_g_ref[...], ln_b_ref[...])


def _cross_attn(x, kv, wq, wo, ln_g, ln_b):
    bsz, t_len, d = x.shape
    tm = ROW_TILE
    return pl.pallas_call(
        _cross_attn_kernel,
        grid=(bsz, t_len // tm),
        in_specs=[
            pl.BlockSpec((1, tm, d), lambda b, t: (b, t, 0)),
            pl.BlockSpec((1,) + kv.shape[1:], lambda b, t: (b, 0, 0)),
            _resident(wq.shape), _resident(wo.shape), _resident(ln_g.shape), _resident(ln_b.shape),
        ],
        out_specs=pl.BlockSpec((1, tm, d), lambda b, t: (b, t, 0)),
        out_shape=jax.ShapeDtypeStruct(x.shape, F32),
        scratch_shapes=[pltpu.VMEM((tm, d), BF16)],
        compiler_params=_params("parallel", "parallel"),
        name="cross_attn",
    )(x, kv, wq, wo, ln_g, ln_b)


def _ffn_kernel(x_ref, w_in_ref, w_out_ref, ln_g_ref, ln_b_ref, o_ref, acc_s):
    x = x_ref[...]
    xb = x.astype(BF16)
    for c in range(D_FF // FFN_CHUNK):
        cols = slice(c * FFN_CHUNK, (c + 1) * FFN_CHUNK)
        gate = _dot(xb, w_in_ref[:, cols])
        up = _dot(xb, w_in_ref[:, D_FF + c * FFN_CHUNK:D_FF + (c + 1) * FFN_CHUNK])
        act = (jax.nn.silu(gate) * up).astype(BF16)
        part = _dot(act, w_out_ref[cols, :])
        if c == 0:
            acc_s[...] = part
        else:
            acc_s[...] += part
    o_ref[...] = _layer_norm(ALPHA * x + acc_s[...], ln_g_ref[...], ln_b_ref[...])


def _ffn(x2d, w_in, w_out, ln_g, ln_b):
    n, d = x2d.shape
    tm = ROW_TILE
    return pl.pallas_call(
        _ffn_kernel,
        grid=(n // tm,),
        in_specs=[
            pl.BlockSpec((tm, d), lambda i: (i, 0)),
            _resident(w_in.shape), _resident(w_out.shape), _resident(ln_g.shape), _resident(ln_b.shape),
        ],
        out_specs=pl.BlockSpec((tm, d), lambda i: (i, 0)),
        out_shape=jax.ShapeDtypeStruct((n, d), F32),
        scratch_shapes=[pltpu.VMEM((tm, d), F32)],
        compiler_params=_params("parallel"),
        name="ffn",
    )(x2d, w_in, w_out, ln_g, ln_b)


def _moba_qkv_kernel(x_ref, w_qk_ref, w_vt_ref, q_ref, k_ref, vt_ref, kmean_ref):
    tm = x_ref.shape[0]
    xb = x_ref[...].astype(BF16)
    q_ref[...] = (_dot(xb, w_qk_ref[:, 0:D_MODEL]) * (MOBA_HDIM ** -0.5)).astype(BF16)
    k = _dot(xb, w_qk_ref[:, D_MODEL:2 * D_MODEL])
    k_ref[...] = k.astype(BF16)
    for i in range(tm // MOBA_BLOCK):
        kmean_ref[i] = jnp.mean(k[i * MOBA_BLOCK:(i + 1) * MOBA_BLOCK], axis=0, keepdims=True)
    vt_ref[0] = _dot_nt(w_vt_ref[...], xb).astype(BF16)


def _moba_qkv(x2d, w_qk, w_vt, bsz, t_len):
    n, d = x2d.shape
    tm = ROW_TILE
    tiles_per_seq = t_len // tm
    return pl.pallas_call(
        _moba_qkv_kernel,
        grid=(n // tm,),
        in_specs=[pl.BlockSpec((tm, d), lambda i: (i, 0)), _resident(w_qk.shape), _resident(w_vt.shape)],
        out_specs=[
            pl.BlockSpec((tm, d), lambda i: (i, 0)),
            pl.BlockSpec((tm, d), lambda i: (i, 0)),
            pl.BlockSpec((1, d, tm), lambda i: (i // tiles_per_seq, 0, i % tiles_per_seq)),
            pl.BlockSpec((tm // MOBA_BLOCK, 1, d), lambda i: (i, 0, 0)),
        ],
        out_shape=[
            jax.ShapeDtypeStruct((n, d), BF16),
            jax.ShapeDtypeStruct((n, d), BF16),
            jax.ShapeDtypeStruct((bsz, d, t_len), BF16),
            jax.ShapeDtypeStruct((n // MOBA_BLOCK, 1, d), F32),
        ],
        compiler_params=_params("parallel"),
        name="moba_qkv",
    )(x2d, w_qk, w_vt)


def _moba_attn_kernel(slopes_ref, x_ref, q_ref, k_ref, vt_ref, kmean_ref, w_out_ref, ln_g_ref, ln_b_ref,
                      o_ref, sa_s, sb_s, p_s, att_s):
    qb = pl.program_id(1)
    n_blocks = kmean_ref.shape[1]
    blk = MOBA_BLOCK
    lane = lax.broadcasted_iota(jnp.int32, (blk, MOBA_PAIR), 1)
    key_i = lax.broadcasted_iota(jnp.int32, (blk, blk), 0)
    qry_i = lax.broadcasted_iota(jnp.int32, (blk, blk), 1)
    rel = (key_i - qry_i).astype(F32)
    causal = key_i <= qry_i
    blk_i = lax.broadcasted_iota(jnp.int32, (n_blocks, blk), 0)

    def attend(n_past):
        n_keys = (n_past + 1) * blk
        select = n_past > MOBA_TOPK

        def pair_lanes(hp):
            return pl.ds(pl.multiple_of(hp * MOBA_PAIR, MOBA_PAIR), MOBA_PAIR)

        def split_heads(hp):
            q_pair = q_ref[0, :, pair_lanes(hp)]
            zero = jnp.zeros_like(q_pair)
            return [jnp.where(lane < MOBA_HDIM, q_pair, zero), jnp.where(lane >= MOBA_HDIM, q_pair, zero)]

        def scores(hp, s_buf):
            q_both = jnp.concatenate(split_heads(hp), axis=0)
            s_buf[0:n_keys, :] = _dot_nt(k_ref[0, 0:n_keys, pair_lanes(hp)], q_both)

        def softmax_pv(hp, s_buf):
            if select:
                q_heads = split_heads(hp)
                kmean = kmean_ref[0, :, pair_lanes(hp)]
                kmean_hi = kmean.astype(BF16)
                kmean_lo = (kmean - kmean_hi.astype(F32)).astype(BF16)
            for e in range(2):
                cols = slice(e * blk, (e + 1) * blk)
                slope = slopes_ref[2 * hp + e]
                alibi = slope * rel
                if select:
                    aff = _dot_nt(kmean_hi, q_heads[e]) + _dot_nt(kmean_lo, q_heads[e])
                    rank = jnp.zeros((n_blocks, blk), F32)
                    for jp in range(n_past):
                        other = aff[jp:jp + 1, :]
                        beats = (other > aff) | ((other == aff) & (jp < blk_i))
                        rank = rank + jnp.where(beats, 1.0, 0.0)
                    chosen = rank < MOBA_TOPK
                m = None
                for j in range(n_past + 1):
                    rows = slice(j * blk, (j + 1) * blk)
                    t = s_buf[rows, cols] + alibi
                    if j == n_past:
                        t = jnp.where(causal, t, MASKED)
                    s_buf[rows, cols] = t
                    m_j = jnp.max(t, axis=0, keepdims=True) + slope * float(j * blk)
                    if select and j < n_past:
                        m_j = jnp.where(chosen[j:j + 1, :], m_j, MASKED)
                    m = m_j if m is None else jnp.maximum(m, m_j)
                denom = jnp.zeros((1, blk), F32)
                for j in range(n_past + 1):
                    rows = slice(j * blk, (j + 1) * blk)
                    shift = m - slope * float(j * blk)
                    if select and j < n_past:
                        shift = jnp.where(chosen[j:j + 1, :], shift, -MASKED)
                    p = jnp.exp(s_buf[rows, cols] - shift)
                    denom = denom + jnp.sum(p, axis=0, keepdims=True)
                    p_s[e, rows, :] = p.astype(BF16)
                feat = pl.ds(pl.multiple_of(hp * MOBA_PAIR + e * MOBA_HDIM, MOBA_HDIM), MOBA_HDIM)
                acc = _dot(vt_ref[0, feat, 0:n_keys], p_s[e, 0:n_keys, :])
                att_s[feat, :] = acc / denom

        n_pairs = MOBA_HEADS // 2
        scores(0, sa_s)

        def two_pairs(i, carry):
            hp = 2 * i
            scores(hp + 1, sb_s)
            softmax_pv(hp, sa_s)
            scores(jnp.minimum(hp + 2, n_pairs - 1), sa_s)
            softmax_pv(hp + 1, sb_s)
            return carry

        lax.fori_loop(0, n_pairs // 2, two_pairs, 0)

    for n_past in range(n_blocks):
        pl.when(qb == n_past)(functools.partial(attend, n_past))

    att = att_s[...].T.astype(BF16)
    y = _dot(att, w_out_ref[...])
    o_ref[0] = _layer_norm(ALPHA * x_ref[0] + y, ln_g_ref[...], ln_b_ref[...])


def _moba_attn(x, q, k, vt, kmean, w_out, ln_g, ln_b):
    bsz, t_len, d = x.shape
    blk = MOBA_BLOCK
    n_blocks = t_len // blk
    slopes = jnp.asarray([2.0 ** (-8.0 * (h + 1) / MOBA_HEADS) for h in range(MOBA_HEADS)], F32)
    return pl.pallas_call(
        _moba_attn_kernel,
        grid=(bsz, n_blocks),
        in_specs=[
            pl.BlockSpec(memory_space=pltpu.SMEM),
            pl.BlockSpec((1, blk, d), lambda b, t: (b, t, 0)),
            pl.BlockSpec((1, blk, d), lambda b, t: (b, t, 0)),
            pl.BlockSpec((1, t_len, d), lambda b, t: (b, 0, 0)),
            pl.BlockSpec((1, d, t_len), lambda b, t: (b, 0, 0)),
            pl.BlockSpec((1, n_blocks, d), lambda b, t: (b, 0, 0)),
            _resident(w_out.shape), _resident(ln_g.shape), _resident(ln_b.shape),
        ],
        out_specs=pl.BlockSpec((1, blk, d), lambda b, t: (b, t, 0)),
        out_shape=jax.ShapeDtypeStruct(x.shape, F32),
        scratch_shapes=[
            pltpu.VMEM((t_len, 2 * blk), F32),
            pltpu.VMEM((t_len, 2 * blk), F32),
            pltpu.VMEM((2, t_len, blk), BF16),
            pltpu.VMEM((d, blk), F32),
        ],
        compiler_params=_params("parallel", "arbitrary"),
        name="moba_attn",
    )(slopes, x, q, k, vt, kmean, w_out, ln_g, ln_b)


def kernel(x, mem, ln_g, ln_b, x_wq, x_wkv, x_wo, ffn_w_in, ffn_w_out, ev_w_in, ev_w_out, a_ws, a_bs,
           a_ln_g, a_ln_b, b_norm_g, hgrn_lb_logits, od_w_qkv, od_w_out):
    bsz, t_len, d = x.shape
    assert d == D_MODEL and t_len % ROW_TILE == 0 and t_len % MOBA_BLOCK == 0
    assert ROW_TILE % MOBA_BLOCK == 0 and t_len // MOBA_BLOCK > 1
    n = bsz * t_len
    mem2d = mem.reshape(bsz * mem.shape[1], d)

    def row(v):
        return v.reshape(1, -1)

    for layer in range(DEPTH):
        j = layer // 2
        if layer % 2 == 0:
            x = _even_mixer(
                x, ev_w_in[j].astype(BF16), ev_w_out[j].astype(BF16), a_ws[j], a_bs[j].T,
                row(a_ln_g[j]), row(a_ln_b[j]), row(b_norm_g[j]), hgrn_lb_logits,
                row(ln_g[layer, 0]), row(ln_b[layer, 0]), j)
        else:
            w_qkv = od_w_qkv[j]
            q, k, vt, kmean = _moba_qkv(
                x.reshape(n, d), w_qkv[:, :2 * d].astype(BF16), w_qkv[:, 2 * d:].T.astype(BF16), bsz, t_len)
            x = _moba_attn(
                x, q.reshape(bsz, t_len, d), k.reshape(bsz, t_len, d), vt,
                kmean.reshape(bsz, t_len // MOBA_BLOCK, d), od_w_out[j].astype(BF16),
                row(ln_g[layer, 0]), row(ln_b[layer, 0]))
        kv = _mem_kv(mem2d, x_wkv[layer].astype(BF16)).reshape(bsz, mem.shape[1], 2 * d)
        x = _cross_attn(x, kv, x_wq[layer].astype(BF16), x_wo[layer].astype(BF16),
                        row(ln_g[layer, 1]), row(ln_b[layer, 1]))
        x = _ffn(x.reshape(n, d), ffn_w_in[layer].astype(BF16), ffn_w_out[layer].astype(BF16),
                 row(ln_g[layer, 2]), row(ln_b[layer, 2])).reshape(bsz, t_len, d)
    return x
```

```python
import functools
import math

import jax
import jax.numpy as jnp
from jax import lax
from jax.experimental import pallas as pl
from jax.experimental.pallas import tpu as pltpu

D_MODEL = 1024
DEPTH = 2
ALPHA = (2.0 * DEPTH) ** 0.25
LN_EPS = 1e-5

GMLP_WIDTH = D_MODEL // 2
GMLP_GROUPS = 4
GMLP_GDIM = GMLP_WIDTH // GMLP_GROUPS
GMLP_CHUNK = 128
HGRN_WIDTH = D_MODEL // 2
HGRN_HEADS = 4
HGRN_DK = HGRN_WIDTH // HGRN_HEADS
HGRN_CHUNK = 64
EVEN_IN_WIDTH = 2 * GMLP_WIDTH + 4 * HGRN_WIDTH

MOBA_HEADS = 16
MOBA_HDIM = D_MODEL // MOBA_HEADS
MOBA_BLOCK = 256
MOBA_TOPK = 3
MOBA_PAIR = 2 * MOBA_HDIM

MEM_HEADS = 4
MEM_HDIM = D_MODEL // MEM_HEADS

D_FF = int(math.ceil(8 * D_MODEL / 3 / 256)) * 256
FFN_CHUNK = 256

ROW_TILE = 512
V7X_VMEM_LIMIT = 56 * 1024 * 1024

MASKED = -1e30

BF16 = jnp.bfloat16
F32 = jnp.float32


def _dot(a, b):
    return jnp.dot(a, b, preferred_element_type=F32)


def _dot_nt(a, b):
    return lax.dot_general(a, b, (((1,), (1,)), ((), ())), preferred_element_type=F32)


def _dot_tn(a, b):
    return lax.dot_general(a, b, (((0,), (0,)), ((), ())), preferred_element_type=F32)


def _layer_norm(z, g, b):
    mu = jnp.mean(z, axis=-1, keepdims=True)
    zc = z - mu
    var = jnp.mean(zc * zc, axis=-1, keepdims=True)
    return zc * lax.rsqrt(var + LN_EPS) * g + b


def _resident(shape):
    return pl.BlockSpec(shape, lambda *_: (0,) * len(shape), pipeline_mode=pl.Buffered(1))


def _params(*semantics):
    return pltpu.CompilerParams(dimension_semantics=semantics, vmem_limit_bytes=V7X_VMEM_LIMIT)


def _even_mixer_kernel(x_ref, w_in_ref, w_out_ref, ws_ref, bs_ref, aln_g_ref, aln_b_ref, bnorm_ref,
                       lb_logits_ref, ln_g_ref, ln_b_ref, o_ref,
                       q_s, f_s, i_s, g_s, y_s, state_s, *, lb_index):
    tm = x_ref.shape[1]
    x = x_ref[0]
    xb = x.astype(BF16)

    @pl.when(pl.program_id(1) == 0)
    def _():
        state_s[...] = jnp.zeros_like(state_s)

    u = jax.nn.gelu(_dot(xb, w_in_ref[:, 0:GMLP_WIDTH]))
    v = jax.nn.gelu(_dot(xb, w_in_ref[:, GMLP_WIDTH:2 * GMLP_WIDTH]))
    row = lax.broadcasted_iota(jnp.int32, (GMLP_CHUNK, GMLP_CHUNK), 0)
    col = lax.broadcasted_iota(jnp.int32, (GMLP_CHUNK, GMLP_CHUNK), 1)
    for g in range(GMLP_GROUPS):
        lanes = slice(g * GMLP_GDIM, (g + 1) * GMLP_GDIM)
        vn = _layer_norm(v[:, lanes], aln_g_ref[:, lanes], aln_b_ref[:, lanes]).astype(BF16)
        wg = jnp.where(col <= row, ws_ref[g], 0.0).astype(BF16)
        bias = bs_ref[:, g:g + 1]
        for c in range(tm // GMLP_CHUNK):
            rows = slice(c * GMLP_CHUNK, (c + 1) * GMLP_CHUNK)
            s = _dot(wg, vn[rows]) + bias
            y_s[rows, lanes] = (u[rows, lanes] * s).astype(BF16)

    base = 2 * GMLP_WIDTH
    q_s[...] = _dot(xb, w_in_ref[:, base:base + HGRN_WIDTH])
    f_s[...] = _dot(xb, w_in_ref[:, base + HGRN_WIDTH:base + 2 * HGRN_WIDTH])
    i_s[...] = jax.nn.silu(_dot(xb, w_in_ref[:, base + 2 * HGRN_WIDTH:base + 3 * HGRN_WIDTH]))
    g_s[...] = jax.nn.silu(_dot(xb, w_in_ref[:, base + 3 * HGRN_WIDTH:base + 4 * HGRN_WIDTH]))

    logits = lb_logits_ref[...]
    e = jnp.exp(logits - jnp.max(logits, axis=0, keepdims=True))
    lb = jnp.sum(e[0:lb_index + 1], axis=0, keepdims=True) / jnp.sum(e, axis=0, keepdims=True)

    crow = lax.broadcasted_iota(jnp.int32, (HGRN_CHUNK, HGRN_CHUNK), 0)
    ccol = lax.broadcasted_iota(jnp.int32, (HGRN_CHUNK, HGRN_CHUNK), 1)
    causal = ccol <= crow
    tril_ones = jnp.where(causal, 1.0, 0.0).astype(BF16)
    norm_g = bnorm_ref[...]

    def chunk_body(c, carry):
        rows = pl.ds(pl.multiple_of(c * HGRN_CHUNK, HGRN_CHUNK), HGRN_CHUNK)
        f = lb + (1.0 - lb) * jax.nn.sigmoid(f_s[rows, :])
        logf = jnp.log(f)
        hi = logf.astype(BF16)
        r1 = logf - hi.astype(F32)
        mid = r1.astype(BF16)
        lo = (r1 - mid.astype(F32)).astype(BF16)
        cum = _dot(tril_ones, hi) + _dot(tril_ones, mid) + _dot(tril_ones, lo)
        cum_last = cum[HGRN_CHUNK - 1:HGRN_CHUNK, :]
        k = 1.0 - f
        q_dec = (q_s[rows, :] * jnp.exp(cum)).astype(BF16)
        k_dec = (k * jnp.exp(-cum)).astype(BF16)
        k_tail = (k * jnp.exp(cum_last - cum)).astype(BF16)
        chunk_decay = jnp.exp(cum_last)
        val = i_s[rows, :].astype(BF16)
        gate = g_s[rows, :]
        for h in range(HGRN_HEADS):
            lanes = slice(h * HGRN_DK, (h + 1) * HGRN_DK)
            attn = jnp.where(causal, _dot_nt(q_dec[:, lanes], k_dec[:, lanes]), 0.0).astype(BF16)
            state_t = state_s[h]
            o = _dot(attn, val[:, lanes]) + _dot_nt(q_dec[:, lanes], state_t.astype(BF16))
            state_s[h] = state_t * chunk_decay[:, lanes] + _dot_tn(val[:, lanes], k_tail[:, lanes])
            rms = lax.rsqrt(jnp.mean(o * o, axis=-1, keepdims=True) + LN_EPS)
            y_b = o * rms * norm_g[:, lanes] * gate[:, lanes]
            y_s[rows, GMLP_WIDTH + h * HGRN_DK:GMLP_WIDTH + (h + 1) * HGRN_DK] = y_b.astype(BF16)
        return carry

    lax.fori_loop(0, tm // HGRN_CHUNK, chunk_body, 0)

    y = _dot(y_s[...], w_out_ref[...])
    o_ref[0] = _layer_norm(ALPHA * x + y, ln_g_ref[...], ln_b_ref[...])


def _even_mixer(x, w_in, w_out, ws, bs_t, aln_g, aln_b, bnorm_g, lb_logits, ln_g, ln_b, lb_index):
    bsz, t_len, d = x.shape
    tm = ROW_TILE
    kern = functools.partial(_even_mixer_kernel, lb_index=lb_index)
    return pl.pallas_call(
        kern,
        grid=(bsz, t_len // tm),
        in_specs=[
            pl.BlockSpec((1, tm, d), lambda b, t: (b, t, 0)),
            _resident(w_in.shape), _resident(w_out.shape), _resident(ws.shape), _resident(bs_t.shape),
            _resident(aln_g.shape), _resident(aln_b.shape), _resident(bnorm_g.shape),
            _resident(lb_logits.shape), _resident(ln_g.shape), _resident(ln_b.shape),
        ],
        out_specs=pl.BlockSpec((1, tm, d), lambda b, t: (b, t, 0)),
        out_shape=jax.ShapeDtypeStruct(x.shape, F32),
        scratch_shapes=[
            pltpu.VMEM((tm, HGRN_WIDTH), F32), pltpu.VMEM((tm, HGRN_WIDTH), F32),
            pltpu.VMEM((tm, HGRN_WIDTH), F32), pltpu.VMEM((tm, HGRN_WIDTH), F32),
            pltpu.VMEM((tm, GMLP_WIDTH + HGRN_WIDTH), BF16),
            pltpu.VMEM((HGRN_HEADS, HGRN_DK, HGRN_DK), F32),
        ],
        compiler_params=_params("parallel", "arbitrary"),
        name="even_mixer",
    )(x, w_in, w_out, ws, bs_t, aln_g, aln_b, bnorm_g, lb_logits, ln_g, ln_b)


def _mem_kv_kernel(mem_ref, w_ref, o_ref):
    o_ref[...] = _dot(mem_ref[...].astype(BF16), w_ref[...]).astype(BF16)


def _mem_kv(mem2d, w_kv):
    n, d = mem2d.shape
    tm = ROW_TILE
    return pl.pallas_call(
        _mem_kv_kernel,
        grid=(n // tm,),
        in_specs=[pl.BlockSpec((tm, d), lambda i: (i, 0)), _resident(w_kv.shape)],
        out_specs=pl.BlockSpec((tm, w_kv.shape[1]), lambda i: (i, 0)),
        out_shape=jax.ShapeDtypeStruct((n, w_kv.shape[1]), BF16),
        compiler_params=_params("parallel"),
        name="mem_kv",
    )(mem2d, w_kv)


def _cross_attn_kernel(x_ref, kv_ref, wq_ref, wo_ref, ln_g_ref, ln_b_ref, o_ref, att_s):
    x = x_ref[0]
    q = (_dot(x.astype(BF16), wq_ref[...]) * (MEM_HDIM ** -0.5)).astype(BF16)
    for h in range(MEM_HEADS):
        lanes = slice(h * MEM_HDIM, (h + 1) * MEM_HDIM)
        k_h = kv_ref[0, :, h * MEM_HDIM:(h + 1) * MEM_HDIM]
        v_h = kv_ref[0, :, D_MODEL + h * MEM_HDIM:D_MODEL + (h + 1) * MEM_HDIM]
        s = _dot_nt(q[:, lanes], k_h)
        p = jnp.exp(s - jnp.max(s, axis=-1, keepdims=True))
        denom = jnp.sum(p, axis=-1, keepdims=True)
        att_s[:, lanes] = (_dot(p.astype(BF16), v_h) / denom).astype(BF16)
    y = _dot(att_s[...], wo_ref[...])
    o_ref[0] = _layer_norm(ALPHA * x + y, ln_g_ref[...], ln_b_ref[...])


def _cross_attn(x, kv, wq, wo, ln_g, ln_b):
    bsz, t_len, d = x.shape
    tm = ROW_TILE
    return pl.pallas_call(
        _cross_attn_kernel,
        grid=(bsz, t_len // tm),
        in_specs=[
            pl.BlockSpec((1, tm, d), lambda b, t: (b, t, 0)),
            pl.BlockSpec((1,) + kv.shape[1:], lambda b, t: (b, 0, 0)),
            _resident(wq.shape), _resident(wo.shape), _resident(ln_g.shape), _resident(ln_b.shape),
        ],
        out_specs=pl.BlockSpec((1, tm, d), lambda b, t: (b, t, 0)),
        out_shape=jax.ShapeDtypeStruct(x.shape, F32),
        scratch_shapes=[pltpu.VMEM((tm, d), BF16)],
        compiler_params=_params("parallel", "parallel"),
        name="cross_attn",
    )(x, kv, wq, wo, ln_g, ln_b)


def _ffn_kernel(x_ref, w_in_ref, w_out_ref, ln_g_ref, ln_b_ref, o_ref, acc_s):
    x = x_ref[...]
    xb = x.astype(BF16)
    for c in range(D_FF // FFN_CHUNK):
        cols = slice(c * FFN_CHUNK, (c + 1) * FFN_CHUNK)
        gate = _dot(xb, w_in_ref[:, cols])
        up = _dot(xb, w_in_ref[:, D_FF + c * FFN_CHUNK:D_FF + (c + 1) * FFN_CHUNK])
        act = (jax.nn.silu(gate) * up).astype(BF16)
        part = _dot(act, w_out_ref[cols, :])
        if c == 0:
            acc_s[...] = part
        else:
            acc_s[...] += part
    o_ref[...] = _layer_norm(ALPHA * x + acc_s[...], ln_g_ref[...], ln_b_ref[...])


def _ffn(x2d, w_in, w_out, ln_g, ln_b):
    n, d = x2d.shape
    tm = ROW_TILE
    return pl.pallas_call(
        _ffn_kernel,
        grid=(n // tm,),
        in_specs=[
            pl.BlockSpec((tm, d), lambda i: (i, 0)),
            _resident(w_in.shape), _resident(w_out.shape), _resident(ln_g.shape), _resident(ln_b.shape),
        ],
        out_specs=pl.BlockSpec((tm, d), lambda i: (i, 0)),
        out_shape=jax.ShapeDtypeStruct((n, d), F32),
        scratch_shapes=[pltpu.VMEM((tm, d), F32)],
        compiler_params=_params("parallel"),
        name="ffn",
    )(x2d, w_in, w_out, ln_g, ln_b)


LOG2_E = 1.4426950408889634
ALIBI_LANES = 3


def _moba_qkv_kernel(x_ref, w_qk_ref, w_vt_ref, fill_ref, q0_ref, q1_ref, k0_ref, k1_ref, vt_ref, kmean_ref,
                     *, tiles_per_seq):
    tm = x_ref.shape[0]
    xb = x_ref[...].astype(BF16)
    q = _dot(xb, w_qk_ref[:, 0:D_MODEL]) * (MOBA_HDIM ** -0.5 * LOG2_E)
    k = _dot(xb, w_qk_ref[:, D_MODEL:2 * D_MODEL])
    for i in range(tm // MOBA_BLOCK):
        kmean_ref[i] = jnp.mean(k[i * MOBA_BLOCK:(i + 1) * MOBA_BLOCK], axis=0, keepdims=True)
    vt_ref[0] = _dot_nt(w_vt_ref[...], xb).astype(BF16)

    lane = lax.broadcasted_iota(jnp.int32, (tm, MOBA_PAIR), 1)
    pos = lax.broadcasted_iota(jnp.int32, (tm, MOBA_PAIR), 0) + (pl.program_id(0) % tiles_per_seq) * tm
    pos_block = ((pos // MOBA_BLOCK) * MOBA_BLOCK).astype(F32)
    pos_offset = (pos % MOBA_BLOCK).astype(F32)
    for e, (q_ref, k_ref) in enumerate(((q0_ref, k0_ref), (q1_ref, k1_ref))):
        own = (lane < MOBA_HDIM) if e == 0 else (lane >= MOBA_HDIM)
        partner = MOBA_HDIM * (1 - e)
        key_fill = jnp.where((lane >= partner) & (lane < partner + ALIBI_LANES), pos_block,
                             jnp.where((lane >= partner + ALIBI_LANES) & (lane < partner + 2 * ALIBI_LANES),
                                       pos_offset, 0.0))
        for p in range(MOBA_HEADS // 2):
            slab = slice(p * MOBA_PAIR, (p + 1) * MOBA_PAIR)
            q_ref[:, slab] = jnp.where(own, q[:, slab], fill_ref[e:e + 1, slab]).astype(BF16)
            k_ref[:, slab] = jnp.where(own, k[:, slab], key_fill).astype(BF16)


def _moba_qkv(x2d, w_qk, w_vt, fill, bsz, t_len):
    n, d = x2d.shape
    tm = ROW_TILE
    tiles_per_seq = t_len // tm
    rows = pl.BlockSpec((tm, d), lambda i: (i, 0))
    return pl.pallas_call(
        functools.partial(_moba_qkv_kernel, tiles_per_seq=tiles_per_seq),
        grid=(n // tm,),
        in_specs=[rows, _resident(w_qk.shape), _resident(w_vt.shape), _resident(fill.shape)],
        out_specs=[
            rows, rows, rows, rows,
            pl.BlockSpec((1, d, tm), lambda i: (i // tiles_per_seq, 0, i % tiles_per_seq)),
            pl.BlockSpec((tm // MOBA_BLOCK, 1, d), lambda i: (i, 0, 0)),
        ],
        out_shape=[
            jax.ShapeDtypeStruct((n, d), BF16), jax.ShapeDtypeStruct((n, d), BF16),
            jax.ShapeDtypeStruct((n, d), BF16), jax.ShapeDtypeStruct((n, d), BF16),
            jax.ShapeDtypeStruct((bsz, d, t_len), BF16),
            jax.ShapeDtypeStruct((n // MOBA_BLOCK, 1, d), F32),
        ],
        compiler_params=_params("parallel"),
        name="moba_qkv",
    )(x2d, w_qk, w_vt, fill)


def _alibi_query_fill():
    slopes = jnp.asarray([2.0 ** (-8.0 * (h + 1) / MOBA_HEADS) for h in range(MOBA_HEADS)], F32) * LOG2_E
    pieces = []
    rest = slopes
    for _ in range(ALIBI_LANES):
        piece = rest.astype(BF16).astype(F32)
        pieces.append(piece)
        rest = rest - piece
    pieces = jnp.stack(pieces + pieces, axis=1)
    fill = jnp.zeros((2, MOBA_HEADS // 2, MOBA_PAIR), F32)
    for e in range(2):
        partner = MOBA_HDIM * (1 - e)
        fill = fill.at[e, :, partner:partner + 2 * ALIBI_LANES].set(pieces[e::2])
    return fill.reshape(2, D_MODEL)


def _moba_attn_kernel(x_ref, q0_ref, q1_ref, k0_ref, k1_ref, vt_ref, kmean_ref, w_out_ref, ln_g_ref, ln_b_ref,
                      o_ref, sa_s, sb_s, sha_s, shb_s, p_s, att_s):
    qb = pl.program_id(1)
    n_blocks = kmean_ref.shape[1]
    blk = MOBA_BLOCK
    q_refs = (q0_ref, q1_ref)
    k_refs = (k0_ref, k1_ref)
    key_i = lax.broadcasted_iota(jnp.int32, (blk, blk), 0)
    qry_i = lax.broadcasted_iota(jnp.int32, (blk, blk), 1)
    causal = key_i <= qry_i
    blk_i = lax.broadcasted_iota(jnp.int32, (n_blocks, blk), 0)
    mean_lane = lax.broadcasted_iota(jnp.int32, (n_blocks, MOBA_PAIR), 1)

    def attend(n_past):
        n_keys = (n_past + 1) * blk
        select = n_past > MOBA_TOPK

        def pair_lanes(hp):
            return pl.ds(pl.multiple_of(hp * MOBA_PAIR, MOBA_PAIR), MOBA_PAIR)

        def scores_and_shifts(hp, s_buf, sh_buf):
            lanes = pair_lanes(hp)
            for e in range(2):
                cols = slice(e * blk, (e + 1) * blk)
                q_e = q_refs[e][0, :, lanes]
                if select:
                    own = (mean_lane < MOBA_HDIM) if e == 0 else (mean_lane >= MOBA_HDIM)
                    kmean = jnp.where(own, kmean_ref[0, :, lanes], 0.0)
                    kmean_hi = kmean.astype(BF16)
                    kmean_lo = (kmean - kmean_hi.astype(F32)).astype(BF16)
                    aff = _dot_nt(kmean_hi, q_e) + _dot_nt(kmean_lo, q_e)
                    rank = jnp.zeros((n_blocks, blk), F32)
                    for jp in range(n_past):
                        other = aff[jp:jp + 1, :]
                        beats = (other > aff) | ((other == aff) & (jp < blk_i))
                        rank = rank + jnp.where(beats, 1.0, 0.0)
                    chosen = rank < MOBA_TOPK
                m = None
                for j in range(n_past + 1):
                    rows = slice(j * blk, (j + 1) * blk)
                    t = _dot_nt(k_refs[e][0, rows, lanes], q_e)
                    if j == n_past:
                        t = jnp.where(causal, t, MASKED)
                    s_buf[rows, cols] = t
                    m_j = jnp.max(t, axis=0, keepdims=True)
                    if select and j < n_past:
                        m_j = jnp.where(chosen[j:j + 1, :], m_j, MASKED)
                    m = m_j if m is None else jnp.maximum(m, m_j)
                for j in range(n_past + 1):
                    picked = select and j < n_past
                    sh_buf[e, j:j + 1, :] = jnp.where(chosen[j:j + 1, :], m, -MASKED) if picked else m

        def softmax_pv(hp, s_buf, sh_buf):
            for e in range(2):
                cols = slice(e * blk, (e + 1) * blk)
                for j in range(n_past + 1):
                    rows = slice(j * blk, (j + 1) * blk)
                    p_s[e, rows, :] = jnp.exp2(s_buf[rows, cols] - sh_buf[e, j:j + 1, :]).astype(BF16)
                feat = pl.ds(pl.multiple_of(hp * MOBA_PAIR + e * MOBA_HDIM, MOBA_HDIM), MOBA_HDIM)
                values = jnp.concatenate([vt_ref[0, feat, 0:n_keys], jnp.ones((16, n_keys), BF16)], axis=0)
                acc = _dot(values, p_s[e, 0:n_keys, :])
                att_s[feat, :] = acc[0:MOBA_HDIM] / acc[MOBA_HDIM:MOBA_HDIM + 1]

        n_pairs = MOBA_HEADS // 2
        scores_and_shifts(0, sa_s, sha_s)

        def two_pairs(i, carry):
            hp = 2 * i
            scores_and_shifts(hp + 1, sb_s, shb_s)
            softmax_pv(hp, sa_s, sha_s)
            scores_and_shifts(jnp.minimum(hp + 2, n_pairs - 1), sa_s, sha_s)
            softmax_pv(hp + 1, sb_s, shb_s)
            return carry

        lax.fori_loop(0, n_pairs // 2, two_pairs, 0)

    for n_past in range(n_blocks):
        pl.when(qb == n_past)(functools.partial(attend, n_past))

    att = att_s[...].T.astype(BF16)
    y = _dot(att, w_out_ref[...])
    o_ref[0] = _layer_norm(ALPHA * x_ref[0] + y, ln_g_ref[...], ln_b_ref[...])


def _moba_attn(x, q0, q1, k0, k1, vt, kmean, w_out, ln_g, ln_b):
    bsz, t_len, d = x.shape
    blk = MOBA_BLOCK
    n_blocks = t_len // blk
    query_rows = pl.BlockSpec((1, blk, d), lambda b, t: (b, t, 0))
    all_keys = pl.BlockSpec((1, t_len, d), lambda b, t: (b, 0, 0))
    return pl.pallas_call(
        _moba_attn_kernel,
        grid=(bsz, n_blocks),
        in_specs=[
            query_rows, query_rows, query_rows, all_keys, all_keys,
            pl.BlockSpec((1, d, t_len), lambda b, t: (b, 0, 0)),
            pl.BlockSpec((1, n_blocks, d), lambda b, t: (b, 0, 0)),
            _resident(w_out.shape), _resident(ln_g.shape), _resident(ln_b.shape),
        ],
        out_specs=query_rows,
        out_shape=jax.ShapeDtypeStruct(x.shape, F32),
        scratch_shapes=[
            pltpu.VMEM((t_len, 2 * blk), F32),
            pltpu.VMEM((t_len, 2 * blk), F32),
            pltpu.VMEM((2, n_blocks, blk), F32),
            pltpu.VMEM((2, n_blocks, blk), F32),
            pltpu.VMEM((2, t_len, blk), BF16),
            pltpu.VMEM((d, blk), F32),
        ],
        compiler_params=_params("parallel", "arbitrary"),
        name="moba_attn",
    )(x, q0, q1, k0, k1, vt, kmean, w_out, ln_g, ln_b)


def kernel(x, mem, ln_g, ln_b, x_wq, x_wkv, x_wo, ffn_w_in, ffn_w_out, ev_w_in, ev_w_out, a_ws, a_bs,
           a_ln_g, a_ln_b, b_norm_g, hgrn_lb_logits, od_w_qkv, od_w_out):
    bsz, t_len, d = x.shape
    assert d == D_MODEL and t_len % ROW_TILE == 0 and t_len % MOBA_BLOCK == 0
    assert ROW_TILE % MOBA_BLOCK == 0 and t_len // MOBA_BLOCK > 1
    n = bsz * t_len
    mem2d = mem.reshape(bsz * mem.shape[1], d)

    def row(v):
        return v.reshape(1, -1)

    def seq(v):
        return v.reshape(bsz, t_len, d)

    for layer in range(DEPTH):
        j = layer // 2
        if layer % 2 == 0:
            x = _even_mixer(
                x, ev_w_in[j].astype(BF16), ev_w_out[j].astype(BF16), a_ws[j], a_bs[j].T,
                row(a_ln_g[j]), row(a_ln_b[j]), row(b_norm_g[j]), hgrn_lb_logits,
                row(ln_g[layer, 0]), row(ln_b[layer, 0]), j)
        else:
            w_qkv = od_w_qkv[j]
            q0, q1, k0, k1, vt, kmean = _moba_qkv(
                x.reshape(n, d), w_qkv[:, :2 * d].astype(BF16), w_qkv[:, 2 * d:].T.astype(BF16),
                _alibi_query_fill(), bsz, t_len)
            x = _moba_attn(
                x, seq(q0), seq(q1), seq(k0), seq(k1), vt, kmean.reshape(bsz, t_len // MOBA_BLOCK, d),
                od_w_out[j].astype(BF16), row(ln_g[layer, 0]), row(ln_b[layer, 0]))
        kv = _mem_kv(mem2d, x_wkv[layer].astype(BF16)).reshape(bsz, mem.shape[1], 2 * d)
        x = _cross_attn(x, kv, x_wq[layer].astype(BF16), x_wo[layer].astype(BF16),
                        row(ln_g[layer, 1]), row(ln_b[layer, 1]))
        x = _ffn(x.reshape(n, d), ffn_w_in[layer].astype(BF16), ffn_w_out[layer].astype(BF16),
                 row(ln_g[layer, 2]), row(ln_b[layer, 2])).reshape(bsz, t_len, d)
    return x
```

```python
import functools
import math

import jax
import jax.numpy as jnp
from jax import lax
from jax.experimental import pallas as pl
from jax.experimental.pallas import tpu as pltpu

D_MODEL = 1024
DEPTH = 2
ALPHA = (2.0 * DEPTH) ** 0.25
LN_EPS = 1e-5

GMLP_WIDTH = D_MODEL // 2
GMLP_GROUPS = 4
GMLP_GDIM = GMLP_WIDTH // GMLP_GROUPS
GMLP_CHUNK = 128
HGRN_WIDTH = D_MODEL // 2
HGRN_HEADS = 4
HGRN_DK = HGRN_WIDTH // HGRN_HEADS
HGRN_CHUNK = 64
EVEN_IN_WIDTH = 2 * GMLP_WIDTH + 4 * HGRN_WIDTH

MOBA_HEADS = 16
MOBA_HDIM = D_MODEL // MOBA_HEADS
MOBA_BLOCK = 256
MOBA_TOPK = 3
MOBA_PAIR = 2 * MOBA_HDIM

MEM_HEADS = 4
MEM_HDIM = D_MODEL // MEM_HEADS

D_FF = int(math.ceil(8 * D_MODEL / 3 / 256)) * 256
FFN_CHUNK = 256

ROW_TILE = 512
FFN_ROW_TILE = 1024
V7X_VMEM_LIMIT = 56 * 1024 * 1024

MASKED = -1e30

BF16 = jnp.bfloat16
F32 = jnp.float32


def _dot(a, b):
    return jnp.dot(a, b, preferred_element_type=F32)


def _dot_nt(a, b):
    return lax.dot_general(a, b, (((1,), (1,)), ((), ())), preferred_element_type=F32)


def _dot_tn(a, b):
    return lax.dot_general(a, b, (((0,), (0,)), ((), ())), preferred_element_type=F32)


def _layer_norm(z, g, b):
    mu = jnp.mean(z, axis=-1, keepdims=True)
    zc = z - mu
    var = jnp.mean(zc * zc, axis=-1, keepdims=True)
    return zc * lax.rsqrt(var + LN_EPS) * g + b


def _resident(shape):
    return pl.BlockSpec(shape, lambda *_: (0,) * len(shape), pipeline_mode=pl.Buffered(1))


def _params(*semantics):
    return pltpu.CompilerParams(dimension_semantics=semantics, vmem_limit_bytes=V7X_VMEM_LIMIT)


def _even_mixer_kernel(x_ref, w_in_ref, w_out_ref, ws_ref, bs_ref, aln_g_ref, aln_b_ref, bnorm_ref,
                       lb_logits_ref, ln_g_ref, ln_b_ref, o_ref,
                       q_s, f_s, i_s, g_s, y_s, state_s, *, lb_index):
    tm = x_ref.shape[1]
    x = x_ref[0]
    xb = x.astype(BF16)

    @pl.when(pl.program_id(1) == 0)
    def _():
        state_s[...] = jnp.zeros_like(state_s)

    u = jax.nn.gelu(_dot(xb, w_in_ref[:, 0:GMLP_WIDTH]))
    v = jax.nn.gelu(_dot(xb, w_in_ref[:, GMLP_WIDTH:2 * GMLP_WIDTH]))
    row = lax.broadcasted_iota(jnp.int32, (GMLP_CHUNK, GMLP_CHUNK), 0)
    col = lax.broadcasted_iota(jnp.int32, (GMLP_CHUNK, GMLP_CHUNK), 1)
    for g in range(GMLP_GROUPS):
        lanes = slice(g * GMLP_GDIM, (g + 1) * GMLP_GDIM)
        vn = _layer_norm(v[:, lanes], aln_g_ref[:, lanes], aln_b_ref[:, lanes]).astype(BF16)
        wg = jnp.where(col <= row, ws_ref[g], 0.0).astype(BF16)
        bias = bs_ref[:, g:g + 1]
        for c in range(tm // GMLP_CHUNK):
            rows = slice(c * GMLP_CHUNK, (c + 1) * GMLP_CHUNK)
            s = _dot(wg, vn[rows]) + bias
            y_s[rows, lanes] = (u[rows, lanes] * s).astype(BF16)

    base = 2 * GMLP_WIDTH
    q_s[...] = _dot(xb, w_in_ref[:, base:base + HGRN_WIDTH])
    f_s[...] = _dot(xb, w_in_ref[:, base + HGRN_WIDTH:base + 2 * HGRN_WIDTH])
    i_s[...] = jax.nn.silu(_dot(xb, w_in_ref[:, base + 2 * HGRN_WIDTH:base + 3 * HGRN_WIDTH]))
    g_s[...] = jax.nn.silu(_dot(xb, w_in_ref[:, base + 3 * HGRN_WIDTH:base + 4 * HGRN_WIDTH]))

    logits = lb_logits_ref[...]
    e = jnp.exp(logits - jnp.max(logits, axis=0, keepdims=True))
    lb = jnp.sum(e[0:lb_index + 1], axis=0, keepdims=True) / jnp.sum(e, axis=0, keepdims=True)

    crow = lax.broadcasted_iota(jnp.int32, (HGRN_CHUNK, HGRN_CHUNK), 0)
    ccol = lax.broadcasted_iota(jnp.int32, (HGRN_CHUNK, HGRN_CHUNK), 1)
    causal = ccol <= crow
    tril_ones = jnp.where(causal, 1.0, 0.0).astype(BF16)
    norm_g = bnorm_ref[...]

    def chunk_body(c, carry):
        rows = pl.ds(pl.multiple_of(c * HGRN_CHUNK, HGRN_CHUNK), HGRN_CHUNK)
        f = lb + (1.0 - lb) * jax.nn.sigmoid(f_s[rows, :])
        logf = jnp.log(f)
        hi = logf.astype(BF16)
        r1 = logf - hi.astype(F32)
        mid = r1.astype(BF16)
        lo = (r1 - mid.astype(F32)).astype(BF16)
        cum = _dot(tril_ones, hi) + _dot(tril_ones, mid) + _dot(tril_ones, lo)
        cum_last = cum[HGRN_CHUNK - 1:HGRN_CHUNK, :]
        k = 1.0 - f
        q_dec = (q_s[rows, :] * jnp.exp(cum)).astype(BF16)
        k_dec = (k * jnp.exp(-cum)).astype(BF16)
        k_tail = (k * jnp.exp(cum_last - cum)).astype(BF16)
        chunk_decay = jnp.exp(cum_last)
        val = i_s[rows, :].astype(BF16)
        gate = g_s[rows, :]
        for h in range(HGRN_HEADS):
            lanes = slice(h * HGRN_DK, (h + 1) * HGRN_DK)
            attn = jnp.where(causal, _dot_nt(q_dec[:, lanes], k_dec[:, lanes]), 0.0).astype(BF16)
            state_t = state_s[h]
            o = _dot(attn, val[:, lanes]) + _dot_nt(q_dec[:, lanes], state_t.astype(BF16))
            state_s[h] = state_t * chunk_decay[:, lanes] + _dot_tn(val[:, lanes], k_tail[:, lanes])
            rms = lax.rsqrt(jnp.mean(o * o, axis=-1, keepdims=True) + LN_EPS)
            y_b = o * rms * norm_g[:, lanes] * gate[:, lanes]
            y_s[rows, GMLP_WIDTH + h * HGRN_DK:GMLP_WIDTH + (h + 1) * HGRN_DK] = y_b.astype(BF16)
        return carry

    lax.fori_loop(0, tm // HGRN_CHUNK, chunk_body, 0, unroll=True)

    y = _dot(y_s[...], w_out_ref[...])
    o_ref[0] = _layer_norm(ALPHA * x + y, ln_g_ref[...], ln_b_ref[...])


def _even_mixer(x, w_in, w_out, ws, bs_t, aln_g, aln_b, bnorm_g, lb_logits, ln_g, ln_b, lb_index):
    bsz, t_len, d = x.shape
    tm = ROW_TILE
    kern = functools.partial(_even_mixer_kernel, lb_index=lb_index)
    return pl.pallas_call(
        kern,
        grid=(bsz, t_len // tm),
        in_specs=[
            pl.BlockSpec((1, tm, d), lambda b, t: (b, t, 0)),
            _resident(w_in.shape), _resident(w_out.shape), _resident(ws.shape), _resident(bs_t.shape),
            _resident(aln_g.shape), _resident(aln_b.shape), _resident(bnorm_g.shape),
            _resident(lb_logits.shape), _resident(ln_g.shape), _resident(ln_b.shape),
        ],
        out_specs=pl.BlockSpec((1, tm, d), lambda b, t: (b, t, 0)),
        out_shape=jax.ShapeDtypeStruct(x.shape, F32),
        scratch_shapes=[
            pltpu.VMEM((tm, HGRN_WIDTH), F32), pltpu.VMEM((tm, HGRN_WIDTH), F32),
            pltpu.VMEM((tm, HGRN_WIDTH), F32), pltpu.VMEM((tm, HGRN_WIDTH), F32),
            pltpu.VMEM((tm, GMLP_WIDTH + HGRN_WIDTH), BF16),
            pltpu.VMEM((HGRN_HEADS, HGRN_DK, HGRN_DK), F32),
        ],
        compiler_params=_params("parallel", "arbitrary"),
        name="even_mixer",
    )(x, w_in, w_out, ws, bs_t, aln_g, aln_b, bnorm_g, lb_logits, ln_g, ln_b)


def _mem_kv_kernel(mem_ref, w_ref, o_ref):
    o_ref[...] = _dot(mem_ref[...].astype(BF16), w_ref[...]).astype(BF16)


def _mem_kv(mem2d, w_kv):
    n, d = mem2d.shape
    tm = ROW_TILE
    return pl.pallas_call(
        _mem_kv_kernel,
        grid=(n // tm,),
        in_specs=[pl.BlockSpec((tm, d), lambda i: (i, 0)), _resident(w_kv.shape)],
        out_specs=pl.BlockSpec((tm, w_kv.shape[1]), lambda i: (i, 0)),
        out_shape=jax.ShapeDtypeStruct((n, w_kv.shape[1]), BF16),
        compiler_params=_params("parallel"),
        name="mem_kv",
    )(mem2d, w_kv)


def _cross_attn_kernel(x_ref, kv_ref, wq_ref, wo_ref, ln_g_ref, ln_b_ref, o_ref, att_s):
    x = x_ref[0]
    q = (_dot(x.astype(BF16), wq_ref[...]) * (MEM_HDIM ** -0.5)).astype(BF16)
    for h in range(MEM_HEADS):
        lanes = slice(h * MEM_HDIM, (h + 1) * MEM_HDIM)
        k_h = kv_ref[0, :, h * MEM_HDIM:(h + 1) * MEM_HDIM]
        v_h = kv_ref[0, :, D_MODEL + h * MEM_HDIM:D_MODEL + (h + 1) * MEM_HDIM]
        s = _dot_nt(q[:, lanes], k_h)
        p = jnp.exp(s - jnp.max(s, axis=-1, keepdims=True))
        denom = jnp.sum(p, axis=-1, keepdims=True)
        att_s[:, lanes] = (_dot(p.astype(BF16), v_h) / denom).astype(BF16)
    y = _dot(att_s[...], wo_ref[...])
    o_ref[0] = _layer_norm(ALPHA * x + y, ln_g_ref[...], ln_b_ref[...])


def _cross_attn(x, kv, wq, wo, ln_g, ln_b):
    bsz, t_len, d = x.shape
    tm = ROW_TILE
    return pl.pallas_call(
        _cross_attn_kernel,
        grid=(bsz, t_len // tm),
        in_specs=[
            pl.BlockSpec((1, tm, d), lambda b, t: (b, t, 0)),
            pl.BlockSpec((1,) + kv.shape[1:], lambda b, t: (b, 0, 0)),
            _resident(wq.shape), _resident(wo.shape), _resident(ln_g.shape), _resident(ln_b.shape),
        ],
        out_specs=pl.BlockSpec((1, tm, d), lambda b, t: (b, t, 0)),
        out_shape=jax.ShapeDtypeStruct(x.shape, F32),
        scratch_shapes=[pltpu.VMEM((tm, d), BF16)],
        compiler_params=_params("parallel", "parallel"),
        name="cross_attn",
    )(x, kv, wq, wo, ln_g, ln_b)


def _ffn_kernel(x_ref, w_in_ref, w_out_ref, ln_g_ref, ln_b_ref, o_ref, acc_s):
    x = x_ref[...]
    xb = x.astype(BF16)
    for c in range(D_FF // FFN_CHUNK):
        cols = slice(c * FFN_CHUNK, (c + 1) * FFN_CHUNK)
        gate = _dot(xb, w_in_ref[:, cols])
        up = _dot(xb, w_in_ref[:, D_FF + c * FFN_CHUNK:D_FF + (c + 1) * FFN_CHUNK])
        act = (jax.nn.silu(gate) * up).astype(BF16)
        part = _dot(act, w_out_ref[cols, :])
        if c == 0:
            acc_s[...] = part
        else:
            acc_s[...] += part
    o_ref[...] = _layer_norm(ALPHA * x + acc_s[...], ln_g_ref[...], ln_b_ref[...])


def _ffn(x2d, w_in, w_out, ln_g, ln_b):
    n, d = x2d.shape
    tm = FFN_ROW_TILE
    assert n % tm == 0
    return pl.pallas_call(
        _ffn_kernel,
        grid=(n // tm,),
        in_specs=[
            pl.BlockSpec((tm, d), lambda i: (i, 0)),
            _resident(w_in.shape), _resident(w_out.shape), _resident(ln_g.shape), _resident(ln_b.shape),
        ],
        out_specs=pl.BlockSpec((tm, d), lambda i: (i, 0)),
        out_shape=jax.ShapeDtypeStruct((n, d), F32),
        scratch_shapes=[pltpu.VMEM((tm, d), F32)],
        compiler_params=_params("parallel"),
        name="ffn",
    )(x2d, w_in, w_out, ln_g, ln_b)


LOG2_E = 1.4426950408889634
ALIBI_LANES = 3


def _moba_qkv_kernel(x_ref, w_qk_ref, w_vt_ref, fill_ref, q0_ref, q1_ref, k0_ref, k1_ref, vt_ref, kmean_ref,
                     *, tiles_per_seq):
    tm = x_ref.shape[0]
    xb = x_ref[...].astype(BF16)
    q = _dot(xb, w_qk_ref[:, 0:D_MODEL]) * (MOBA_HDIM ** -0.5 * LOG2_E)
    k = _dot(xb, w_qk_ref[:, D_MODEL:2 * D_MODEL])
    for i in range(tm // MOBA_BLOCK):
        kmean_ref[i] = jnp.mean(k[i * MOBA_BLOCK:(i + 1) * MOBA_BLOCK], axis=0, keepdims=True)
    vt_ref[0] = _dot_nt(w_vt_ref[...], xb).astype(BF16)

    lane = lax.broadcasted_iota(jnp.int32, (tm, MOBA_PAIR), 1)
    pos = lax.broadcasted_iota(jnp.int32, (tm, MOBA_PAIR), 0) + (pl.program_id(0) % tiles_per_seq) * tm
    pos_block = ((pos // MOBA_BLOCK) * MOBA_BLOCK).astype(F32)
    pos_offset = (pos % MOBA_BLOCK).astype(F32)
    for e, (q_ref, k_ref) in enumerate(((q0_ref, k0_ref), (q1_ref, k1_ref))):
        own = (lane < MOBA_HDIM) if e == 0 else (lane >= MOBA_HDIM)
        partner = MOBA_HDIM * (1 - e)
        key_fill = jnp.where((lane >= partner) & (lane < partner + ALIBI_LANES), pos_block,
                             jnp.where((lane >= partner + ALIBI_LANES) & (lane < partner + 2 * ALIBI_LANES),
                                       pos_offset, 0.0))
        for p in range(MOBA_HEADS // 2):
            slab = slice(p * MOBA_PAIR, (p + 1) * MOBA_PAIR)
            q_ref[:, slab] = jnp.where(own, q[:, slab], fill_ref[e:e + 1, slab]).astype(BF16)
            k_ref[:, slab] = jnp.where(own, k[:, slab], key_fill).astype(BF16)


def _moba_qkv(x2d, w_qk, w_vt, fill, bsz, t_len):
    n, d = x2d.shape
    tm = ROW_TILE
    tiles_per_seq = t_len // tm
    rows = pl.BlockSpec((tm, d), lambda i: (i, 0))
    return pl.pallas_call(
        functools.partial(_moba_qkv_kernel, tiles_per_seq=tiles_per_seq),
        grid=(n // tm,),
        in_specs=[rows, _resident(w_qk.shape), _resident(w_vt.shape), _resident(fill.shape)],
        out_specs=[
            rows, rows, rows, rows,
            pl.BlockSpec((1, d, tm), lambda i: (i // tiles_per_seq, 0, i % tiles_per_seq)),
            pl.BlockSpec((tm // MOBA_BLOCK, 1, d), lambda i: (i, 0, 0)),
        ],
        out_shape=[
            jax.ShapeDtypeStruct((n, d), BF16), jax.ShapeDtypeStruct((n, d), BF16),
            jax.ShapeDtypeStruct((n, d), BF16), jax.ShapeDtypeStruct((n, d), BF16),
            jax.ShapeDtypeStruct((bsz, d, t_len), BF16),
            jax.ShapeDtypeStruct((n // MOBA_BLOCK, 1, d), F32),
        ],
        compiler_params=_params("parallel"),
        name="moba_qkv",
    )(x2d, w_qk, w_vt, fill)


def _alibi_query_fill():
    slopes = jnp.asarray([2.0 ** (-8.0 * (h + 1) / MOBA_HEADS) for h in range(MOBA_HEADS)], F32) * LOG2_E
    pieces = []
    rest = slopes
    for _ in range(ALIBI_LANES):
        piece = rest.astype(BF16).astype(F32)
        pieces.append(piece)
        rest = rest - piece
    pieces = jnp.stack(pieces + pieces, axis=1)
    fill = jnp.zeros((2, MOBA_HEADS // 2, MOBA_PAIR), F32)
    for e in range(2):
        partner = MOBA_HDIM * (1 - e)
        fill = fill.at[e, :, partner:partner + 2 * ALIBI_LANES].set(pieces[e::2])
    return fill.reshape(2, D_MODEL)


def _moba_attn_kernel(x_ref, q0_ref, q1_ref, k0_ref, k1_ref, vt_ref, kmean_ref, w_out_ref, ln_g_ref, ln_b_ref,
                      o_ref, sa_s, sb_s, sha_s, shb_s, pa_s, pb_s, att_s):
    qb = pl.program_id(1)
    n_blocks = kmean_ref.shape[1]
    blk = MOBA_BLOCK
    q_refs = (q0_ref, q1_ref)
    k_refs = (k0_ref, k1_ref)
    key_i = lax.broadcasted_iota(jnp.int32, (blk, blk), 0)
    qry_i = lax.broadcasted_iota(jnp.int32, (blk, blk), 1)
    causal = key_i <= qry_i
    blk_i = lax.broadcasted_iota(jnp.int32, (n_blocks, blk), 0)
    mean_lane = lax.broadcasted_iota(jnp.int32, (n_blocks, MOBA_PAIR), 1)

    def attend(n_past):
        n_keys = (n_past + 1) * blk
        select = n_past > MOBA_TOPK

        def pair_lanes(hp):
            return pl.ds(pl.multiple_of(hp * MOBA_PAIR, MOBA_PAIR), MOBA_PAIR)

        def scores_and_shifts(hp, s_buf, sh_buf):
            lanes = pair_lanes(hp)
            for e in range(2):
                cols = slice(e * blk, (e + 1) * blk)
                q_e = q_refs[e][0, :, lanes]
                if select:
                    own = (mean_lane < MOBA_HDIM) if e == 0 else (mean_lane >= MOBA_HDIM)
                    kmean = jnp.where(own, kmean_ref[0, :, lanes], 0.0)
                    kmean_hi = kmean.astype(BF16)
                    kmean_lo = (kmean - kmean_hi.astype(F32)).astype(BF16)
                    aff = _dot_nt(kmean_hi, q_e) + _dot_nt(kmean_lo, q_e)
                    rank = jnp.zeros((n_blocks, blk), F32)
                    for jp in range(n_past):
                        other = aff[jp:jp + 1, :]
                        beats = (other > aff) | ((other == aff) & (jp < blk_i))
                        rank = rank + jnp.where(beats, 1.0, 0.0)
                    chosen = rank < MOBA_TOPK
                m = None
                for j in range(n_past + 1):
                    rows = slice(j * blk, (j + 1) * blk)
                    t = _dot_nt(k_refs[e][0, rows, lanes], q_e)
                    if j == n_past:
                        t = jnp.where(causal, t, MASKED)
                    s_buf[rows, cols] = t
                    m_j = jnp.max(t, axis=0, keepdims=True)
                    if select and j < n_past:
                        m_j = jnp.where(chosen[j:j + 1, :], m_j, MASKED)
                    m = m_j if m is None else jnp.maximum(m, m_j)
                shifts = jnp.broadcast_to(m, (n_blocks, blk))
                if select:
                    shifts = jnp.where(chosen | (blk_i == n_past), shifts, -MASKED)
                sh_buf[e] = shifts

        def probabilities(s_buf, sh_buf, p_buf, after=None):
            hold = 0.0 if after is None else jnp.minimum(jnp.abs(after), 0.0)
            for e in range(2):
                cols = slice(e * blk, (e + 1) * blk)
                for j in range(n_past + 1):
                    rows = slice(j * blk, (j + 1) * blk)
                    shift = sh_buf[e, j:j + 1, :] + hold
                    p_buf[e, rows, :] = jnp.exp2(s_buf[rows, cols] - shift).astype(BF16)

        def weighted_values(hp, p_buf):
            for e in range(2):
                feat = pl.ds(pl.multiple_of(hp * MOBA_PAIR + e * MOBA_HDIM, MOBA_HDIM), MOBA_HDIM)
                values = jnp.concatenate([vt_ref[0, feat, 0:n_keys], jnp.ones((16, n_keys), BF16)], axis=0)
                acc = _dot(values, p_buf[e, 0:n_keys, :])
                denom = acc[MOBA_HDIM:MOBA_HDIM + 1]
                att_s[feat, :] = acc[0:MOBA_HDIM] / denom
            return denom

        n_pairs = MOBA_HEADS // 2
        scores_and_shifts(0, sa_s, sha_s)
        scores_and_shifts(1, sb_s, shb_s)
        probabilities(sa_s, sha_s, pa_s)

        def two_pairs(i, carry):
            hp = 2 * i + 1
            scores_and_shifts(hp + 1, sa_s, sha_s)
            probabilities(sb_s, shb_s, pb_s)
            consumed = weighted_values(hp - 1, pa_s)
            scores_and_shifts(hp + 2, sb_s, shb_s)
            probabilities(sa_s, sha_s, pa_s, after=consumed)
            weighted_values(hp, pb_s)
            return carry

        lax.fori_loop(0, n_pairs // 2 - 1, two_pairs, 0)
        probabilities(sb_s, shb_s, pb_s)
        weighted_values(n_pairs - 2, pa_s)
        weighted_values(n_pairs - 1, pb_s)

    for n_past in range(n_blocks):
        pl.when(qb == n_past)(functools.partial(attend, n_past))

    att = att_s[...].T.astype(BF16)
    y = _dot(att, w_out_ref[...])
    o_ref[0] = _layer_norm(ALPHA * x_ref[0] + y, ln_g_ref[...], ln_b_ref[...])


def _moba_attn(x, q0, q1, k0, k1, vt, kmean, w_out, ln_g, ln_b):
    bsz, t_len, d = x.shape
    blk = MOBA_BLOCK
    n_blocks = t_len // blk
    query_rows = pl.BlockSpec((1, blk, d), lambda b, t: (b, t, 0))
    all_keys = pl.BlockSpec((1, t_len, d), lambda b, t: (b, 0, 0))
    return pl.pallas_call(
        _moba_attn_kernel,
        grid=(bsz, n_blocks),
        in_specs=[
            query_rows, query_rows, query_rows, all_keys, all_keys,
            pl.BlockSpec((1, d, t_len), lambda b, t: (b, 0, 0)),
            pl.BlockSpec((1, n_blocks, d), lambda b, t: (b, 0, 0)),
            _resident(w_out.shape), _resident(ln_g.shape), _resident(ln_b.shape),
        ],
        out_specs=query_rows,
        out_shape=jax.ShapeDtypeStruct(x.shape, F32),
        scratch_shapes=[
            pltpu.VMEM((t_len, 2 * blk), F32),
            pltpu.VMEM((t_len, 2 * blk), F32),
            pltpu.VMEM((2, n_blocks, blk), F32),
            pltpu.VMEM((2, n_blocks, blk), F32),
            pltpu.VMEM((2, t_len, blk), BF16),
            pltpu.VMEM((2, t_len, blk), BF16),
            pltpu.VMEM((d, blk), F32),
        ],
        compiler_params=_params("parallel", "arbitrary"),
        name="moba_attn",
    )(x, q0, q1, k0, k1, vt, kmean, w_out, ln_g, ln_b)


def kernel(x, mem, ln_g, ln_b, x_wq, x_wkv, x_wo, ffn_w_in, ffn_w_out, ev_w_in, ev_w_out, a_ws, a_bs,
           a_ln_g, a_ln_b, b_norm_g, hgrn_lb_logits, od_w_qkv, od_w_out):
    bsz, t_len, d = x.shape
    assert d == D_MODEL and t_len % ROW_TILE == 0 and t_len % MOBA_BLOCK == 0
    assert ROW_TILE % MOBA_BLOCK == 0 and t_len // MOBA_BLOCK > 1
    n = bsz * t_len
    mem2d = mem.reshape(bsz * mem.shape[1], d)

    def row(v):
        return v.reshape(1, -1)

    def seq(v):
        return v.reshape(bsz, t_len, d)

    for layer in range(DEPTH):
        j = layer // 2
        if layer % 2 == 0:
            x = _even_mixer(
                x, ev_w_in[j].astype(BF16), ev_w_out[j].astype(BF16), a_ws[j], a_bs[j].T,
                row(a_ln_g[j]), row(a_ln_b[j]), row(b_norm_g[j]), hgrn_lb_logits,
                row(ln_g[layer, 0]), row(ln_b[layer, 0]), j)
        else:
            w_qkv = od_w_qkv[j]
            q0, q1, k0, k1, vt, kmean = _moba_qkv(
                x.reshape(n, d), w_qkv[:, :2 * d].astype(BF16), w_qkv[:, 2 * d:].T.astype(BF16),
                _alibi_query_fill(), bsz, t_len)
            x = _moba_attn(
                x, seq(q0), seq(q1), seq(k0), seq(k1), vt, kmean.reshape(bsz, t_len // MOBA_BLOCK, d),
                od_w_out[j].astype(BF16), row(ln_g[layer, 0]), row(ln_b[layer, 0]))
        kv = _mem_kv(mem2d, x_wkv[layer].astype(BF16)).reshape(bsz, mem.shape[1], 2 * d)
        x = _cross_attn(x, kv, x_wq[layer].astype(BF16), x_wo[layer].astype(BF16),
                        row(ln_g[layer, 1]), row(ln_b[layer, 1]))
        x = _ffn(x.reshape(n, d), ffn_w_in[layer].astype(BF16), ffn_w_out[layer].astype(BF16),
                 row(ln_g[layer, 2]), row(ln_b[layer, 2])).reshape(bsz, t_len, d)
    return x
```

```python
import functools
import math

import jax
import jax.numpy as jnp
from jax import lax
from jax.experimental import pallas as pl
from jax.experimental.pallas import tpu as pltpu

D_MODEL = 1024
DEPTH = 2
ALPHA = (2.0 * DEPTH) ** 0.25
LN_EPS = 1e-5

GMLP_WIDTH = D_MODEL // 2
GMLP_GROUPS = 4
GMLP_GDIM = GMLP_WIDTH // GMLP_GROUPS
GMLP_CHUNK = 128
HGRN_WIDTH = D_MODEL // 2
HGRN_HEADS = 4
HGRN_DK = HGRN_WIDTH // HGRN_HEADS
HGRN_CHUNK = 64
EVEN_IN_WIDTH = 2 * GMLP_WIDTH + 4 * HGRN_WIDTH

MOBA_HEADS = 16
MOBA_HDIM = D_MODEL // MOBA_HEADS
MOBA_BLOCK = 256
MOBA_TOPK = 3
MOBA_PAIR = 2 * MOBA_HDIM

MEM_HEADS = 4
MEM_HDIM = D_MODEL // MEM_HEADS

D_FF = int(math.ceil(8 * D_MODEL / 3 / 256)) * 256
FFN_CHUNK = 256

ROW_TILE = 512
FFN_ROW_TILE = 1024
V7X_VMEM_LIMIT = 56 * 1024 * 1024

MASKED = -1e30

BF16 = jnp.bfloat16
F32 = jnp.float32


def _dot(a, b):
    return jnp.dot(a, b, preferred_element_type=F32)


def _dot_nt(a, b):
    return lax.dot_general(a, b, (((1,), (1,)), ((), ())), preferred_element_type=F32)


def _dot_tn(a, b):
    return lax.dot_general(a, b, (((0,), (0,)), ((), ())), preferred_element_type=F32)


def _layer_norm(z, g, b):
    mu = jnp.mean(z, axis=-1, keepdims=True)
    zc = z - mu
    var = jnp.mean(zc * zc, axis=-1, keepdims=True)
    return zc * lax.rsqrt(var + LN_EPS) * g + b


def _resident(picked):
    stack, index = picked
    return pl.BlockSpec((None,) + stack.shape[1:], lambda *_: (index,) + (0,) * (stack.ndim - 1),
                        pipeline_mode=pl.Buffered(1))


def _stacks(*picked):
    return [stack for stack, _ in picked]


def _params(*semantics):
    return pltpu.CompilerParams(dimension_semantics=semantics, vmem_limit_bytes=V7X_VMEM_LIMIT)


def _even_mixer_kernel(x_ref, w_in_ref, w_out_ref, ws_ref, bs_ref, aln_g_ref, aln_b_ref, bnorm_ref,
                       lb_logits_ref, ln_g_ref, ln_b_ref, o_ref,
                       q_s, f_s, i_s, g_s, y_s, state_s, *, lb_index):
    tm = x_ref.shape[1]
    x = x_ref[0]
    xb = x.astype(BF16)

    @pl.when(pl.program_id(1) == 0)
    def _():
        state_s[...] = jnp.zeros_like(state_s)

    u = jax.nn.gelu(_dot(xb, w_in_ref[:, 0:GMLP_WIDTH]))
    v = jax.nn.gelu(_dot(xb, w_in_ref[:, GMLP_WIDTH:2 * GMLP_WIDTH]))
    row = lax.broadcasted_iota(jnp.int32, (GMLP_CHUNK, GMLP_CHUNK), 0)
    col = lax.broadcasted_iota(jnp.int32, (GMLP_CHUNK, GMLP_CHUNK), 1)
    for g in range(GMLP_GROUPS):
        lanes = slice(g * GMLP_GDIM, (g + 1) * GMLP_GDIM)
        vn = _layer_norm(v[:, lanes], aln_g_ref[:, lanes], aln_b_ref[:, lanes]).astype(BF16)
        wg = jnp.where(col <= row, ws_ref[g], 0.0).astype(BF16)
        bias = bs_ref[:, g:g + 1]
        for c in range(tm // GMLP_CHUNK):
            rows = slice(c * GMLP_CHUNK, (c + 1) * GMLP_CHUNK)
            s = _dot(wg, vn[rows]) + bias
            y_s[rows, lanes] = (u[rows, lanes] * s).astype(BF16)

    base = 2 * GMLP_WIDTH
    q_s[...] = _dot(xb, w_in_ref[:, base:base + HGRN_WIDTH])
    f_s[...] = _dot(xb, w_in_ref[:, base + HGRN_WIDTH:base + 2 * HGRN_WIDTH])
    i_s[...] = jax.nn.silu(_dot(xb, w_in_ref[:, base + 2 * HGRN_WIDTH:base + 3 * HGRN_WIDTH]))
    g_s[...] = jax.nn.silu(_dot(xb, w_in_ref[:, base + 3 * HGRN_WIDTH:base + 4 * HGRN_WIDTH]))

    logits = lb_logits_ref[...]
    e = jnp.exp(logits - jnp.max(logits, axis=0, keepdims=True))
    lb = jnp.sum(e[0:lb_index + 1], axis=0, keepdims=True) / jnp.sum(e, axis=0, keepdims=True)

    crow = lax.broadcasted_iota(jnp.int32, (HGRN_CHUNK, HGRN_CHUNK), 0)
    ccol = lax.broadcasted_iota(jnp.int32, (HGRN_CHUNK, HGRN_CHUNK), 1)
    causal = ccol <= crow
    tril_ones = jnp.where(causal, 1.0, 0.0).astype(BF16)
    norm_g = bnorm_ref[...]

    def chunk_body(c, carry):
        rows = pl.ds(pl.multiple_of(c * HGRN_CHUNK, HGRN_CHUNK), HGRN_CHUNK)
        f = lb + (1.0 - lb) * jax.nn.sigmoid(f_s[rows, :])
        logf = jnp.log(f)
        hi = logf.astype(BF16)
        r1 = logf - hi.astype(F32)
        mid = r1.astype(BF16)
        lo = (r1 - mid.astype(F32)).astype(BF16)
        cum = _dot(tril_ones, hi) + _dot(tril_ones, mid) + _dot(tril_ones, lo)
        cum_last = cum[HGRN_CHUNK - 1:HGRN_CHUNK, :]
        k = 1.0 - f
        q_dec = (q_s[rows, :] * jnp.exp(cum)).astype(BF16)
        k_dec = (k * jnp.exp(-cum)).astype(BF16)
        k_tail = (k * jnp.exp(cum_last - cum)).astype(BF16)
        chunk_decay = jnp.exp(cum_last)
        val = i_s[rows, :].astype(BF16)
        gate = g_s[rows, :]
        for h in range(HGRN_HEADS):
            lanes = slice(h * HGRN_DK, (h + 1) * HGRN_DK)
            attn = jnp.where(causal, _dot_nt(q_dec[:, lanes], k_dec[:, lanes]), 0.0).astype(BF16)
            state_t = state_s[h]
            o = _dot(attn, val[:, lanes]) + _dot_nt(q_dec[:, lanes], state_t.astype(BF16))
            state_s[h] = state_t * chunk_decay[:, lanes] + _dot_tn(val[:, lanes], k_tail[:, lanes])
            rms = lax.rsqrt(jnp.mean(o * o, axis=-1, keepdims=True) + LN_EPS)
            y_b = o * rms * norm_g[:, lanes] * gate[:, lanes]
            y_s[rows, GMLP_WIDTH + h * HGRN_DK:GMLP_WIDTH + (h + 1) * HGRN_DK] = y_b.astype(BF16)
        return carry

    lax.fori_loop(0, tm // HGRN_CHUNK, chunk_body, 0, unroll=True)

    y = _dot(y_s[...], w_out_ref[...])
    o_ref[0] = _layer_norm(ALPHA * x + y, ln_g_ref[...], ln_b_ref[...])


def _even_mixer(x, lb_index, *params):
    bsz, t_len, d = x.shape
    tm = ROW_TILE
    kern = functools.partial(_even_mixer_kernel, lb_index=lb_index)
    return pl.pallas_call(
        kern,
        grid=(bsz, t_len // tm),
        in_specs=[pl.BlockSpec((1, tm, d), lambda b, t: (b, t, 0))] + [_resident(p) for p in params],
        out_specs=pl.BlockSpec((1, tm, d), lambda b, t: (b, t, 0)),
        out_shape=jax.ShapeDtypeStruct(x.shape, F32),
        scratch_shapes=[
            pltpu.VMEM((tm, HGRN_WIDTH), F32), pltpu.VMEM((tm, HGRN_WIDTH), F32),
            pltpu.VMEM((tm, HGRN_WIDTH), F32), pltpu.VMEM((tm, HGRN_WIDTH), F32),
            pltpu.VMEM((tm, GMLP_WIDTH + HGRN_WIDTH), BF16),
            pltpu.VMEM((HGRN_HEADS, HGRN_DK, HGRN_DK), F32),
        ],
        compiler_params=_params("parallel", "arbitrary"),
        name="even_mixer",
    )(x, *_stacks(*params))


def _mem_kv_kernel(mem_ref, w_ref, o_ref):
    o_ref[...] = _dot(mem_ref[...].astype(BF16), w_ref[...]).astype(BF16)


def _mem_kv(mem2d, w_kv):
    n, d = mem2d.shape
    tm = ROW_TILE
    width = w_kv[0].shape[-1]
    return pl.pallas_call(
        _mem_kv_kernel,
        grid=(n // tm,),
        in_specs=[pl.BlockSpec((tm, d), lambda i: (i, 0)), _resident(w_kv)],
        out_specs=pl.BlockSpec((tm, width), lambda i: (i, 0)),
        out_shape=jax.ShapeDtypeStruct((n, width), BF16),
        compiler_params=_params("parallel"),
        name="mem_kv",
    )(mem2d, *_stacks(w_kv))


def _cross_attn_kernel(x_ref, kv_ref, wq_ref, wo_ref, ln_g_ref, ln_b_ref, o_ref, att_s):
    x = x_ref[0]
    q = (_dot(x.astype(BF16), wq_ref[...]) * (MEM_HDIM ** -0.5)).astype(BF16)
    for h in range(MEM_HEADS):
        lanes = slice(h * MEM_HDIM, (h + 1) * MEM_HDIM)
        k_h = kv_ref[0, :, h * MEM_HDIM:(h + 1) * MEM_HDIM]
        v_h = kv_ref[0, :, D_MODEL + h * MEM_HDIM:D_MODEL + (h + 1) * MEM_HDIM]
        s = _dot_nt(q[:, lanes], k_h)
        p = jnp.exp(s - jnp.max(s, axis=-1, keepdims=True))
        denom = jnp.sum(p, axis=-1, keepdims=True)
        att_s[:, lanes] = (_dot(p.astype(BF16), v_h) / denom).astype(BF16)
    y = _dot(att_s[...], wo_ref[...])
    o_ref[0] = _layer_norm(ALPHA * x + y, ln_g_ref[...], ln_b_ref[...])


def _cross_attn(x, kv, *params):
    bsz, t_len, d = x.shape
    tm = ROW_TILE
    return pl.pallas_call(
        _cross_attn_kernel,
        grid=(bsz, t_len // tm),
        in_specs=[
            pl.BlockSpec((1, tm, d), lambda b, t: (b, t, 0)),
            pl.BlockSpec((1,) + kv.shape[1:], lambda b, t: (b, 0, 0)),
        ] + [_resident(p) for p in params],
        out_specs=pl.BlockSpec((1, tm, d), lambda b, t: (b, t, 0)),
        out_shape=jax.ShapeDtypeStruct(x.shape, F32),
        scratch_shapes=[pltpu.VMEM((tm, d), BF16)],
        compiler_params=_params("parallel", "parallel"),
        name="cross_attn",
    )(x, kv, *_stacks(*params))


def _ffn_kernel(x_ref, w_in_ref, w_out_ref, ln_g_ref, ln_b_ref, o_ref, acc_s):
    half = x_ref.shape[0] // 2
    n_chunks = D_FF // FFN_CHUNK

    def chunk(rows, xb, c):
        cols = slice(c * FFN_CHUNK, (c + 1) * FFN_CHUNK)
        gate = _dot(xb, w_in_ref[:, cols])
        up = _dot(xb, w_in_ref[:, D_FF + c * FFN_CHUNK:D_FF + (c + 1) * FFN_CHUNK])
        act = (jax.nn.silu(gate) * up).astype(BF16)
        part = _dot(act, w_out_ref[cols, :])
        if c == 0:
            acc_s[rows, :] = part
        else:
            acc_s[rows, :] += part

    def finish(rows):
        o_ref[rows, :] = _layer_norm(ALPHA * x_ref[rows, :] + acc_s[rows, :], ln_g_ref[...], ln_b_ref[...])

    first, second = slice(0, half), slice(half, 2 * half)
    xb_first = x_ref[first, :].astype(BF16)
    xb_second = x_ref[second, :].astype(BF16)
    for c in range(n_chunks):
        chunk(first, xb_first, c)
    chunk(second, xb_second, 0)
    finish(first)
    for c in range(1, n_chunks):
        chunk(second, xb_second, c)
    finish(second)


def _ffn(x2d, *params):
    n, d = x2d.shape
    tm = FFN_ROW_TILE
    assert n % tm == 0
    return pl.pallas_call(
        _ffn_kernel,
        grid=(n // tm,),
        in_specs=[pl.BlockSpec((tm, d), lambda i: (i, 0))] + [_resident(p) for p in params],
        out_specs=pl.BlockSpec((tm, d), lambda i: (i, 0)),
        out_shape=jax.ShapeDtypeStruct((n, d), F32),
        scratch_shapes=[pltpu.VMEM((tm, d), F32)],
        compiler_params=_params("parallel"),
        name="ffn",
    )(x2d, *_stacks(*params))


LOG2_E = 1.4426950408889634
ALIBI_LANES = 3


def _moba_qkv_kernel(x_ref, w_ref, fill_ref, q0_ref, q1_ref, k0_ref, k1_ref, vt_ref, kmean_ref,
                     *, tiles_per_seq):
    tm = x_ref.shape[0]
    xb = x_ref[...].astype(BF16)
    q = _dot(xb, w_ref[:, 0:D_MODEL]) * (MOBA_HDIM ** -0.5 * LOG2_E)
    k = _dot(xb, w_ref[:, D_MODEL:2 * D_MODEL])
    for i in range(tm // MOBA_BLOCK):
        kmean_ref[i] = jnp.mean(k[i * MOBA_BLOCK:(i + 1) * MOBA_BLOCK], axis=0, keepdims=True)
    vt_ref[0] = lax.dot_general(w_ref[:, 2 * D_MODEL:3 * D_MODEL], xb, (((0,), (1,)), ((), ())),
                                preferred_element_type=F32).astype(BF16)

    lane = lax.broadcasted_iota(jnp.int32, (tm, MOBA_PAIR), 1)
    pos = lax.broadcasted_iota(jnp.int32, (tm, MOBA_PAIR), 0) + (pl.program_id(0) % tiles_per_seq) * tm
    pos_block = ((pos // MOBA_BLOCK) * MOBA_BLOCK).astype(F32)
    pos_offset = (pos % MOBA_BLOCK).astype(F32)
    for e, (q_ref, k_ref) in enumerate(((q0_ref, k0_ref), (q1_ref, k1_ref))):
        own = (lane < MOBA_HDIM) if e == 0 else (lane >= MOBA_HDIM)
        partner = MOBA_HDIM * (1 - e)
        key_fill = jnp.where((lane >= partner) & (lane < partner + ALIBI_LANES), pos_block,
                             jnp.where((lane >= partner + ALIBI_LANES) & (lane < partner + 2 * ALIBI_LANES),
                                       pos_offset, 0.0))
        for p in range(MOBA_HEADS // 2):
            slab = slice(p * MOBA_PAIR, (p + 1) * MOBA_PAIR)
            q_ref[:, slab] = jnp.where(own, q[:, slab], fill_ref[e:e + 1, slab]).astype(BF16)
            k_ref[:, slab] = jnp.where(own, k[:, slab], key_fill).astype(BF16)


def _moba_qkv(x2d, w_qkv, fill, bsz, t_len):
    n, d = x2d.shape
    tm = ROW_TILE
    tiles_per_seq = t_len // tm
    rows = pl.BlockSpec((tm, d), lambda i: (i, 0))
    return pl.pallas_call(
        functools.partial(_moba_qkv_kernel, tiles_per_seq=tiles_per_seq),
        grid=(n // tm,),
        in_specs=[rows, _resident(w_qkv), _resident(fill)],
        out_specs=[
            rows, rows, rows, rows,
            pl.BlockSpec((1, d, tm), lambda i: (i // tiles_per_seq, 0, i % tiles_per_seq)),
            pl.BlockSpec((tm // MOBA_BLOCK, 1, d), lambda i: (i, 0, 0)),
        ],
        out_shape=[
            jax.ShapeDtypeStruct((n, d), BF16), jax.ShapeDtypeStruct((n, d), BF16),
            jax.ShapeDtypeStruct((n, d), BF16), jax.ShapeDtypeStruct((n, d), BF16),
            jax.ShapeDtypeStruct((bsz, d, t_len), BF16),
            jax.ShapeDtypeStruct((n // MOBA_BLOCK, 1, d), F32),
        ],
        compiler_params=_params("parallel"),
        name="moba_qkv",
    )(x2d, *_stacks(w_qkv, fill))


def _alibi_query_fill():
    slopes = jnp.asarray([2.0 ** (-8.0 * (h + 1) / MOBA_HEADS) for h in range(MOBA_HEADS)], F32) * LOG2_E
    pieces = []
    rest = slopes
    for _ in range(ALIBI_LANES):
        piece = rest.astype(BF16).astype(F32)
        pieces.append(piece)
        rest = rest - piece
    pieces = jnp.stack(pieces + pieces, axis=1)
    fill = jnp.zeros((2, MOBA_HEADS // 2, MOBA_PAIR), F32)
    for e in range(2):
        partner = MOBA_HDIM * (1 - e)
        fill = fill.at[e, :, partner:partner + 2 * ALIBI_LANES].set(pieces[e::2])
    return fill.reshape(2, D_MODEL)


def _moba_attn_kernel(x_ref, q0_ref, q1_ref, k0_ref, k1_ref, vt_ref, kmean_ref, w_out_ref, ln_g_ref, ln_b_ref,
                      o_ref, sa_s, sb_s, sha_s, shb_s, pa_s, pb_s, att_s):
    qb = pl.program_id(1)
    n_blocks = kmean_ref.shape[1]
    blk = MOBA_BLOCK
    q_refs = (q0_ref, q1_ref)
    k_refs = (k0_ref, k1_ref)
    key_i = lax.broadcasted_iota(jnp.int32, (blk, blk), 0)
    qry_i = lax.broadcasted_iota(jnp.int32, (blk, blk), 1)
    causal = key_i <= qry_i
    blk_i = lax.broadcasted_iota(jnp.int32, (n_blocks, blk), 0)
    mean_lane = lax.broadcasted_iota(jnp.int32, (n_blocks, MOBA_PAIR), 1)

    def attend(n_past):
        n_keys = (n_past + 1) * blk
        select = n_past > MOBA_TOPK

        def pair_lanes(hp):
            return pl.ds(pl.multiple_of(hp * MOBA_PAIR, MOBA_PAIR), MOBA_PAIR)

        def scores_and_shifts(hp, s_buf, sh_buf):
            lanes = pair_lanes(hp)
            for e in range(2):
                cols = slice(e * blk, (e + 1) * blk)
                q_e = q_refs[e][0, :, lanes]
                if select:
                    own = (mean_lane < MOBA_HDIM) if e == 0 else (mean_lane >= MOBA_HDIM)
                    kmean = jnp.where(own, kmean_ref[0, :, lanes], 0.0)
                    kmean_hi = kmean.astype(BF16)
                    kmean_lo = (kmean - kmean_hi.astype(F32)).astype(BF16)
                    aff = _dot_nt(kmean_hi, q_e) + _dot_nt(kmean_lo, q_e)
                    rank = jnp.zeros((n_blocks, blk), F32)
                    for jp in range(n_past):
                        other = aff[jp:jp + 1, :]
                        beats = (other > aff) | ((other == aff) & (jp < blk_i))
                        rank = rank + jnp.where(beats, 1.0, 0.0)
                    chosen = rank < MOBA_TOPK
                m = None
                for j in range(n_past + 1):
                    rows = slice(j * blk, (j + 1) * blk)
                    t = _dot_nt(k_refs[e][0, rows, lanes], q_e)
                    if j == n_past:
                        t = jnp.where(causal, t, MASKED)
                    s_buf[rows, cols] = t
                    m_j = jnp.max(t, axis=0, keepdims=True)
                    if select and j < n_past:
                        m_j = jnp.where(chosen[j:j + 1, :], m_j, MASKED)
                    m = m_j if m is None else jnp.maximum(m, m_j)
                shifts = jnp.broadcast_to(m, (n_blocks, blk))
                if select:
                    shifts = jnp.where(chosen | (blk_i == n_past), shifts, -MASKED)
                sh_buf[e] = shifts

        def probabilities(s_buf, sh_buf, p_buf, after=None):
            hold = 0.0 if after is None else jnp.minimum(jnp.abs(after), 0.0)
            for e in range(2):
                cols = slice(e * blk, (e + 1) * blk)
                for j in range(n_past + 1):
                    rows = slice(j * blk, (j + 1) * blk)
                    shift = sh_buf[e, j:j + 1, :] + hold
                    p_buf[e, rows, :] = jnp.exp2(s_buf[rows, cols] - shift).astype(BF16)

        def weighted_values(hp, p_buf):
            for e in range(2):
                feat = pl.ds(pl.multiple_of(hp * MOBA_PAIR + e * MOBA_HDIM, MOBA_HDIM), MOBA_HDIM)
                values = jnp.concatenate([vt_ref[0, feat, 0:n_keys], jnp.ones((16, n_keys), BF16)], axis=0)
                acc = _dot(values, p_buf[e, 0:n_keys, :])
                denom = acc[MOBA_HDIM:MOBA_HDIM + 1]
                att_s[feat, :] = acc[0:MOBA_HDIM] / denom
            return denom

        n_pairs = MOBA_HEADS // 2
        scores_and_shifts(0, sa_s, sha_s)
        scores_and_shifts(1, sb_s, shb_s)
        probabilities(sa_s, sha_s, pa_s)

        def two_pairs(i, carry):
            hp = 2 * i + 1
            scores_and_shifts(hp + 1, sa_s, sha_s)
            probabilities(sb_s, shb_s, pb_s)
            consumed = weighted_values(hp - 1, pa_s)
            scores_and_shifts(hp + 2, sb_s, shb_s)
            probabilities(sa_s, sha_s, pa_s, after=consumed)
            weighted_values(hp, pb_s)
            return carry

        lax.fori_loop(0, n_pairs // 2 - 1, two_pairs, 0)
        probabilities(sb_s, shb_s, pb_s)
        weighted_values(n_pairs - 2, pa_s)
        weighted_values(n_pairs - 1, pb_s)

    for n_past in range(n_blocks):
        pl.when(qb == n_past)(functools.partial(attend, n_past))

    att = att_s[...].T.astype(BF16)
    y = _dot(att, w_out_ref[...])
    o_ref[0] = _layer_norm(ALPHA * x_ref[0] + y, ln_g_ref[...], ln_b_ref[...])


def _moba_attn(x, q0, q1, k0, k1, vt, kmean, *params):
    bsz, t_len, d = x.shape
    blk = MOBA_BLOCK
    n_blocks = t_len // blk
    query_rows = pl.BlockSpec((1, blk, d), lambda b, t: (b, t, 0))
    all_keys = pl.BlockSpec((1, t_len, d), lambda b, t: (b, 0, 0))
    return pl.pallas_call(
        _moba_attn_kernel,
        grid=(bsz, n_blocks),
        in_specs=[
            query_rows, query_rows, query_rows, all_keys, all_keys,
            pl.BlockSpec((1, d, t_len), lambda b, t: (b, 0, 0)),
            pl.BlockSpec((1, n_blocks, d), lambda b, t: (b, 0, 0)),
        ] + [_resident(p) for p in params],
        out_specs=query_rows,
        out_shape=jax.ShapeDtypeStruct(x.shape, F32),
        scratch_shapes=[
            pltpu.VMEM((t_len, 2 * blk), F32),
            pltpu.VMEM((t_len, 2 * blk), F32),
            pltpu.VMEM((2, n_blocks, blk), F32),
            pltpu.VMEM((2, n_blocks, blk), F32),
            pltpu.VMEM((2, t_len, blk), BF16),
            pltpu.VMEM((2, t_len, blk), BF16),
            pltpu.VMEM((d, blk), F32),
        ],
        compiler_params=_params("parallel", "arbitrary"),
        name="moba_attn",
    )(x, q0, q1, k0, k1, vt, kmean, *_stacks(*params))


def kernel(x, mem, ln_g, ln_b, x_wq, x_wkv, x_wo, ffn_w_in, ffn_w_out, ev_w_in, ev_w_out, a_ws, a_bs,
           a_ln_g, a_ln_b, b_norm_g, hgrn_lb_logits, od_w_qkv, od_w_out):
    bsz, t_len, d = x.shape
    assert d == D_MODEL and t_len % ROW_TILE == 0 and t_len % MOBA_BLOCK == 0
    assert ROW_TILE % MOBA_BLOCK == 0 and t_len // MOBA_BLOCK > 1
    n = bsz * t_len
    mem2d = mem.reshape(bsz * mem.shape[1], d)

    def rows_of(v):
        return v.reshape(-1, 1, v.shape[-1])

    def seq(v):
        return v.reshape(bsz, t_len, d)

    x_wq, x_wkv, x_wo, ffn_w_in, ffn_w_out, ev_w_in, ev_w_out, od_w_qkv, od_w_out = (
        w.astype(BF16) for w in (x_wq, x_wkv, x_wo, ffn_w_in, ffn_w_out, ev_w_in, ev_w_out, od_w_qkv, od_w_out))
    ln_g, ln_b = rows_of(ln_g), rows_of(ln_b)
    a_ln_g, a_ln_b, b_norm_g = rows_of(a_ln_g), rows_of(a_ln_b), rows_of(b_norm_g)
    a_bs_t = jnp.swapaxes(a_bs, 1, 2)
    lb_logits = hgrn_lb_logits[None]
    alibi_fill = _alibi_query_fill()[None]

    for layer in range(DEPTH):
        j = layer // 2
        norm = [(ln_g, 3 * layer), (ln_b, 3 * layer)]
        if layer % 2 == 0:
            x = _even_mixer(
                x, j, (ev_w_in, j), (ev_w_out, j), (a_ws, j), (a_bs_t, j), (a_ln_g, j), (a_ln_b, j),
                (b_norm_g, j), (lb_logits, 0), *norm)
        else:
            q0, q1, k0, k1, vt, kmean = _moba_qkv(x.reshape(n, d), (od_w_qkv, j), (alibi_fill, 0), bsz, t_len)
            x = _moba_attn(
                x, seq(q0), seq(q1), seq(k0), seq(k1), vt, kmean.reshape(bsz, t_len // MOBA_BLOCK, d),
                (od_w_out, j), *norm)
        kv = _mem_kv(mem2d, (x_wkv, layer)).reshape(bsz, mem.shape[1], 2 * d)
        x = _cross_attn(x, kv, (x_wq, layer), (x_wo, layer), (ln_g, 3 * layer + 1), (ln_b, 3 * layer + 1))
        x = _ffn(x.reshape(n, d), (ffn_w_in, layer), (ffn_w_out, layer),
                 (ln_g, 3 * layer + 2), (ln_b, 3 * layer + 2)).reshape(bsz, t_len, d)
    return x
```

```python
import functools
import math

import jax
import jax.numpy as jnp
from jax import lax
from jax.experimental import pallas as pl
from jax.experimental.pallas import tpu as pltpu

D_MODEL = 1024
DEPTH = 2
ALPHA = (2.0 * DEPTH) ** 0.25
LN_EPS = 1e-5

GMLP_WIDTH = D_MODEL // 2
GMLP_GROUPS = 4
GMLP_GDIM = GMLP_WIDTH // GMLP_GROUPS
GMLP_CHUNK = 128
HGRN_WIDTH = D_MODEL // 2
HGRN_HEADS = 4
HGRN_DK = HGRN_WIDTH // HGRN_HEADS
HGRN_CHUNK = 64
EVEN_IN_WIDTH = 2 * GMLP_WIDTH + 4 * HGRN_WIDTH

MOBA_HEADS = 16
MOBA_HDIM = D_MODEL // MOBA_HEADS
MOBA_BLOCK = 256
MOBA_TOPK = 3
MOBA_PAIR = 2 * MOBA_HDIM
MOBA_BLOCKS_PER_CALL = 4

MEM_HEADS = 4
MEM_HDIM = D_MODEL // MEM_HEADS

D_FF = int(math.ceil(8 * D_MODEL / 3 / 256)) * 256
FFN_CHUNK = 256

ROW_TILE = 512
FFN_ROW_TILE = 1024
V7X_VMEM_LIMIT = 56 * 1024 * 1024

MASKED = -1e30

BF16 = jnp.bfloat16
F32 = jnp.float32


def _dot(a, b):
    return jnp.dot(a, b, preferred_element_type=F32)


def _dot_nt(a, b):
    return lax.dot_general(a, b, (((1,), (1,)), ((), ())), preferred_element_type=F32)


def _dot_tn(a, b):
    return lax.dot_general(a, b, (((0,), (0,)), ((), ())), preferred_element_type=F32)


def _layer_norm(z, g, b):
    mu = jnp.mean(z, axis=-1, keepdims=True)
    zc = z - mu
    var = jnp.mean(zc * zc, axis=-1, keepdims=True)
    return zc * lax.rsqrt(var + LN_EPS) * g + b


def _resident(picked):
    stack, index = picked
    return pl.BlockSpec((None,) + stack.shape[1:], lambda *_: (index,) + (0,) * (stack.ndim - 1),
                        pipeline_mode=pl.Buffered(1))


def _stacks(*picked):
    return [stack for stack, _ in picked]


def _params(*semantics):
    return pltpu.CompilerParams(dimension_semantics=semantics, vmem_limit_bytes=V7X_VMEM_LIMIT)


def _even_mixer_kernel(x_ref, w_in_ref, w_out_ref, ws_ref, bs_ref, aln_g_ref, aln_b_ref, bnorm_ref,
                       lb_logits_ref, ln_g_ref, ln_b_ref, o_ref,
                       q_s, f_s, i_s, g_s, y_s, state_s, *, lb_index):
    tm = x_ref.shape[1]
    x = x_ref[0]
    xb = x.astype(BF16)

    @pl.when(pl.program_id(1) == 0)
    def _():
        state_s[...] = jnp.zeros_like(state_s)

    u = jax.nn.gelu(_dot(xb, w_in_ref[:, 0:GMLP_WIDTH]))
    v = jax.nn.gelu(_dot(xb, w_in_ref[:, GMLP_WIDTH:2 * GMLP_WIDTH]))
    row = lax.broadcasted_iota(jnp.int32, (GMLP_CHUNK, GMLP_CHUNK), 0)
    col = lax.broadcasted_iota(jnp.int32, (GMLP_CHUNK, GMLP_CHUNK), 1)
    for g in range(GMLP_GROUPS):
        lanes = slice(g * GMLP_GDIM, (g + 1) * GMLP_GDIM)
        vn = _layer_norm(v[:, lanes], aln_g_ref[:, lanes], aln_b_ref[:, lanes]).astype(BF16)
        wg = jnp.where(col <= row, ws_ref[g], 0.0).astype(BF16)
        bias = bs_ref[:, g:g + 1]
        for c in range(tm // GMLP_CHUNK):
            rows = slice(c * GMLP_CHUNK, (c + 1) * GMLP_CHUNK)
            s = _dot(wg, vn[rows]) + bias
            y_s[rows, lanes] = (u[rows, lanes] * s).astype(BF16)

    base = 2 * GMLP_WIDTH
    n_chunks = tm // HGRN_CHUNK
    logits = lb_logits_ref[...]
    e = jnp.exp(logits - jnp.max(logits, axis=0, keepdims=True))
    lb = jnp.sum(e[0:lb_index + 1], axis=0, keepdims=True) / jnp.sum(e, axis=0, keepdims=True)

    q_s[...] = _dot(xb, w_in_ref[:, base:base + HGRN_WIDTH])
    f_s[...] = _dot(xb, w_in_ref[:, base + HGRN_WIDTH:base + 2 * HGRN_WIDTH])
    i_s[...] = jax.nn.silu(_dot(xb, w_in_ref[:, base + 2 * HGRN_WIDTH:base + 3 * HGRN_WIDTH]))
    g_s[...] = jax.nn.silu(_dot(xb, w_in_ref[:, base + 3 * HGRN_WIDTH:base + 4 * HGRN_WIDTH]))

    crow = lax.broadcasted_iota(jnp.int32, (HGRN_CHUNK, HGRN_CHUNK), 0)
    ccol = lax.broadcasted_iota(jnp.int32, (HGRN_CHUNK, HGRN_CHUNK), 1)
    causal = ccol <= crow
    tril_ones = jnp.where(causal, 1.0, 0.0).astype(BF16)
    norm_g = bnorm_ref[...]

    for c in range(n_chunks):
        rows = slice(c * HGRN_CHUNK, (c + 1) * HGRN_CHUNK)
        f = lb + (1.0 - lb) * jax.nn.sigmoid(f_s[rows, :])
        log_f = jnp.log(f)
        log_f_hi = log_f.astype(BF16)
        log_f_lo = (log_f - log_f_hi.astype(F32)).astype(BF16)
        cum = _dot(tril_ones, log_f_hi) + _dot(tril_ones, log_f_lo)
        chunk_decay = jnp.exp(cum[HGRN_CHUNK - 1:HGRN_CHUNK, :])
        k_back = (1.0 - f) * jnp.exp(-cum)
        q_dec = (q_s[rows, :] * jnp.exp(cum)).astype(BF16)
        k_dec = k_back.astype(BF16)
        k_tail = (k_back * chunk_decay).astype(BF16)
        val = i_s[rows, :].astype(BF16)
        gate = g_s[rows, :]
        for h in range(HGRN_HEADS):
            lanes = slice(h * HGRN_DK, (h + 1) * HGRN_DK)
            attn = jnp.where(causal, _dot_nt(q_dec[:, lanes], k_dec[:, lanes]), 0.0).astype(BF16)
            state_t = state_s[h]
            o = _dot(attn, val[:, lanes]) + _dot_nt(q_dec[:, lanes], state_t.astype(BF16))
            state_s[h] = state_t * chunk_decay[:, lanes] + _dot_tn(val[:, lanes], k_tail[:, lanes])
            rms = lax.rsqrt(jnp.mean(o * o, axis=-1, keepdims=True) + LN_EPS)
            y_b = o * rms * norm_g[:, lanes] * gate[:, lanes]
            y_s[rows, GMLP_WIDTH + h * HGRN_DK:GMLP_WIDTH + (h + 1) * HGRN_DK] = y_b.astype(BF16)

    y = _dot(y_s[...], w_out_ref[...])
    o_ref[0] = _layer_norm(ALPHA * x + y, ln_g_ref[...], ln_b_ref[...])


def _even_mixer(x, lb_index, *params):
    bsz, t_len, d = x.shape
    tm = ROW_TILE
    kern = functools.partial(_even_mixer_kernel, lb_index=lb_index)
    return pl.pallas_call(
        kern,
        grid=(bsz, t_len // tm),
        in_specs=[pl.BlockSpec((1, tm, d), lambda b, t: (b, t, 0))] + [_resident(p) for p in params],
        out_specs=pl.BlockSpec((1, tm, d), lambda b, t: (b, t, 0)),
        out_shape=jax.ShapeDtypeStruct(x.shape, F32),
        scratch_shapes=[
            pltpu.VMEM((tm, HGRN_WIDTH), F32), pltpu.VMEM((tm, HGRN_WIDTH), F32),
            pltpu.VMEM((tm, HGRN_WIDTH), F32), pltpu.VMEM((tm, HGRN_WIDTH), F32),
            pltpu.VMEM((tm, GMLP_WIDTH + HGRN_WIDTH), BF16),
            pltpu.VMEM((HGRN_HEADS, HGRN_DK, HGRN_DK), F32),
        ],
        compiler_params=_params("parallel", "arbitrary"),
        name="even_mixer",
    )(x, *_stacks(*params))


def _mem_kv_kernel(mem_ref, w_ref, o_ref):
    o_ref[...] = _dot(mem_ref[...].astype(BF16), w_ref[...]).astype(BF16)


def _mem_kv(mem2d, w_kv):
    n, d = mem2d.shape
    tm = ROW_TILE
    width = w_kv[0].shape[-1]
    return pl.pallas_call(
        _mem_kv_kernel,
        grid=(n // tm,),
        in_specs=[pl.BlockSpec((tm, d), lambda i: (i, 0)), _resident(w_kv)],
        out_specs=pl.BlockSpec((tm, width), lambda i: (i, 0)),
        out_shape=jax.ShapeDtypeStruct((n, width), BF16),
        compiler_params=_params("parallel"),
        name="mem_kv",
    )(mem2d, *_stacks(w_kv))


def _cross_attn_kernel(x_ref, kv_ref, wq_ref, wo_ref, ln_g_ref, ln_b_ref, o_ref, att_s):
    x = x_ref[0]
    q = (_dot(x.astype(BF16), wq_ref[...]) * (MEM_HDIM ** -0.5)).astype(BF16)
    for h in range(MEM_HEADS):
        lanes = slice(h * MEM_HDIM, (h + 1) * MEM_HDIM)
        k_h = kv_ref[0, :, h * MEM_HDIM:(h + 1) * MEM_HDIM]
        v_h = kv_ref[0, :, D_MODEL + h * MEM_HDIM:D_MODEL + (h + 1) * MEM_HDIM]
        s = _dot_nt(q[:, lanes], k_h)
        p = jnp.exp(s - jnp.max(s, axis=-1, keepdims=True))
        denom = jnp.sum(p, axis=-1, keepdims=True)
        att_s[:, lanes] = (_dot(p.astype(BF16), v_h) / denom).astype(BF16)
    y = _dot(att_s[...], wo_ref[...])
    o_ref[0] = _layer_norm(ALPHA * x + y, ln_g_ref[...], ln_b_ref[...])


def _cross_attn(x, kv, *params):
    bsz, t_len, d = x.shape
    tm = ROW_TILE
    return pl.pallas_call(
        _cross_attn_kernel,
        grid=(bsz, t_len // tm),
        in_specs=[
            pl.BlockSpec((1, tm, d), lambda b, t: (b, t, 0)),
            pl.BlockSpec((1,) + kv.shape[1:], lambda b, t: (b, 0, 0)),
        ] + [_resident(p) for p in params],
        out_specs=pl.BlockSpec((1, tm, d), lambda b, t: (b, t, 0)),
        out_shape=jax.ShapeDtypeStruct(x.shape, F32),
        scratch_shapes=[pltpu.VMEM((tm, d), BF16)],
        compiler_params=_params("parallel", "parallel"),
        name="cross_attn",
    )(x, kv, *_stacks(*params))


def _ffn_kernel(x_ref, w_in_ref, w_out_ref, ln_g_ref, ln_b_ref, o_ref, acc_s):
    half = x_ref.shape[0] // 2
    n_chunks = D_FF // FFN_CHUNK

    def chunk(rows, xb, c):
        cols = slice(c * FFN_CHUNK, (c + 1) * FFN_CHUNK)
        gate = _dot(xb, w_in_ref[:, cols])
        up = _dot(xb, w_in_ref[:, D_FF + c * FFN_CHUNK:D_FF + (c + 1) * FFN_CHUNK])
        act = (jax.nn.silu(gate) * up).astype(BF16)
        part = _dot(act, w_out_ref[cols, :])
        if c == 0:
            acc_s[rows, :] = part
        else:
            acc_s[rows, :] += part

    def finish(rows):
        o_ref[rows, :] = _layer_norm(ALPHA * x_ref[rows, :] + acc_s[rows, :], ln_g_ref[...], ln_b_ref[...])

    first, second = slice(0, half), slice(half, 2 * half)
    xb_first = x_ref[first, :].astype(BF16)
    xb_second = x_ref[second, :].astype(BF16)
    for c in range(n_chunks):
        chunk(first, xb_first, c)
    chunk(second, xb_second, 0)
    finish(first)
    for c in range(1, n_chunks):
        chunk(second, xb_second, c)
    finish(second)


def _ffn(x2d, *params):
    n, d = x2d.shape
    tm = FFN_ROW_TILE
    assert n % tm == 0
    return pl.pallas_call(
        _ffn_kernel,
        grid=(n // tm,),
        in_specs=[pl.BlockSpec((tm, d), lambda i: (i, 0))] + [_resident(p) for p in params],
        out_specs=pl.BlockSpec((tm, d), lambda i: (i, 0)),
        out_shape=jax.ShapeDtypeStruct((n, d), F32),
        scratch_shapes=[pltpu.VMEM((tm, d), F32)],
        compiler_params=_params("parallel"),
        name="ffn",
    )(x2d, *_stacks(*params))


LOG2_E = 1.4426950408889634
ALIBI_LANES = 3


def _moba_qkv_kernel(x_ref, w_ref, fill_ref, q0_ref, q1_ref, k0_ref, k1_ref, vt_ref, kmean_ref,
                     *, tiles_per_seq):
    tm = x_ref.shape[0]
    xb = x_ref[...].astype(BF16)
    q = _dot(xb, w_ref[:, 0:D_MODEL]) * (MOBA_HDIM ** -0.5 * LOG2_E)
    k = _dot(xb, w_ref[:, D_MODEL:2 * D_MODEL])
    for i in range(tm // MOBA_BLOCK):
        kmean_ref[i] = jnp.mean(k[i * MOBA_BLOCK:(i + 1) * MOBA_BLOCK], axis=0, keepdims=True)
    vt_ref[0] = lax.dot_general(w_ref[:, 2 * D_MODEL:3 * D_MODEL], xb, (((0,), (1,)), ((), ())),
                                preferred_element_type=F32).astype(BF16)

    lane = lax.broadcasted_iota(jnp.int32, (tm, MOBA_PAIR), 1)
    pos = lax.broadcasted_iota(jnp.int32, (tm, MOBA_PAIR), 0) + (pl.program_id(0) % tiles_per_seq) * tm
    pos_block = ((pos // MOBA_BLOCK) * MOBA_BLOCK).astype(F32)
    pos_offset = (pos % MOBA_BLOCK).astype(F32)
    for e, (q_ref, k_ref) in enumerate(((q0_ref, k0_ref), (q1_ref, k1_ref))):
        own = (lane < MOBA_HDIM) if e == 0 else (lane >= MOBA_HDIM)
        partner = MOBA_HDIM * (1 - e)
        key_fill = jnp.where((lane >= partner) & (lane < partner + ALIBI_LANES), pos_block,
                             jnp.where((lane >= partner + ALIBI_LANES) & (lane < partner + 2 * ALIBI_LANES),
                                       pos_offset, 0.0))
        for p in range(MOBA_HEADS // 2):
            slab = slice(p * MOBA_PAIR, (p + 1) * MOBA_PAIR)
            q_ref[:, slab] = jnp.where(own, q[:, slab], fill_ref[e:e + 1, slab]).astype(BF16)
            k_ref[:, slab] = jnp.where(own, k[:, slab], key_fill).astype(BF16)


def _moba_qkv(x2d, w_qkv, fill, bsz, t_len):
    n, d = x2d.shape
    tm = ROW_TILE
    tiles_per_seq = t_len // tm
    rows = pl.BlockSpec((tm, d), lambda i: (i, 0))
    return pl.pallas_call(
        functools.partial(_moba_qkv_kernel, tiles_per_seq=tiles_per_seq),
        grid=(n // tm,),
        in_specs=[rows, _resident(w_qkv), _resident(fill)],
        out_specs=[
            rows, rows, rows, rows,
            pl.BlockSpec((1, d, tm), lambda i: (i // tiles_per_seq, 0, i % tiles_per_seq)),
            pl.BlockSpec((tm // MOBA_BLOCK, 1, d), lambda i: (i, 0, 0)),
        ],
        out_shape=[
            jax.ShapeDtypeStruct((n, d), BF16), jax.ShapeDtypeStruct((n, d), BF16),
            jax.ShapeDtypeStruct((n, d), BF16), jax.ShapeDtypeStruct((n, d), BF16),
            jax.ShapeDtypeStruct((bsz, d, t_len), BF16),
            jax.ShapeDtypeStruct((n // MOBA_BLOCK, 1, d), F32),
        ],
        compiler_params=_params("parallel"),
        name="moba_qkv",
    )(x2d, *_stacks(w_qkv, fill))


def _alibi_query_fill():
    slopes = jnp.asarray([2.0 ** (-8.0 * (h + 1) / MOBA_HEADS) for h in range(MOBA_HEADS)], F32) * LOG2_E
    pieces = []
    rest = slopes
    for _ in range(ALIBI_LANES):
        piece = rest.astype(BF16).astype(F32)
        pieces.append(piece)
        rest = rest - piece
    pieces = jnp.stack(pieces + pieces, axis=1)
    fill = jnp.zeros((2, MOBA_HEADS // 2, MOBA_PAIR), F32)
    for e in range(2):
        partner = MOBA_HDIM * (1 - e)
        fill = fill.at[e, :, partner:partner + 2 * ALIBI_LANES].set(pieces[e::2])
    return fill.reshape(2, D_MODEL)


def _moba_attn_kernel(*refs, first_block, n_query_blocks, fills_prior):
    if fills_prior:
        refs = refs[1:]
    (x_ref, q0_ref, q1_ref, k0_ref, k1_ref, vt_ref, kmean_ref, w_out_ref, ln_g_ref, ln_b_ref,
     o_ref, sa_s, sb_s, sha_s, shb_s, pa_s, pb_s, att_s) = refs
    qb = pl.program_id(1) + first_block
    n_blocks = kmean_ref.shape[1]
    blk = MOBA_BLOCK
    q_refs = (q0_ref, q1_ref)
    k_refs = (k0_ref, k1_ref)
    key_i = lax.broadcasted_iota(jnp.int32, (blk, blk), 0)
    qry_i = lax.broadcasted_iota(jnp.int32, (blk, blk), 1)
    causal = key_i <= qry_i
    blk_i = lax.broadcasted_iota(jnp.int32, (n_blocks, blk), 0)
    mean_lane = lax.broadcasted_iota(jnp.int32, (n_blocks, MOBA_PAIR), 1)

    def attend(n_past):
        n_keys = (n_past + 1) * blk
        select = n_past > MOBA_TOPK

        def pair_lanes(hp):
            return pl.ds(pl.multiple_of(hp * MOBA_PAIR, MOBA_PAIR), MOBA_PAIR)

        def scores_and_shifts(hp, s_buf, sh_buf):
            lanes = pair_lanes(hp)
            for e in range(2):
                cols = slice(e * blk, (e + 1) * blk)
                q_e = q_refs[e][0, :, lanes]
                if select:
                    own = (mean_lane < MOBA_HDIM) if e == 0 else (mean_lane >= MOBA_HDIM)
                    kmean = jnp.where(own, kmean_ref[0, :, lanes], 0.0)
                    kmean_hi = kmean.astype(BF16)
                    kmean_lo = (kmean - kmean_hi.astype(F32)).astype(BF16)
                    aff = _dot_nt(kmean_hi, q_e) + _dot_nt(kmean_lo, q_e)
                    rank = jnp.zeros((n_blocks, blk), F32)
                    for jp in range(n_past):
                        other = aff[jp:jp + 1, :]
                        beats = (other > aff) | ((other == aff) & (jp < blk_i))
                        rank = rank + jnp.where(beats, 1.0, 0.0)
                    chosen = rank < MOBA_TOPK
                m = None
                for j in range(n_past + 1):
                    rows = slice(j * blk, (j + 1) * blk)
                    t = _dot_nt(k_refs[e][0, rows, lanes], q_e)
                    if j == n_past:
                        t = jnp.where(causal, t, MASKED)
                    s_buf[rows, cols] = t
                    m_j = jnp.max(t, axis=0, keepdims=True)
                    if select and j < n_past:
                        m_j = jnp.where(chosen[j:j + 1, :], m_j, MASKED)
                    m = m_j if m is None else jnp.maximum(m, m_j)
                shifts = jnp.broadcast_to(m, (n_blocks, blk))
                if select:
                    shifts = jnp.where(chosen | (blk_i == n_past), shifts, -MASKED)
                sh_buf[e] = shifts

        def probabilities(s_buf, sh_buf, p_buf, after=None):
            hold = 0.0 if after is None else jnp.minimum(jnp.abs(after), 0.0)
            for e in range(2):
                cols = slice(e * blk, (e + 1) * blk)
                for j in range(n_past + 1):
                    rows = slice(j * blk, (j + 1) * blk)
                    shift = sh_buf[e, j:j + 1, :] + hold
                    p_buf[e, rows, :] = jnp.exp2(s_buf[rows, cols] - shift).astype(BF16)

        def weighted_values(hp, p_buf):
            for e in range(2):
                feat = pl.ds(pl.multiple_of(hp * MOBA_PAIR + e * MOBA_HDIM, MOBA_HDIM), MOBA_HDIM)
                values = jnp.concatenate([vt_ref[0, feat, 0:n_keys], jnp.ones((16, n_keys), BF16)], axis=0)
                acc = _dot(values, p_buf[e, 0:n_keys, :])
                denom = acc[MOBA_HDIM:MOBA_HDIM + 1]
                att_s[feat, :] = acc[0:MOBA_HDIM] / denom
            return denom

        n_pairs = MOBA_HEADS // 2
        scores_and_shifts(0, sa_s, sha_s)
        scores_and_shifts(1, sb_s, shb_s)
        probabilities(sa_s, sha_s, pa_s)

        def two_pairs(i, carry):
            hp = 2 * i + 1
            scores_and_shifts(hp + 1, sa_s, sha_s)
            probabilities(sb_s, shb_s, pb_s)
            consumed = weighted_values(hp - 1, pa_s)
            scores_and_shifts(hp + 2, sb_s, shb_s)
            probabilities(sa_s, sha_s, pa_s, after=consumed)
            weighted_values(hp, pb_s)
            return carry

        lax.fori_loop(0, n_pairs // 2 - 1, two_pairs, 0)
        probabilities(sb_s, shb_s, pb_s)
        weighted_values(n_pairs - 2, pa_s)
        weighted_values(n_pairs - 1, pb_s)

    for n_past in range(first_block, first_block + n_query_blocks):
        pl.when(qb == n_past)(functools.partial(attend, n_past))

    att = att_s[...].T.astype(BF16)
    y = _dot(att, w_out_ref[...])
    o_ref[0] = _layer_norm(ALPHA * x_ref[0] + y, ln_g_ref[...], ln_b_ref[...])


def _moba_attn(prior, first_block, n_query_blocks, x, q0, q1, k0, k1, vt, kmean, *params):
    bsz, t_len, d = x.shape
    blk = MOBA_BLOCK
    n_blocks = t_len // blk
    n_keys = (first_block + n_query_blocks) * blk
    query_rows = pl.BlockSpec((1, blk, d), lambda b, t: (b, t + first_block, 0))
    visible_keys = pl.BlockSpec((1, n_keys, d), lambda b, t: (b, 0, 0))
    fills_prior = prior is not None
    return pl.pallas_call(
        functools.partial(_moba_attn_kernel, first_block=first_block, n_query_blocks=n_query_blocks,
                          fills_prior=fills_prior),
        grid=(bsz, n_query_blocks),
        in_specs=([pl.BlockSpec(memory_space=pl.ANY)] if fills_prior else []) + [
            query_rows, query_rows, query_rows, visible_keys, visible_keys,
            pl.BlockSpec((1, d, n_keys), lambda b, t: (b, 0, 0)),
            pl.BlockSpec((1, n_blocks, d), lambda b, t: (b, 0, 0)),
        ] + [_resident(p) for p in params],
        out_specs=query_rows,
        out_shape=jax.ShapeDtypeStruct(x.shape, F32),
        input_output_aliases={0: 0} if fills_prior else {},
        scratch_shapes=[
            pltpu.VMEM((n_keys, 2 * blk), F32),
            pltpu.VMEM((n_keys, 2 * blk), F32),
            pltpu.VMEM((2, n_blocks, blk), F32),
            pltpu.VMEM((2, n_blocks, blk), F32),
            pltpu.VMEM((2, n_keys, blk), BF16),
            pltpu.VMEM((2, n_keys, blk), BF16),
            pltpu.VMEM((d, blk), F32),
        ],
        compiler_params=_params("parallel", "arbitrary"),
        name=f"moba_attn_from_block_{first_block}",
    )(*([prior] if fills_prior else []), x, q0, q1, k0, k1, vt, kmean, *_stacks(*params))


def kernel(x, mem, ln_g, ln_b, x_wq, x_wkv, x_wo, ffn_w_in, ffn_w_out, ev_w_in, ev_w_out, a_ws, a_bs,
           a_ln_g, a_ln_b, b_norm_g, hgrn_lb_logits, od_w_qkv, od_w_out):
    bsz, t_len, d = x.shape
    assert d == D_MODEL and t_len % ROW_TILE == 0 and t_len % MOBA_BLOCK == 0
    assert ROW_TILE % MOBA_BLOCK == 0 and t_len // MOBA_BLOCK > 1
    n = bsz * t_len
    mem2d = mem.reshape(bsz * mem.shape[1], d)

    def rows_of(v):
        return v.reshape(-1, 1, v.shape[-1])

    def seq(v):
        return v.reshape(bsz, t_len, d)

    x_wq, x_wkv, x_wo, ffn_w_in, ffn_w_out, ev_w_in, ev_w_out, od_w_qkv, od_w_out = (
        w.astype(BF16) for w in (x_wq, x_wkv, x_wo, ffn_w_in, ffn_w_out, ev_w_in, ev_w_out, od_w_qkv, od_w_out))
    ln_g, ln_b = rows_of(ln_g), rows_of(ln_b)
    a_ln_g, a_ln_b, b_norm_g = rows_of(a_ln_g), rows_of(a_ln_b), rows_of(b_norm_g)
    a_bs_t = jnp.swapaxes(a_bs, 1, 2)
    lb_logits = hgrn_lb_logits[None]
    alibi_fill = _alibi_query_fill()[None]

    for layer in range(DEPTH):
        j = layer // 2
        norm = [(ln_g, 3 * layer), (ln_b, 3 * layer)]
        if layer % 2 == 0:
            x = _even_mixer(
                x, j, (ev_w_in, j), (ev_w_out, j), (a_ws, j), (a_bs_t, j), (a_ln_g, j), (a_ln_b, j),
                (b_norm_g, j), (lb_logits, 0), *norm)
        else:
            q0, q1, k0, k1, vt, kmean = _moba_qkv(x.reshape(n, d), (od_w_qkv, j), (alibi_fill, 0), bsz, t_len)
            attn_args = (x, seq(q0), seq(q1), seq(k0), seq(k1), vt,
                         kmean.reshape(bsz, t_len // MOBA_BLOCK, d), (od_w_out, j), *norm)
            n_blocks = t_len // MOBA_BLOCK
            filled = None
            for first_block in range(0, n_blocks, MOBA_BLOCKS_PER_CALL):
                filled = _moba_attn(filled, first_block, min(MOBA_BLOCKS_PER_CALL, n_blocks - first_block),
                                    *attn_args)
            x = filled
        kv = _mem_kv(mem2d, (x_wkv, layer)).reshape(bsz, mem.shape[1], 2 * d)
        x = _cross_attn(x, kv, (x_wq, layer), (x_wo, layer), (ln_g, 3 * layer + 1), (ln_b, 3 * layer + 1))
        x = _ffn(x.reshape(n, d), (ffn_w_in, layer), (ffn_w_out, layer),
                 (ln_g, 3 * layer + 2), (ln_b, 3 * layer + 2)).reshape(bsz, t_len, d)
    return x
```

```python
import functools
import math

import jax
import jax.numpy as jnp
from jax import lax
from jax.experimental import pallas as pl
from jax.experimental.pallas import tpu as pltpu

D_MODEL = 1024
DEPTH = 2
ALPHA = (2.0 * DEPTH) ** 0.25
LN_EPS = 1e-5

GMLP_WIDTH = D_MODEL // 2
GMLP_GROUPS = 4
GMLP_GDIM = GMLP_WIDTH // GMLP_GROUPS
GMLP_CHUNK = 128
HGRN_WIDTH = D_MODEL // 2
HGRN_HEADS = 4
HGRN_DK = HGRN_WIDTH // HGRN_HEADS
HGRN_CHUNK = 64
EVEN_IN_WIDTH = 2 * GMLP_WIDTH + 4 * HGRN_WIDTH

MOBA_HEADS = 16
MOBA_HDIM = D_MODEL // MOBA_HEADS
MOBA_BLOCK = 256
MOBA_TOPK = 3
MOBA_PAIR = 2 * MOBA_HDIM
MOBA_BLOCKS_PER_CALL = 4

MEM_HEADS = 4
MEM_HDIM = D_MODEL // MEM_HEADS

D_FF = int(math.ceil(8 * D_MODEL / 3 / 256)) * 256
FFN_CHUNK = 256

ROW_TILE = 512
FFN_ROW_TILE = 1024
FFN_NORM_SLICES = 8
V7X_VMEM_LIMIT = 56 * 1024 * 1024

MASKED = -1e30

BF16 = jnp.bfloat16
F32 = jnp.float32


def _dot(a, b):
    return jnp.dot(a, b, preferred_element_type=F32)


def _dot_nt(a, b):
    return lax.dot_general(a, b, (((1,), (1,)), ((), ())), preferred_element_type=F32)


def _dot_tn(a, b):
    return lax.dot_general(a, b, (((0,), (0,)), ((), ())), preferred_element_type=F32)


def _layer_norm(z, g, b):
    mu = jnp.mean(z, axis=-1, keepdims=True)
    zc = z - mu
    var = jnp.mean(zc * zc, axis=-1, keepdims=True)
    return zc * lax.rsqrt(var + LN_EPS) * g + b


def _zero_after(token):
    bits = lax.bitcast_convert_type(token, jnp.int32)
    return lax.shift_right_logical(lax.shift_right_logical(bits, 16), 16).astype(F32)


def _residual_norm(x, y, g, b, after=None):
    alpha = ALPHA if after is None else ALPHA + _zero_after(after)
    return _layer_norm(x * alpha + y, g, b)


def _resident(picked):
    stack, index = picked
    return pl.BlockSpec((None,) + stack.shape[1:], lambda *_: (index,) + (0,) * (stack.ndim - 1),
                        pipeline_mode=pl.Buffered(1))


def _stacks(*picked):
    return [stack for stack, _ in picked]


def _params(*semantics):
    return pltpu.CompilerParams(dimension_semantics=semantics, vmem_limit_bytes=V7X_VMEM_LIMIT)


def _even_mixer_kernel(x_ref, w_in_ref, w_out_ref, ws_ref, bs_ref, aln_g_ref, aln_b_ref, bnorm_ref,
                       lb_logits_ref, ln_g_ref, ln_b_ref, o_ref,
                       q_s, f_s, i_s, g_s, y_s, state_s, *, lb_index):
    tm = x_ref.shape[1]
    x = x_ref[0]
    xb = x.astype(BF16)

    @pl.when(pl.program_id(1) == 0)
    def _():
        state_s[...] = jnp.zeros_like(state_s)

    base = 2 * GMLP_WIDTH
    u_pre = _dot(xb, w_in_ref[:, 0:GMLP_WIDTH])
    v_pre = _dot(xb, w_in_ref[:, GMLP_WIDTH:2 * GMLP_WIDTH])
    q_pre = _dot(xb, w_in_ref[:, base:base + HGRN_WIDTH])
    q_s[...] = q_pre
    f_pre = _dot(xb, w_in_ref[:, base + HGRN_WIDTH:base + 2 * HGRN_WIDTH])
    f_s[...] = f_pre
    i_s[...] = jax.nn.silu(_dot(xb, w_in_ref[:, base + 2 * HGRN_WIDTH:base + 3 * HGRN_WIDTH]))
    g_s[...] = jax.nn.silu(_dot(xb, w_in_ref[:, base + 3 * HGRN_WIDTH:base + 4 * HGRN_WIDTH]))

    u = jax.nn.gelu(u_pre + _zero_after(q_pre[0:1, 0:1]))
    v = jax.nn.gelu(v_pre + _zero_after(f_pre[0:1, 0:1]))
    row = lax.broadcasted_iota(jnp.int32, (GMLP_CHUNK, GMLP_CHUNK), 0)
    col = lax.broadcasted_iota(jnp.int32, (GMLP_CHUNK, GMLP_CHUNK), 1)
    for g in range(GMLP_GROUPS):
        lanes = slice(g * GMLP_GDIM, (g + 1) * GMLP_GDIM)
        vn = _layer_norm(v[:, lanes], aln_g_ref[:, lanes], aln_b_ref[:, lanes]).astype(BF16)
        wg = jnp.where(col <= row, ws_ref[g], 0.0).astype(BF16)
        bias = bs_ref[:, g:g + 1]
        for c in range(tm // GMLP_CHUNK):
            rows = slice(c * GMLP_CHUNK, (c + 1) * GMLP_CHUNK)
            s = _dot(wg, vn[rows]) + bias
            y_s[rows, lanes] = (u[rows, lanes] * s).astype(BF16)

    n_chunks = tm // HGRN_CHUNK
    logits = lb_logits_ref[...]
    e = jnp.exp(logits - jnp.max(logits, axis=0, keepdims=True))
    lb = jnp.sum(e[0:lb_index + 1], axis=0, keepdims=True) / jnp.sum(e, axis=0, keepdims=True)

    crow =lax.broadcasted_iota(jnp.int32, (HGRN_CHUNK, HGRN_CHUNK), 0)
    ccol = lax.broadcasted_iota(jnp.int32, (HGRN_CHUNK, HGRN_CHUNK), 1)
    causal = ccol <= crow
    tril_ones = jnp.where(causal, 1.0, 0.0).astype(BF16)
    norm_g = bnorm_ref[...]

    for c in range(n_chunks):
        rows = slice(c * HGRN_CHUNK, (c + 1) * HGRN_CHUNK)
        f = lb + (1.0 - lb) * jax.nn.sigmoid(f_s[rows, :])
        log_f = jnp.log(f)
        log_f_hi = log_f.astype(BF16)
        log_f_lo = (log_f - log_f_hi.astype(F32)).astype(BF16)
        cum = _dot(tril_ones, log_f_hi) + _dot(tril_ones, log_f_lo)
        chunk_decay = jnp.exp(cum[HGRN_CHUNK - 1:HGRN_CHUNK, :])
        k_back = (1.0 - f) * jnp.exp(-cum)
        q_dec = (q_s[rows, :] * jnp.exp(cum)).astype(BF16)
        k_dec = k_back.astype(BF16)
        k_tail = (k_back * chunk_decay).astype(BF16)
        val = i_s[rows, :].astype(BF16)
        gate = g_s[rows, :]
        for h in range(HGRN_HEADS):
            lanes = slice(h * HGRN_DK, (h + 1) * HGRN_DK)
            attn = jnp.where(causal, _dot_nt(q_dec[:, lanes], k_dec[:, lanes]), 0.0).astype(BF16)
            state_t = state_s[h]
            o = _dot(attn, val[:, lanes]) + _dot_nt(q_dec[:, lanes], state_t.astype(BF16))
            state_s[h] = state_t * chunk_decay[:, lanes] + _dot_tn(val[:, lanes], k_tail[:, lanes])
            rms = lax.rsqrt(jnp.mean(o * o, axis=-1, keepdims=True) + LN_EPS)
            y_b = o * rms * norm_g[:, lanes] * gate[:, lanes]
            y_s[rows, GMLP_WIDTH + h * HGRN_DK:GMLP_WIDTH + (h + 1) * HGRN_DK] = y_b.astype(BF16)

    for rows in (slice(0, tm // 2), slice(tm // 2, tm)):
        y = _dot(y_s[rows, :], w_out_ref[...])
        o_ref[0, rows, :] = _residual_norm(x_ref[0, rows, :], y, ln_g_ref[...], ln_b_ref[...])


def _even_mixer(x, lb_index, *params):
    bsz, t_len, d = x.shape
    tm = ROW_TILE
    kern = functools.partial(_even_mixer_kernel, lb_index=lb_index)
    return pl.pallas_call(
        kern,
        grid=(bsz, t_len // tm),
        in_specs=[pl.BlockSpec((1, tm, d), lambda b, t: (b, t, 0))] + [_resident(p) for p in params],
        out_specs=pl.BlockSpec((1, tm, d), lambda b, t: (b, t, 0)),
        out_shape=jax.ShapeDtypeStruct(x.shape, F32),
        scratch_shapes=[
            pltpu.VMEM((tm, HGRN_WIDTH), F32), pltpu.VMEM((tm, HGRN_WIDTH), F32),
            pltpu.VMEM((tm, HGRN_WIDTH), F32), pltpu.VMEM((tm, HGRN_WIDTH), F32),
            pltpu.VMEM((tm, GMLP_WIDTH + HGRN_WIDTH), BF16),
            pltpu.VMEM((HGRN_HEADS, HGRN_DK, HGRN_DK), F32),
        ],
        compiler_params=_params("parallel", "arbitrary"),
        name="even_mixer",
    )(x, *_stacks(*params))


def _mem_kv_kernel(mem_ref, w_ref, o_ref):
    o_ref[...] = _dot(mem_ref[...].astype(BF16), w_ref[...]).astype(BF16)


def _mem_kv(mem2d, w_kv):
    n, d = mem2d.shape
    tm = ROW_TILE
    width = w_kv[0].shape[-1]
    return pl.pallas_call(
        _mem_kv_kernel,
        grid=(n // tm,),
        in_specs=[pl.BlockSpec((tm, d), lambda i: (i, 0)), _resident(w_kv)],
        out_specs=pl.BlockSpec((tm, width), lambda i: (i, 0)),
        out_shape=jax.ShapeDtypeStruct((n, width), BF16),
        compiler_params=_params("parallel"),
        name="mem_kv",
    )(mem2d, *_stacks(w_kv))


def _cross_attn_kernel(x_ref, kv_ref, wq_ref, wo_ref, ln_g_ref, ln_b_ref, o_ref, att_s, y_s):
    half = x_ref.shape[1] // 2
    slice_rows = half // MEM_HEADS

    def attend(rows):
        q = (_dot(x_ref[0, rows, :].astype(BF16), wq_ref[...]) * (MEM_HDIM ** -0.5)).astype(BF16)
        done = []
        for h in range(MEM_HEADS):
            lanes = slice(h * MEM_HDIM, (h + 1) * MEM_HDIM)
            k_h = kv_ref[0, :, h * MEM_HDIM:(h + 1) * MEM_HDIM]
            v_h = kv_ref[0, :, D_MODEL + h * MEM_HDIM:D_MODEL + (h + 1) * MEM_HDIM]
            s = _dot_nt(q[:, lanes], k_h)
            p = jnp.exp(s - jnp.max(s, axis=-1, keepdims=True))
            denom = jnp.sum(p, axis=-1, keepdims=True)
            weighted = _dot(p.astype(BF16), v_h)
            att_s[rows, lanes] = (weighted / denom).astype(BF16)
            done.append(weighted[0:1, 0:1])
        y_s[rows, :] = _dot(att_s[rows, :], wo_ref[...])
        return done

    def finish(rows, after=None):
        o_ref[0, rows, :] = _residual_norm(x_ref[0, rows, :], y_s[rows, :], ln_g_ref[...], ln_b_ref[...], after)

    attend(slice(0, half))
    done = attend(slice(half, 2 * half))
    for h in range(MEM_HEADS):
        finish(slice(h * slice_rows, (h + 1) * slice_rows), after=done[h])
    finish(slice(half, 2 * half))


def _cross_attn(x, kv, *params):
    bsz, t_len, d = x.shape
    tm = ROW_TILE
    return pl.pallas_call(
        _cross_attn_kernel,
        grid=(bsz, t_len // tm),
        in_specs=[
            pl.BlockSpec((1, tm, d), lambda b, t: (b, t, 0)),
            pl.BlockSpec((1,) + kv.shape[1:], lambda b, t: (b, 0, 0)),
        ] + [_resident(p) for p in params],
        out_specs=pl.BlockSpec((1, tm, d), lambda b, t: (b, t, 0)),
        out_shape=jax.ShapeDtypeStruct(x.shape, F32),
        scratch_shapes=[pltpu.VMEM((tm, d), BF16), pltpu.VMEM((tm, d), F32)],
        compiler_params=_params("parallel", "parallel"),
        name="cross_attn",
    )(x, kv, *_stacks(*params))


def _ffn_kernel(x_ref, w_in_ref, w_out_ref, ln_g_ref, ln_b_ref, o_ref, acc_s):
    half = x_ref.shape[0] // 2
    n_chunks = D_FF // FFN_CHUNK
    slice_rows = half // FFN_NORM_SLICES

    def chunk(rows, xb, c):
        cols = slice(c * FFN_CHUNK, (c + 1) * FFN_CHUNK)
        gate = _dot(xb, w_in_ref[:, cols])
        up = _dot(xb, w_in_ref[:, D_FF + c * FFN_CHUNK:D_FF + (c + 1) * FFN_CHUNK])
        act = (jax.nn.silu(gate) * up).astype(BF16)
        part = _dot(act, w_out_ref[cols, :])
        if c == 0:
            acc_s[rows, :] = part
        else:
            acc_s[rows, :] += part
        return part[0:1, :]

    def finish(rows, after=None):
        o_ref[rows, :] = _residual_norm(x_ref[rows, :], acc_s[rows, :], ln_g_ref[...], ln_b_ref[...], after)

    first, second = slice(0, half), slice(half, 2 * half)
    xb_first = x_ref[first, :].astype(BF16)
    xb_second = x_ref[second, :].astype(BF16)
    for c in range(n_chunks):
        chunk(first, xb_first, c)
    for c in range(n_chunks):
        done = chunk(second, xb_second, c)
        if c < FFN_NORM_SLICES:
            finish(slice(c * slice_rows, (c + 1) * slice_rows), after=done)
    finish(second)


def _ffn(x2d, *params):
    n, d = x2d.shape
    tm = FFN_ROW_TILE
    assert n % tm == 0
    return pl.pallas_call(
        _ffn_kernel,
        grid=(n // tm,),
        in_specs=[pl.BlockSpec((tm, d), lambda i: (i, 0))] + [_resident(p) for p in params],
        out_specs=pl.BlockSpec((tm, d), lambda i: (i, 0)),
        out_shape=jax.ShapeDtypeStruct((n, d), F32),
        scratch_shapes=[pltpu.VMEM((tm, d), F32)],
        compiler_params=_params("parallel"),
        name="ffn",
    )(x2d, *_stacks(*params))


LOG2_E = 1.4426950408889634
ALIBI_LANES = 3


def _moba_qkv_kernel(x_ref, w_ref, fill_ref, q0_ref, q1_ref, k0_ref, k1_ref, vt_ref, kmean_ref,
                     *, tiles_per_seq):
    tm = x_ref.shape[0]
    xb = x_ref[...].astype(BF16)
    q = _dot(xb, w_ref[:, 0:D_MODEL]) * (MOBA_HDIM ** -0.5 * LOG2_E)
    k = _dot(xb, w_ref[:, D_MODEL:2 * D_MODEL])
    for i in range(tm // MOBA_BLOCK):
        kmean_ref[i] = jnp.mean(k[i * MOBA_BLOCK:(i + 1) * MOBA_BLOCK], axis=0, keepdims=True)
    vt_ref[0] = lax.dot_general(w_ref[:, 2 * D_MODEL:3 * D_MODEL], xb, (((0,), (1,)), ((), ())),
                                preferred_element_type=F32).astype(BF16)

    lane = lax.broadcasted_iota(jnp.int32, (tm, MOBA_PAIR), 1)
    pos = lax.broadcasted_iota(jnp.int32, (tm, MOBA_PAIR), 0) + (pl.program_id(0) % tiles_per_seq) * tm
    pos_block = ((pos // MOBA_BLOCK) * MOBA_BLOCK).astype(F32)
    pos_offset = (pos % MOBA_BLOCK).astype(F32)
    for e, (q_ref, k_ref) in enumerate(((q0_ref, k0_ref), (q1_ref, k1_ref))):
        own = (lane < MOBA_HDIM) if e == 0 else (lane >= MOBA_HDIM)
        partner = MOBA_HDIM * (1 - e)
        key_fill = jnp.where((lane >= partner) & (lane < partner + ALIBI_LANES), pos_block,
                             jnp.where((lane >= partner + ALIBI_LANES) & (lane < partner + 2 * ALIBI_LANES),
                                       pos_offset, 0.0))
        for p in range(MOBA_HEADS // 2):
            slab = slice(p * MOBA_PAIR, (p + 1) * MOBA_PAIR)
            q_ref[:, slab] = jnp.where(own, q[:, slab], fill_ref[e:e + 1, slab]).astype(BF16)
            k_ref[:, slab] = jnp.where(own, k[:, slab], key_fill).astype(BF16)


def _moba_qkv(x2d, w_qkv, fill, bsz, t_len):
    n, d = x2d.shape
    tm = ROW_TILE
    tiles_per_seq = t_len // tm
    rows = pl.BlockSpec((tm, d), lambda i: (i, 0))
    return pl.pallas_call(
        functools.partial(_moba_qkv_kernel, tiles_per_seq=tiles_per_seq),
        grid=(n // tm,),
        in_specs=[rows, _resident(w_qkv), _resident(fill)],
        out_specs=[
            rows, rows, rows, rows,
            pl.BlockSpec((1, d, tm), lambda i: (i // tiles_per_seq, 0, i % tiles_per_seq)),
            pl.BlockSpec((tm // MOBA_BLOCK, 1, d), lambda i: (i, 0, 0)),
        ],
        out_shape=[
            jax.ShapeDtypeStruct((n, d), BF16), jax.ShapeDtypeStruct((n, d), BF16),
            jax.ShapeDtypeStruct((n, d), BF16), jax.ShapeDtypeStruct((n, d), BF16),
            jax.ShapeDtypeStruct((bsz, d, t_len), BF16),
            jax.ShapeDtypeStruct((n // MOBA_BLOCK, 1, d), F32),
        ],
        compiler_params=_params("parallel"),
        name="moba_qkv",
    )(x2d, *_stacks(w_qkv, fill))


def _alibi_query_fill():
    slopes = jnp.asarray([2.0 ** (-8.0 * (h + 1) / MOBA_HEADS) for h in range(MOBA_HEADS)], F32) * LOG2_E
    pieces = []
    rest = slopes
    for _ in range(ALIBI_LANES):
        piece = rest.astype(BF16).astype(F32)
        pieces.append(piece)
        rest = rest - piece
    pieces = jnp.stack(pieces + pieces, axis=1)
    fill = jnp.zeros((2, MOBA_HEADS // 2, MOBA_PAIR), F32)
    for e in range(2):
        partner = MOBA_HDIM * (1 - e)
        fill = fill.at[e, :, partner:partner + 2 * ALIBI_LANES].set(pieces[e::2])
    return fill.reshape(2, D_MODEL)


def _moba_attn_kernel(*refs, first_block, n_query_blocks, fills_prior):
    if fills_prior:
        refs = refs[1:]
    (x_ref, q0_ref, q1_ref, k0_ref, k1_ref, vt_ref, kmean_ref, w_out_ref, ln_g_ref, ln_b_ref,
     o_ref, sa_s, sb_s, sha_s, shb_s, pa_s, pb_s, att_s) = refs
    qb = pl.program_id(1) + first_block
    n_blocks = kmean_ref.shape[1]
    blk = MOBA_BLOCK
    q_refs = (q0_ref, q1_ref)
    k_refs = (k0_ref, k1_ref)
    key_i = lax.broadcasted_iota(jnp.int32, (blk, blk), 0)
    qry_i = lax.broadcasted_iota(jnp.int32, (blk, blk), 1)
    causal = key_i <= qry_i
    blk_i = lax.broadcasted_iota(jnp.int32, (n_blocks, blk), 0)
    mean_lane = lax.broadcasted_iota(jnp.int32, (n_blocks, MOBA_PAIR), 1)

    def attend(n_past):
        n_keys = (n_past + 1) * blk
        select = n_past > MOBA_TOPK

        def pair_lanes(hp):
            return pl.ds(pl.multiple_of(hp * MOBA_PAIR, MOBA_PAIR), MOBA_PAIR)

        def scores_and_shifts(hp, s_buf, sh_buf):
            lanes = pair_lanes(hp)
            for e in range(2):
                cols = slice(e * blk, (e + 1) * blk)
                q_e = q_refs[e][0, :, lanes]
                if select:
                    own = (mean_lane < MOBA_HDIM) if e == 0 else (mean_lane >= MOBA_HDIM)
                    kmean = jnp.where(own, kmean_ref[0, :, lanes], 0.0)
                    kmean_hi = kmean.astype(BF16)
                    kmean_lo = (kmean - kmean_hi.astype(F32)).astype(BF16)
                    aff = _dot_nt(kmean_hi, q_e) + _dot_nt(kmean_lo, q_e)
                    rank = jnp.zeros((n_blocks, blk), F32)
                    for jp in range(n_past):
                        other = aff[jp:jp + 1, :]
                        beats = (other > aff) | ((other == aff) & (jp < blk_i))
                        rank = rank + jnp.where(beats, 1.0, 0.0)
                    chosen = rank < MOBA_TOPK
                m = None
                for j in range(n_past + 1):
                    rows = slice(j * blk, (j + 1) * blk)
                    t = _dot_nt(k_refs[e][0, rows, lanes], q_e)
                    if j == n_past:
                        t = jnp.where(causal, t, MASKED)
                    s_buf[rows, cols] = t
                    m_j = jnp.max(t, axis=0, keepdims=True)
                    if select and j < n_past:
                        m_j = jnp.where(chosen[j:j + 1, :], m_j, MASKED)
                    m = m_j if m is None else jnp.maximum(m, m_j)
                shifts = jnp.broadcast_to(m, (n_blocks, blk))
                if select:
                    shifts = jnp.where(chosen | (blk_i == n_past), shifts, -MASKED)
                sh_buf[e] = shifts

        def probabilities(s_buf, sh_buf, p_buf, after=None):
            hold = 0.0 if after is None else _zero_after(after)
            for e in range(2):
                cols = slice(e * blk, (e + 1) * blk)
                for j in range(n_past + 1):
                    rows = slice(j * blk, (j + 1) * blk)
                    shift = sh_buf[e, j:j + 1, :] + hold
                    p_buf[e, rows, :] = jnp.exp2(s_buf[rows, cols] - shift).astype(BF16)

        def weighted_values(hp, p_buf):
            for e in range(2):
                feat = pl.ds(pl.multiple_of(hp * MOBA_PAIR + e * MOBA_HDIM, MOBA_HDIM), MOBA_HDIM)
                values = jnp.concatenate([vt_ref[0, feat, 0:n_keys], jnp.ones((16, n_keys), BF16)], axis=0)
                acc = _dot(values, p_buf[e, 0:n_keys, :])
                denom = acc[MOBA_HDIM:MOBA_HDIM + 1]
                att_s[feat, :] = acc[0:MOBA_HDIM] / denom
            return denom

        n_pairs = MOBA_HEADS // 2
        scores_and_shifts(0, sa_s, sha_s)
        scores_and_shifts(1, sb_s, shb_s)
        probabilities(sa_s, sha_s, pa_s)

        def two_pairs(i, carry):
            hp = 2 * i + 1
            scores_and_shifts(hp + 1, sa_s, sha_s)
            probabilities(sb_s, shb_s, pb_s)
            consumed = weighted_values(hp - 1, pa_s)
            scores_and_shifts(hp + 2, sb_s, shb_s)
            probabilities(sa_s, sha_s, pa_s, after=consumed)
            weighted_values(hp, pb_s)
            return carry

        lax.fori_loop(0, n_pairs // 2 - 1, two_pairs, 0)
        probabilities(sb_s, shb_s, pb_s)
        weighted_values(n_pairs - 2, pa_s)
        weighted_values(n_pairs - 1, pb_s)

    for n_past in range(first_block, first_block + n_query_blocks):
        pl.when(qb == n_past)(functools.partial(attend, n_past))

    att = att_s[...].T.astype(BF16)
    y = _dot(att, w_out_ref[...])
    o_ref[0] = _layer_norm(ALPHA * x_ref[0] + y, ln_g_ref[...], ln_b_ref[...])


def _moba_attn(prior, first_block, n_query_blocks, x, q0, q1, k0, k1, vt, kmean, *params):
    bsz, t_len, d = x.shape
    blk = MOBA_BLOCK
    n_blocks = t_len // blk
    n_keys = (first_block + n_query_blocks) * blk
    query_rows = pl.BlockSpec((1, blk, d), lambda b, t: (b, t + first_block, 0))
    visible_keys = pl.BlockSpec((1, n_keys, d), lambda b, t: (b, 0, 0))
    fills_prior = prior is not None
    return pl.pallas_call(
        functools.partial(_moba_attn_kernel, first_block=first_block, n_query_blocks=n_query_blocks,
                          fills_prior=fills_prior),
        grid=(bsz, n_query_blocks),
        in_specs=([pl.BlockSpec(memory_space=pl.ANY)] if fills_prior else []) + [
            query_rows, query_rows, query_rows, visible_keys, visible_keys,
            pl.BlockSpec((1, d, n_keys), lambda b, t: (b, 0, 0)),
            pl.BlockSpec((1, n_blocks, d), lambda b, t: (b, 0, 0)),
        ] + [_resident(p) for p in params],
        out_specs=query_rows,
        out_shape=jax.ShapeDtypeStruct(x.shape, F32),
        input_output_aliases={0: 0} if fills_prior else {},
        scratch_shapes=[
            pltpu.VMEM((n_keys, 2 * blk), F32),
            pltpu.VMEM((n_keys, 2 * blk), F32),
            pltpu.VMEM((2, n_blocks, blk), F32),
            pltpu.VMEM((2, n_blocks, blk), F32),
            pltpu.VMEM((2, n_keys, blk), BF16),
            pltpu.VMEM((2, n_keys, blk), BF16),
            pltpu.VMEM((d, blk), F32),
        ],
        compiler_params=_params("parallel", "arbitrary"),
        name=f"moba_attn_from_block_{first_block}",
    )(*([prior] if fills_prior else []), x, q0, q1, k0, k1, vt, kmean, *_stacks(*params))


def kernel(x, mem, ln_g, ln_b, x_wq, x_wkv, x_wo, ffn_w_in, ffn_w_out, ev_w_in, ev_w_out, a_ws, a_bs,
           a_ln_g, a_ln_b, b_norm_g, hgrn_lb_logits, od_w_qkv, od_w_out):
    bsz, t_len, d = x.shape
    assert d == D_MODEL and t_len % ROW_TILE == 0 and t_len % MOBA_BLOCK == 0
    assert ROW_TILE % MOBA_BLOCK == 0 and t_len // MOBA_BLOCK > 1
    n = bsz * t_len
    mem2d = mem.reshape(bsz * mem.shape[1], d)

    def rows_of(v):
        return v.reshape(-1, 1, v.shape[-1])

    def seq(v):
        return v.reshape(bsz, t_len, d)

    x_wq, x_wkv, x_wo, ffn_w_in, ffn_w_out, ev_w_in, ev_w_out, od_w_qkv, od_w_out = (
        w.astype(BF16) for w in (x_wq, x_wkv, x_wo, ffn_w_in, ffn_w_out, ev_w_in, ev_w_out, od_w_qkv, od_w_out))
    ln_g, ln_b = rows_of(ln_g), rows_of(ln_b)
    a_ln_g, a_ln_b, b_norm_g = rows_of(a_ln_g), rows_of(a_ln_b), rows_of(b_norm_g)
    a_bs_t = jnp.swapaxes(a_bs, 1, 2)
    lb_logits = hgrn_lb_logits[None]
    alibi_fill = _alibi_query_fill()[None]

    for layer in range(DEPTH):
        j = layer // 2
        norm = [(ln_g, 3 * layer), (ln_b, 3 * layer)]
        if layer % 2 == 0:
            x = _even_mixer(
                x, j, (ev_w_in, j), (ev_w_out, j), (a_ws, j), (a_bs_t, j), (a_ln_g, j), (a_ln_b, j),
                (b_norm_g, j), (lb_logits, 0), *norm)
        else:
            q0, q1, k0, k1, vt, kmean = _moba_qkv(x.reshape(n, d), (od_w_qkv, j), (alibi_fill, 0), bsz, t_len)
            attn_args = (x, seq(q0), seq(q1), seq(k0), seq(k1), vt,
                         kmean.reshape(bsz, t_len // MOBA_BLOCK, d), (od_w_out, j), *norm)
            n_blocks = t_len // MOBA_BLOCK
            filled = None
            for first_block in range(0, n_blocks, MOBA_BLOCKS_PER_CALL):
                filled = _moba_attn(filled, first_block, min(MOBA_BLOCKS_PER_CALL, n_blocks - first_block),
                                    *attn_args)
            x = filled
        kv = _mem_kv(mem2d, (x_wkv, layer)).reshape(bsz, mem.shape[1], 2 * d)
        x = _cross_attn(x, kv, (x_wq, layer), (x_wo, layer), (ln_g, 3 * layer + 1), (ln_b, 3 * layer + 1))
        x = _ffn(x.reshape(n, d), (ffn_w_in, layer), (ffn_w_out, layer),
                 (ln_g, 3 * layer + 2), (ln_b, 3 * layer + 2)).reshape(bsz, t_len, d)
    return x
```

```python
import functools
import math

import jax
import jax.numpy as jnp
from jax import lax
from jax.experimental import pallas as pl
from jax.experimental.pallas import tpu as pltpu

D_MODEL = 1024
DEPTH = 2
ALPHA = (2.0 * DEPTH) ** 0.25
LN_EPS = 1e-5

GMLP_WIDTH = D_MODEL // 2
GMLP_GROUPS = 4
GMLP_GDIM = GMLP_WIDTH // GMLP_GROUPS
GMLP_CHUNK = 128
HGRN_WIDTH = D_MODEL // 2
HGRN_HEADS = 4
HGRN_DK = HGRN_WIDTH // HGRN_HEADS
HGRN_CHUNK = 64
EVEN_IN_WIDTH = 2 * GMLP_WIDTH + 4 * HGRN_WIDTH

MOBA_HEADS = 16
MOBA_HDIM = D_MODEL // MOBA_HEADS
MOBA_BLOCK = 256
MOBA_TOPK = 3
MOBA_PAIR = 2 * MOBA_HDIM
MOBA_BLOCKS_PER_CALL = 4

MEM_HEADS = 4
MEM_HDIM = D_MODEL // MEM_HEADS

D_FF = int(math.ceil(8 * D_MODEL / 3 / 256)) * 256
FFN_CHUNK = 256

ROW_TILE = 512
FFN_ROW_TILE = 1024
FFN_NORM_SLICES = 8
V7X_VMEM_LIMIT = 56 * 1024 * 1024

MASKED = -1e30

BF16 = jnp.bfloat16
F32 = jnp.float32


def _dot(a, b):
    return jnp.dot(a, b, preferred_element_type=F32)


def _dot_nt(a, b):
    return lax.dot_general(a, b, (((1,), (1,)), ((), ())), preferred_element_type=F32)


def _dot_tn(a, b):
    return lax.dot_general(a, b, (((0,), (0,)), ((), ())), preferred_element_type=F32)


def _layer_norm(z, g, b):
    mu = jnp.mean(z, axis=-1, keepdims=True)
    zc = z - mu
    var = jnp.mean(zc * zc, axis=-1, keepdims=True)
    return zc * lax.rsqrt(var + LN_EPS) * g + b


def _zero_after(token):
    bits = lax.bitcast_convert_type(token, jnp.int32)
    return lax.shift_right_logical(lax.shift_right_logical(bits, 16), 16).astype(F32)


def _residual_norm(x, y, g, b, after=None):
    alpha = ALPHA if after is None else ALPHA + _zero_after(after)
    return _layer_norm(x * alpha + y, g, b)


def _resident(picked):
    stack, index = picked
    return pl.BlockSpec((None,) + stack.shape[1:], lambda *_: (index,) + (0,) * (stack.ndim - 1),
                        pipeline_mode=pl.Buffered(1))


def _stacks(*picked):
    return [stack for stack, _ in picked]


def _bf16_copy(picked):
    stack, _ = picked
    return pltpu.VMEM(stack.shape[1:], BF16)


def _cast_weights_once(n_grid_axes, *pairs):
    first = pl.program_id(0) == 0
    for axis in range(1, n_grid_axes):
        first = jnp.logical_and(first, pl.program_id(axis) == 0)

    @pl.when(first)
    def _():
        for src, dst in pairs:
            dst[...] = src[...].astype(BF16)


def _params(*semantics):
    return pltpu.CompilerParams(dimension_semantics=semantics, vmem_limit_bytes=V7X_VMEM_LIMIT)


def _even_mixer_kernel(x_ref, w_in_f32_ref, w_out_f32_ref, ws_ref, bs_ref, aln_g_ref, aln_b_ref, bnorm_ref,
                       lb_logits_ref, ln_g_ref, ln_b_ref, o_ref,
                       q_s, f_s, i_s, g_s, y_s, state_s, w_in_ref, w_out_ref, *, lb_index):
    tm = x_ref.shape[1]
    _cast_weights_once(2, (w_in_f32_ref, w_in_ref), (w_out_f32_ref, w_out_ref))
    x = x_ref[0]
    xb = x.astype(BF16)

    @pl.when(pl.program_id(1) == 0)
    def _():
        state_s[...] = jnp.zeros_like(state_s)

    base = 2 * GMLP_WIDTH
    u_pre = _dot(xb, w_in_ref[:, 0:GMLP_WIDTH])
    v_pre = _dot(xb, w_in_ref[:, GMLP_WIDTH:2 * GMLP_WIDTH])
    q_pre = _dot(xb, w_in_ref[:, base:base + HGRN_WIDTH])
    q_s[...] = q_pre
    f_pre = _dot(xb, w_in_ref[:, base + HGRN_WIDTH:base + 2 * HGRN_WIDTH])
    f_s[...] = f_pre
    i_s[...] = jax.nn.silu(_dot(xb, w_in_ref[:, base + 2 * HGRN_WIDTH:base + 3 * HGRN_WIDTH]))
    g_s[...] = jax.nn.silu(_dot(xb, w_in_ref[:, base + 3 * HGRN_WIDTH:base + 4 * HGRN_WIDTH]))

    u = jax.nn.gelu(u_pre + _zero_after(q_pre[0:1, 0:1]))
    v = jax.nn.gelu(v_pre + _zero_after(f_pre[0:1, 0:1]))
    row = lax.broadcasted_iota(jnp.int32, (GMLP_CHUNK, GMLP_CHUNK), 0)
    col = lax.broadcasted_iota(jnp.int32, (GMLP_CHUNK, GMLP_CHUNK), 1)
    for g in range(GMLP_GROUPS):
        lanes = slice(g * GMLP_GDIM, (g + 1) * GMLP_GDIM)
        vn = _layer_norm(v[:, lanes], aln_g_ref[:, lanes], aln_b_ref[:, lanes]).astype(BF16)
        wg = jnp.where(col <= row, ws_ref[g], 0.0).astype(BF16)
        bias = bs_ref[:, g:g + 1]
        for c in range(tm // GMLP_CHUNK):
            rows = slice(c * GMLP_CHUNK, (c + 1) * GMLP_CHUNK)
            s = _dot(wg, vn[rows]) + bias
            y_s[rows, lanes] = (u[rows, lanes] * s).astype(BF16)

    n_chunks = tm // HGRN_CHUNK
    logits = lb_logits_ref[...]
    e = jnp.exp(logits - jnp.max(logits, axis=0, keepdims=True))
    lb = jnp.sum(e[0:lb_index + 1], axis=0, keepdims=True) / jnp.sum(e, axis=0, keepdims=True)

    crow =lax.broadcasted_iota(jnp.int32, (HGRN_CHUNK, HGRN_CHUNK), 0)
    ccol = lax.broadcasted_iota(jnp.int32, (HGRN_CHUNK, HGRN_CHUNK), 1)
    causal = ccol <= crow
    tril_ones = jnp.where(causal, 1.0, 0.0).astype(BF16)
    norm_g = bnorm_ref[...]

    for c in range(n_chunks):
        rows = slice(c * HGRN_CHUNK, (c + 1) * HGRN_CHUNK)
        f = lb + (1.0 - lb) * jax.nn.sigmoid(f_s[rows, :])
        log_f = jnp.log(f)
        log_f_hi = log_f.astype(BF16)
        log_f_lo = (log_f - log_f_hi.astype(F32)).astype(BF16)
        cum = _dot(tril_ones, log_f_hi) + _dot(tril_ones, log_f_lo)
        chunk_decay = jnp.exp(cum[HGRN_CHUNK - 1:HGRN_CHUNK, :])
        k_back = (1.0 - f) * jnp.exp(-cum)
        q_dec = (q_s[rows, :] * jnp.exp(cum)).astype(BF16)
        k_dec = k_back.astype(BF16)
        k_tail = (k_back * chunk_decay).astype(BF16)
        val = i_s[rows, :].astype(BF16)
        gate = g_s[rows, :]
        for h in range(HGRN_HEADS):
            lanes = slice(h * HGRN_DK, (h + 1) * HGRN_DK)
            attn = jnp.where(causal, _dot_nt(q_dec[:, lanes], k_dec[:, lanes]), 0.0).astype(BF16)
            state_t = state_s[h]
            o = _dot(attn, val[:, lanes]) + _dot_nt(q_dec[:, lanes], state_t.astype(BF16))
            state_s[h] = state_t * chunk_decay[:, lanes] + _dot_tn(val[:, lanes], k_tail[:, lanes])
            rms = lax.rsqrt(jnp.mean(o * o, axis=-1, keepdims=True) + LN_EPS)
            y_b = o * rms * norm_g[:, lanes] * gate[:, lanes]
            y_s[rows, GMLP_WIDTH + h * HGRN_DK:GMLP_WIDTH + (h + 1) * HGRN_DK] = y_b.astype(BF16)

    for rows in (slice(0, tm // 2), slice(tm // 2, tm)):
        y = _dot(y_s[rows, :], w_out_ref[...])
        o_ref[0, rows, :] = _residual_norm(x_ref[0, rows, :], y, ln_g_ref[...], ln_b_ref[...])


def _even_mixer(x, lb_index, *params):
    bsz, t_len, d = x.shape
    tm = ROW_TILE
    kern = functools.partial(_even_mixer_kernel, lb_index=lb_index)
    return pl.pallas_call(
        kern,
        grid=(bsz, t_len // tm),
        in_specs=[pl.BlockSpec((1, tm, d), lambda b, t: (b, t, 0))] + [_resident(p) for p in params],
        out_specs=pl.BlockSpec((1, tm, d), lambda b, t: (b, t, 0)),
        out_shape=jax.ShapeDtypeStruct(x.shape, F32),
        scratch_shapes=[
            pltpu.VMEM((tm, HGRN_WIDTH), F32), pltpu.VMEM((tm, HGRN_WIDTH), F32),
            pltpu.VMEM((tm, HGRN_WIDTH), F32), pltpu.VMEM((tm, HGRN_WIDTH), F32),
            pltpu.VMEM((tm, GMLP_WIDTH + HGRN_WIDTH), BF16),
            pltpu.VMEM((HGRN_HEADS, HGRN_DK, HGRN_DK), F32),
            _bf16_copy(params[0]), _bf16_copy(params[1]),
        ],
        compiler_params=_params("arbitrary", "arbitrary"),
        name="even_mixer",
    )(x, *_stacks(*params))


def _mem_kv_kernel(mem_ref, w_f32_ref, o_ref, w_ref):
    _cast_weights_once(1, (w_f32_ref, w_ref))
    o_ref[...] = _dot(mem_ref[...].astype(BF16), w_ref[...]).astype(BF16)


def _mem_kv(mem2d, w_kv):
    n, d = mem2d.shape
    tm = ROW_TILE
    width = w_kv[0].shape[-1]
    return pl.pallas_call(
        _mem_kv_kernel,
        grid=(n // tm,),
        in_specs=[pl.BlockSpec((tm, d), lambda i: (i, 0)), _resident(w_kv)],
        out_specs=pl.BlockSpec((tm, width), lambda i: (i, 0)),
        out_shape=jax.ShapeDtypeStruct((n, width), BF16),
        scratch_shapes=[_bf16_copy(w_kv)],
        compiler_params=_params("arbitrary"),
        name="mem_kv",
    )(mem2d, *_stacks(w_kv))


def _cross_attn_kernel(x_ref, kv_ref, wq_f32_ref, wo_f32_ref, ln_g_ref, ln_b_ref, o_ref, att_s, y_s,
                       wq_ref, wo_ref):
    _cast_weights_once(2, (wq_f32_ref, wq_ref), (wo_f32_ref, wo_ref))
    half = x_ref.shape[1] // 2
    slice_rows = half // MEM_HEADS

    def attend(rows):
        q = (_dot(x_ref[0, rows, :].astype(BF16), wq_ref[...]) * (MEM_HDIM ** -0.5)).astype(BF16)
        done = []
        for h in range(MEM_HEADS):
            lanes = slice(h * MEM_HDIM, (h + 1) * MEM_HDIM)
            k_h = kv_ref[0, :, h * MEM_HDIM:(h + 1) * MEM_HDIM]
            v_h = kv_ref[0, :, D_MODEL + h * MEM_HDIM:D_MODEL + (h + 1) * MEM_HDIM]
            s = _dot_nt(q[:, lanes], k_h)
            p = jnp.exp(s - jnp.max(s, axis=-1, keepdims=True))
            denom = jnp.sum(p, axis=-1, keepdims=True)
            weighted = _dot(p.astype(BF16), v_h)
            att_s[rows, lanes] = (weighted / denom).astype(BF16)
            done.append(weighted[0:1, 0:1])
        y_s[rows, :] = _dot(att_s[rows, :], wo_ref[...])
        return done

    def finish(rows, after=None):
        o_ref[0, rows, :] = _residual_norm(x_ref[0, rows, :], y_s[rows, :], ln_g_ref[...], ln_b_ref[...], after)

    attend(slice(0, half))
    done = attend(slice(half, 2 * half))
    for h in range(MEM_HEADS):
        finish(slice(h * slice_rows, (h + 1) * slice_rows), after=done[h])
    finish(slice(half, 2 * half))


def _cross_attn(x, kv, *params):
    bsz, t_len, d = x.shape
    tm = ROW_TILE
    return pl.pallas_call(
        _cross_attn_kernel,
        grid=(bsz, t_len // tm),
        in_specs=[
            pl.BlockSpec((1, tm, d), lambda b, t: (b, t, 0)),
            pl.BlockSpec((1,) + kv.shape[1:], lambda b, t: (b, 0, 0)),
        ] + [_resident(p) for p in params],
        out_specs=pl.BlockSpec((1, tm, d), lambda b, t: (b, t, 0)),
        out_shape=jax.ShapeDtypeStruct(x.shape, F32),
        scratch_shapes=[pltpu.VMEM((tm, d), BF16), pltpu.VMEM((tm, d), F32),
                        _bf16_copy(params[0]), _bf16_copy(params[1])],
        compiler_params=_params("arbitrary", "arbitrary"),
        name="cross_attn",
    )(x, kv, *_stacks(*params))


def _ffn_kernel(x_ref, w_in_ref, w_out_ref, ln_g_ref, ln_b_ref, o_ref, acc_s):
    half = x_ref.shape[0] // 2
    n_chunks = D_FF // FFN_CHUNK
    slice_rows = half // FFN_NORM_SLICES

    def chunk(rows, xb, c):
        cols = slice(c * FFN_CHUNK, (c + 1) * FFN_CHUNK)
        gate = _dot(xb, w_in_ref[:, cols])
        up = _dot(xb, w_in_ref[:, D_FF + c * FFN_CHUNK:D_FF + (c + 1) * FFN_CHUNK])
        act = (jax.nn.silu(gate) * up).astype(BF16)
        part = _dot(act, w_out_ref[cols, :])
        if c == 0:
            acc_s[rows, :] = part
        else:
            acc_s[rows, :] += part
        return part[0:1, :]

    def finish(rows, after=None):
        o_ref[rows, :] = _residual_norm(x_ref[rows, :], acc_s[rows, :], ln_g_ref[...], ln_b_ref[...], after)

    first, second = slice(0, half), slice(half, 2 * half)
    xb_first = x_ref[first, :].astype(BF16)
    xb_second = x_ref[second, :].astype(BF16)
    for c in range(n_chunks):
        chunk(first, xb_first, c)
    for c in range(n_chunks):
        done = chunk(second, xb_second, c)
        if c < FFN_NORM_SLICES:
            finish(slice(c * slice_rows, (c + 1) * slice_rows), after=done)
    finish(second)


def _ffn(x2d, *params):
    n, d = x2d.shape
    tm = FFN_ROW_TILE
    assert n % tm == 0
    return pl.pallas_call(
        _ffn_kernel,
        grid=(n // tm,),
        in_specs=[pl.BlockSpec((tm, d), lambda i: (i, 0))] + [_resident(p) for p in params],
        out_specs=pl.BlockSpec((tm, d), lambda i: (i, 0)),
        out_shape=jax.ShapeDtypeStruct((n, d), F32),
        scratch_shapes=[pltpu.VMEM((tm, d), F32)],
        compiler_params=_params("parallel"),
        name="ffn",
    )(x2d, *_stacks(*params))


LOG2_E = 1.4426950408889634
ALIBI_LANES = 3


def _moba_qkv_kernel(x_ref, w_f32_ref, fill_ref, q0_ref, q1_ref, k0_ref, k1_ref, vt_ref, kmean_ref, w_ref,
                     *, tiles_per_seq):
    _cast_weights_once(1, (w_f32_ref, w_ref))
    tm = x_ref.shape[0]
    xb = x_ref[...].astype(BF16)
    q = _dot(xb, w_ref[:, 0:D_MODEL]) * (MOBA_HDIM ** -0.5 * LOG2_E)
    k = _dot(xb, w_ref[:, D_MODEL:2 * D_MODEL])
    for i in range(tm // MOBA_BLOCK):
        kmean_ref[i] = jnp.mean(k[i * MOBA_BLOCK:(i + 1) * MOBA_BLOCK], axis=0, keepdims=True)
    vt_ref[0] = lax.dot_general(w_ref[:, 2 * D_MODEL:3 * D_MODEL], xb, (((0,), (1,)), ((), ())),
                                preferred_element_type=F32).astype(BF16)

    lane = lax.broadcasted_iota(jnp.int32, (tm, MOBA_PAIR), 1)
    pos = lax.broadcasted_iota(jnp.int32, (tm, MOBA_PAIR), 0) + (pl.program_id(0) % tiles_per_seq) * tm
    pos_block = ((pos // MOBA_BLOCK) * MOBA_BLOCK).astype(F32)
    pos_offset = (pos % MOBA_BLOCK).astype(F32)
    for e, (q_ref, k_ref) in enumerate(((q0_ref, k0_ref), (q1_ref, k1_ref))):
        own = (lane < MOBA_HDIM) if e == 0 else (lane >= MOBA_HDIM)
        partner = MOBA_HDIM * (1 - e)
        key_fill = jnp.where((lane >= partner) & (lane < partner + ALIBI_LANES), pos_block,
                             jnp.where((lane >= partner + ALIBI_LANES) & (lane < partner + 2 * ALIBI_LANES),
                                       pos_offset, 0.0))
        for p in range(MOBA_HEADS // 2):
            slab = slice(p * MOBA_PAIR, (p + 1) * MOBA_PAIR)
            q_ref[:, slab] = jnp.where(own, q[:, slab], fill_ref[e:e + 1, slab]).astype(BF16)
            k_ref[:, slab] = jnp.where(own, k[:, slab], key_fill).astype(BF16)


def _moba_qkv(x2d, w_qkv, fill, bsz, t_len):
    n, d = x2d.shape
    tm = ROW_TILE
    tiles_per_seq = t_len // tm
    rows = pl.BlockSpec((tm, d), lambda i: (i, 0))
    return pl.pallas_call(
        functools.partial(_moba_qkv_kernel, tiles_per_seq=tiles_per_seq),
        grid=(n // tm,),
        in_specs=[rows, _resident(w_qkv), _resident(fill)],
        out_specs=[
            rows, rows, rows, rows,
            pl.BlockSpec((1, d, tm), lambda i: (i // tiles_per_seq, 0, i % tiles_per_seq)),
            pl.BlockSpec((tm // MOBA_BLOCK, 1, d), lambda i: (i, 0, 0)),
        ],
        out_shape=[
            jax.ShapeDtypeStruct((n, d), BF16), jax.ShapeDtypeStruct((n, d), BF16),
            jax.ShapeDtypeStruct((n, d), BF16), jax.ShapeDtypeStruct((n, d), BF16),
            jax.ShapeDtypeStruct((bsz, d, t_len), BF16),
            jax.ShapeDtypeStruct((n // MOBA_BLOCK, 1, d), F32),
        ],
        scratch_shapes=[_bf16_copy(w_qkv)],
        compiler_params=_params("arbitrary"),
        name="moba_qkv",
    )(x2d, *_stacks(w_qkv, fill))


def _alibi_query_fill():
    slopes = jnp.asarray([2.0 ** (-8.0 * (h + 1) / MOBA_HEADS) for h in range(MOBA_HEADS)], F32) * LOG2_E
    pieces = []
    rest = slopes
    for _ in range(ALIBI_LANES):
        piece = rest.astype(BF16).astype(F32)
        pieces.append(piece)
        rest = rest - piece
    pieces = jnp.stack(pieces + pieces, axis=1)
    fill = jnp.zeros((2, MOBA_HEADS // 2, MOBA_PAIR), F32)
    for e in range(2):
        partner = MOBA_HDIM * (1 - e)
        fill = fill.at[e, :, partner:partner + 2 * ALIBI_LANES].set(pieces[e::2])
    return fill.reshape(2, D_MODEL)


def _moba_attn_kernel(*refs, first_block, n_query_blocks, fills_prior):
    if fills_prior:
        refs = refs[1:]
    (x_ref, q0_ref, q1_ref, k0_ref, k1_ref, vt_ref, kmean_ref, w_out_f32_ref, ln_g_ref, ln_b_ref,
     o_ref, sa_s, sb_s, sha_s, shb_s, pa_s, pb_s, att_s, w_out_ref) = refs
    _cast_weights_once(2, (w_out_f32_ref, w_out_ref))
    qb = pl.program_id(1) + first_block
    n_blocks = kmean_ref.shape[1]
    blk = MOBA_BLOCK
    q_refs = (q0_ref, q1_ref)
    k_refs = (k0_ref, k1_ref)
    key_i = lax.broadcasted_iota(jnp.int32, (blk, blk), 0)
    qry_i = lax.broadcasted_iota(jnp.int32, (blk, blk), 1)
    causal = key_i <= qry_i
    blk_i = lax.broadcasted_iota(jnp.int32, (n_blocks, blk), 0)
    mean_lane = lax.broadcasted_iota(jnp.int32, (n_blocks, MOBA_PAIR), 1)

    def attend(n_past):
        n_keys = (n_past + 1) * blk
        select = n_past > MOBA_TOPK

        def pair_lanes(hp):
            return pl.ds(pl.multiple_of(hp * MOBA_PAIR, MOBA_PAIR), MOBA_PAIR)

        def scores_and_shifts(hp, s_buf, sh_buf):
            lanes = pair_lanes(hp)
            for e in range(2):
                cols = slice(e * blk, (e + 1) * blk)
                q_e = q_refs[e][0, :, lanes]
                if select:
                    own = (mean_lane < MOBA_HDIM) if e == 0 else (mean_lane >= MOBA_HDIM)
                    kmean = jnp.where(own, kmean_ref[0, :, lanes], 0.0)
                    kmean_hi = kmean.astype(BF16)
                    kmean_lo = (kmean - kmean_hi.astype(F32)).astype(BF16)
                    aff = _dot_nt(kmean_hi, q_e) + _dot_nt(kmean_lo, q_e)
                    rank = jnp.zeros((n_blocks, blk), F32)
                    for jp in range(n_past):
                        other = aff[jp:jp + 1, :]
                        beats = (other > aff) | ((other == aff) & (jp < blk_i))
                        rank = rank + jnp.where(beats, 1.0, 0.0)
                    chosen = rank < MOBA_TOPK
                m = None
                for j in range(n_past + 1):
                    rows = slice(j * blk, (j + 1) * blk)
                    t = _dot_nt(k_refs[e][0, rows, lanes], q_e)
                    if j == n_past:
                        t = jnp.where(causal, t, MASKED)
                    s_buf[rows, cols] = t
                    m_j = jnp.max(t, axis=0, keepdims=True)
                    if select and j < n_past:
                        m_j = jnp.where(chosen[j:j + 1, :], m_j, MASKED)
                    m = m_j if m is None else jnp.maximum(m, m_j)
                shifts = jnp.broadcast_to(m, (n_blocks, blk))
                if select:
                    shifts = jnp.where(chosen | (blk_i == n_past), shifts, -MASKED)
                sh_buf[e] = shifts

        def probabilities(s_buf, sh_buf, p_buf, after=None):
            hold = 0.0 if after is None else _zero_after(after)
            for e in range(2):
                cols = slice(e * blk, (e + 1) * blk)
                for j in range(n_past + 1):
                    rows = slice(j * blk, (j + 1) * blk)
                    shift = sh_buf[e, j:j + 1, :] + hold
                    p_buf[e, rows, :] = jnp.exp2(s_buf[rows, cols] - shift).astype(BF16)

        def weighted_values(hp, p_buf):
            for e in range(2):
                feat = pl.ds(pl.multiple_of(hp * MOBA_PAIR + e * MOBA_HDIM, MOBA_HDIM), MOBA_HDIM)
                values = jnp.concatenate([vt_ref[0, feat, 0:n_keys], jnp.ones((16, n_keys), BF16)], axis=0)
                acc = _dot(values, p_buf[e, 0:n_keys, :])
                denom = acc[MOBA_HDIM:MOBA_HDIM + 1]
                att_s[feat, :] = acc[0:MOBA_HDIM] / denom
            return denom

        n_pairs = MOBA_HEADS // 2
        scores_and_shifts(0, sa_s, sha_s)
        scores_and_shifts(1, sb_s, shb_s)
        probabilities(sa_s, sha_s, pa_s)

        def two_pairs(i, carry):
            hp = 2 * i + 1
            scores_and_shifts(hp + 1, sa_s, sha_s)
            probabilities(sb_s, shb_s, pb_s)
            consumed = weighted_values(hp - 1, pa_s)
            scores_and_shifts(hp + 2, sb_s, shb_s)
            probabilities(sa_s, sha_s, pa_s, after=consumed)
            weighted_values(hp, pb_s)
            return carry

        lax.fori_loop(0, n_pairs // 2 - 1, two_pairs, 0)
        probabilities(sb_s, shb_s, pb_s)
        weighted_values(n_pairs - 2, pa_s)
        weighted_values(n_pairs - 1, pb_s)

    for n_past in range(first_block, first_block + n_query_blocks):
        pl.when(qb == n_past)(functools.partial(attend, n_past))

    att = att_s[...].T.astype(BF16)
    y = _dot(att, w_out_ref[...])
    o_ref[0] = _layer_norm(ALPHA * x_ref[0] + y, ln_g_ref[...], ln_b_ref[...])


def _moba_attn(prior, first_block, n_query_blocks, x, q0, q1, k0, k1, vt, kmean, *params):
    bsz, t_len, d = x.shape
    blk = MOBA_BLOCK
    n_blocks = t_len // blk
    n_keys = (first_block + n_query_blocks) * blk
    query_rows = pl.BlockSpec((1, blk, d), lambda b, t: (b, t + first_block, 0))
    visible_keys = pl.BlockSpec((1, n_keys, d), lambda b, t: (b, 0, 0))
    fills_prior = prior is not None
    return pl.pallas_call(
        functools.partial(_moba_attn_kernel, first_block=first_block, n_query_blocks=n_query_blocks,
                          fills_prior=fills_prior),
        grid=(bsz, n_query_blocks),
        in_specs=([pl.BlockSpec(memory_space=pl.ANY)] if fills_prior else []) + [
            query_rows, query_rows, query_rows, visible_keys, visible_keys,
            pl.BlockSpec((1, d, n_keys), lambda b, t: (b, 0, 0)),
            pl.BlockSpec((1, n_blocks, d), lambda b, t: (b, 0, 0)),
        ] + [_resident(p) for p in params],
        out_specs=query_rows,
        out_shape=jax.ShapeDtypeStruct(x.shape, F32),
        input_output_aliases={0: 0} if fills_prior else {},
        scratch_shapes=[
            pltpu.VMEM((n_keys, 2 * blk), F32),
            pltpu.VMEM((n_keys, 2 * blk), F32),
            pltpu.VMEM((2, n_blocks, blk), F32),
            pltpu.VMEM((2, n_blocks, blk), F32),
            pltpu.VMEM((2, n_keys, blk), BF16),
            pltpu.VMEM((2, n_keys, blk), BF16),
            pltpu.VMEM((d, blk), F32),
            _bf16_copy(params[0]),
        ],
        compiler_params=_params("arbitrary", "arbitrary"),
        name=f"moba_attn_from_block_{first_block}",
    )(*([prior] if fills_prior else []), x, q0, q1, k0, k1, vt, kmean, *_stacks(*params))


def kernel(x, mem, ln_g, ln_b, x_wq, x_wkv, x_wo, ffn_w_in, ffn_w_out, ev_w_in, ev_w_out, a_ws, a_bs,
           a_ln_g, a_ln_b, b_norm_g, hgrn_lb_logits, od_w_qkv, od_w_out):
    bsz, t_len, d = x.shape
    assert d == D_MODEL and t_len % ROW_TILE == 0 and t_len % MOBA_BLOCK == 0
    assert ROW_TILE % MOBA_BLOCK == 0 and t_len // MOBA_BLOCK > 1
    n = bsz * t_len
    mem2d = mem.reshape(bsz * mem.shape[1], d)

    def rows_of(v):
        return v.reshape(-1, 1, v.shape[-1])

    def seq(v):
        return v.reshape(bsz, t_len, d)

    ffn_w_in, ffn_w_out = ffn_w_in.astype(BF16), ffn_w_out.astype(BF16)
    ln_g, ln_b = rows_of(ln_g), rows_of(ln_b)
    a_ln_g, a_ln_b, b_norm_g = rows_of(a_ln_g), rows_of(a_ln_b), rows_of(b_norm_g)
    a_bs_t = jnp.swapaxes(a_bs, 1, 2)
    lb_logits = hgrn_lb_logits[None]
    alibi_fill = _alibi_query_fill()[None]

    for layer in range(DEPTH):
        j = layer // 2
        norm = [(ln_g, 3 * layer), (ln_b, 3 * layer)]
        if layer % 2 == 0:
            x = _even_mixer(
                x, j, (ev_w_in, j), (ev_w_out, j), (a_ws, j), (a_bs_t, j), (a_ln_g, j), (a_ln_b, j),
                (b_norm_g, j), (lb_logits, 0), *norm)
        else:
            q0, q1, k0, k1, vt, kmean = _moba_qkv(x.reshape(n, d), (od_w_qkv, j), (alibi_fill, 0), bsz, t_len)
            attn_args = (x, seq(q0), seq(q1), seq(k0), seq(k1), vt,
                         kmean.reshape(bsz, t_len // MOBA_BLOCK, d), (od_w_out, j), *norm)
            n_blocks = t_len // MOBA_BLOCK
            filled = None
            for first_block in range(0, n_blocks, MOBA_BLOCKS_PER_CALL):
                filled = _moba_attn(filled, first_block, min(MOBA_BLOCKS_PER_CALL, n_blocks - first_block),
                                    *attn_args)
            x = filled
        kv = _mem_kv(mem2d, (x_wkv, layer)).reshape(bsz, mem.shape[1], 2 * d)
        x = _cross_attn(x, kv, (x_wq, layer), (x_wo, layer), (ln_g, 3 * layer + 1), (ln_b, 3 * layer + 1))
        x = _ffn(x.reshape(n, d), (ffn_w_in, layer), (ffn_w_out, layer),
                 (ln_g, 3 * layer + 2), (ln_b, 3 * layer + 2)).reshape(bsz, t_len, d)
    return x
```

```python
import functools
import math

import jax
import jax.numpy as jnp
from jax import lax
from jax.experimental import pallas as pl
from jax.experimental.pallas import tpu as pltpu

D_MODEL = 1024
DEPTH = 2
ALPHA = (2.0 * DEPTH) ** 0.25
LN_EPS = 1e-5

GMLP_WIDTH = D_MODEL // 2
GMLP_GROUPS = 4
GMLP_GDIM = GMLP_WIDTH // GMLP_GROUPS
GMLP_CHUNK = 128
HGRN_WIDTH = D_MODEL // 2
HGRN_HEADS = 4
HGRN_DK = HGRN_WIDTH // HGRN_HEADS
HGRN_CHUNK = 64
EVEN_IN_WIDTH = 2 * GMLP_WIDTH + 4 * HGRN_WIDTH

MOBA_HEADS = 16
MOBA_HDIM = D_MODEL // MOBA_HEADS
MOBA_BLOCK = 256
MOBA_TOPK = 3
MOBA_PAIR = 2 * MOBA_HDIM
MOBA_BLOCKS_PER_CALL = 2

MEM_HEADS = 4
MEM_HDIM = D_MODEL // MEM_HEADS

D_FF = int(math.ceil(8 * D_MODEL / 3 / 256)) * 256
FFN_CHUNK = 256

ROW_TILE = 512
FFN_ROW_TILE = 1024
FFN_NORM_SLICES = 8
V7X_VMEM_LIMIT = 56 * 1024 * 1024

MASKED = -1e30

BF16 = jnp.bfloat16
F32 = jnp.float32


def _dot(a, b):
    return jnp.dot(a, b, preferred_element_type=F32)


def _dot_nt(a, b):
    return lax.dot_general(a, b, (((1,), (1,)), ((), ())), preferred_element_type=F32)


def _dot_tn(a, b):
    return lax.dot_general(a, b, (((0,), (0,)), ((), ())), preferred_element_type=F32)


def _layer_norm(z, g, b):
    mu = jnp.mean(z, axis=-1, keepdims=True)
    zc = z - mu
    var = jnp.mean(zc * zc, axis=-1, keepdims=True)
    return zc * lax.rsqrt(var + LN_EPS) * g + b


def _zero_after(token):
    bits = lax.bitcast_convert_type(token, jnp.int32)
    return lax.shift_right_logical(lax.shift_right_logical(bits, 16), 16).astype(F32)


def _residual_norm(x, y, g, b, after=None):
    alpha = ALPHA if after is None else ALPHA + _zero_after(after)
    return _layer_norm(x * alpha + y, g, b)


def _resident(picked):
    stack, index = picked
    return pl.BlockSpec((None,) + stack.shape[1:], lambda *_: (index,) + (0,) * (stack.ndim - 1),
                        pipeline_mode=pl.Buffered(1))


def _stacks(*picked):
    return [stack for stack, _ in picked]


def _bf16_copy(picked):
    stack, _ = picked
    return pltpu.VMEM(stack.shape[1:], BF16)


def _cast_weights_once(n_grid_axes, *pairs):
    first = pl.program_id(0) == 0
    for axis in range(1, n_grid_axes):
        first = jnp.logical_and(first, pl.program_id(axis) == 0)

    @pl.when(first)
    def _():
        for src, dst in pairs:
            dst[...] = src[...].astype(BF16)


def _params(*semantics):
    return pltpu.CompilerParams(dimension_semantics=semantics, vmem_limit_bytes=V7X_VMEM_LIMIT)


def _even_mixer_kernel(x_ref, w_in_f32_ref, w_out_f32_ref, ws_ref, bs_ref, aln_g_ref, aln_b_ref, bnorm_ref,
                       lb_logits_ref, ln_g_ref, ln_b_ref, o_ref,
                       q_s, f_s, i_s, g_s, y_s, state_s, w_in_ref, w_out_ref, *, lb_index):
    tm = x_ref.shape[1]
    _cast_weights_once(2, (w_in_f32_ref, w_in_ref), (w_out_f32_ref, w_out_ref))
    x = x_ref[0]
    xb = x.astype(BF16)

    @pl.when(pl.program_id(1) == 0)
    def _():
        state_s[...] = jnp.zeros_like(state_s)

    base = 2 * GMLP_WIDTH
    u_pre = _dot(xb, w_in_ref[:, 0:GMLP_WIDTH])
    v_pre = _dot(xb, w_in_ref[:, GMLP_WIDTH:2 * GMLP_WIDTH])
    q_pre = _dot(xb, w_in_ref[:, base:base + HGRN_WIDTH])
    q_s[...] = q_pre
    f_pre = _dot(xb, w_in_ref[:, base + HGRN_WIDTH:base + 2 * HGRN_WIDTH])
    f_s[...] = f_pre
    i_s[...] = jax.nn.silu(_dot(xb, w_in_ref[:, base + 2 * HGRN_WIDTH:base + 3 * HGRN_WIDTH]))
    g_s[...] = jax.nn.silu(_dot(xb, w_in_ref[:, base + 3 * HGRN_WIDTH:base + 4 * HGRN_WIDTH]))

    u = jax.nn.gelu(u_pre + _zero_after(q_pre[0:1, 0:1]))
    v = jax.nn.gelu(v_pre + _zero_after(f_pre[0:1, 0:1]))
    row = lax.broadcasted_iota(jnp.int32, (GMLP_CHUNK, GMLP_CHUNK), 0)
    col = lax.broadcasted_iota(jnp.int32, (GMLP_CHUNK, GMLP_CHUNK), 1)
    for g in range(GMLP_GROUPS):
        lanes = slice(g * GMLP_GDIM, (g + 1) * GMLP_GDIM)
        vn = _layer_norm(v[:, lanes], aln_g_ref[:, lanes], aln_b_ref[:, lanes]).astype(BF16)
        wg = jnp.where(col <= row, ws_ref[g], 0.0).astype(BF16)
        bias = bs_ref[:, g:g + 1]
        for c in range(tm // GMLP_CHUNK):
            rows = slice(c * GMLP_CHUNK, (c + 1) * GMLP_CHUNK)
            s = _dot(wg, vn[rows]) + bias
            y_s[rows, lanes] = (u[rows, lanes] * s).astype(BF16)

    n_chunks = tm // HGRN_CHUNK
    logits = lb_logits_ref[...]
    e = jnp.exp(logits - jnp.max(logits, axis=0, keepdims=True))
    lb = jnp.sum(e[0:lb_index + 1], axis=0, keepdims=True) / jnp.sum(e, axis=0, keepdims=True)

    crow =lax.broadcasted_iota(jnp.int32, (HGRN_CHUNK, HGRN_CHUNK), 0)
    ccol = lax.broadcasted_iota(jnp.int32, (HGRN_CHUNK, HGRN_CHUNK), 1)
    causal = ccol <= crow
    tril_ones = jnp.where(causal, 1.0, 0.0).astype(BF16)
    norm_g = bnorm_ref[...]

    for c in range(n_chunks):
        rows = slice(c * HGRN_CHUNK, (c + 1) * HGRN_CHUNK)
        f = lb + (1.0 - lb) * jax.nn.sigmoid(f_s[rows, :])
        log_f = jnp.log(f)
        log_f_hi = log_f.astype(BF16)
        log_f_lo = (log_f - log_f_hi.astype(F32)).astype(BF16)
        cum = _dot(tril_ones, log_f_hi) + _dot(tril_ones, log_f_lo)
        chunk_decay = jnp.exp(cum[HGRN_CHUNK - 1:HGRN_CHUNK, :])
        k_back = (1.0 - f) * jnp.exp(-cum)
        q_dec = (q_s[rows, :] * jnp.exp(cum)).astype(BF16)
        k_dec = k_back.astype(BF16)
        k_tail = (k_back * chunk_decay).astype(BF16)
        val = i_s[rows, :].astype(BF16)
        gate = g_s[rows, :]
        for h in range(HGRN_HEADS):
            lanes = slice(h * HGRN_DK, (h + 1) * HGRN_DK)
            attn = jnp.where(causal, _dot_nt(q_dec[:, lanes], k_dec[:, lanes]), 0.0).astype(BF16)
            state_t = state_s[h]
            o = _dot(attn, val[:, lanes]) + _dot_nt(q_dec[:, lanes], state_t.astype(BF16))
            state_s[h] = state_t * chunk_decay[:, lanes] + _dot_tn(val[:, lanes], k_tail[:, lanes])
            rms = lax.rsqrt(jnp.mean(o * o, axis=-1, keepdims=True) + LN_EPS)
            y_b = o * rms * norm_g[:, lanes] * gate[:, lanes]
            y_s[rows, GMLP_WIDTH + h * HGRN_DK:GMLP_WIDTH + (h + 1) * HGRN_DK] = y_b.astype(BF16)

    for rows in (slice(0, tm // 2), slice(tm // 2, tm)):
        y = _dot(y_s[rows, :], w_out_ref[...])
        o_ref[0, rows, :] = _residual_norm(x_ref[0, rows, :], y, ln_g_ref[...], ln_b_ref[...])


def _even_mixer(x, lb_index, *params):
    bsz, t_len, d = x.shape
    tm = ROW_TILE
    kern = functools.partial(_even_mixer_kernel, lb_index=lb_index)
    return pl.pallas_call(
        kern,
        grid=(bsz, t_len // tm),
        in_specs=[pl.BlockSpec((1, tm, d), lambda b, t: (b, t, 0))] + [_resident(p) for p in params],
        out_specs=pl.BlockSpec((1, tm, d), lambda b, t: (b, t, 0)),
        out_shape=jax.ShapeDtypeStruct(x.shape, F32),
        scratch_shapes=[
            pltpu.VMEM((tm, HGRN_WIDTH), F32), pltpu.VMEM((tm, HGRN_WIDTH), F32),
            pltpu.VMEM((tm, HGRN_WIDTH), F32), pltpu.VMEM((tm, HGRN_WIDTH), F32),
            pltpu.VMEM((tm, GMLP_WIDTH + HGRN_WIDTH), BF16),
            pltpu.VMEM((HGRN_HEADS, HGRN_DK, HGRN_DK), F32),
            _bf16_copy(params[0]), _bf16_copy(params[1]),
        ],
        compiler_params=_params("arbitrary", "arbitrary"),
        name="even_mixer",
    )(x, *_stacks(*params))


def _mem_kv_kernel(mem_ref, w_f32_ref, o_ref, w_ref):
    _cast_weights_once(1, (w_f32_ref, w_ref))
    o_ref[...] = _dot(mem_ref[...].astype(BF16), w_ref[...]).astype(BF16)


def _mem_kv(mem2d, w_kv):
    n, d = mem2d.shape
    tm = ROW_TILE
    width = w_kv[0].shape[-1]
    return pl.pallas_call(
        _mem_kv_kernel,
        grid=(n // tm,),
        in_specs=[pl.BlockSpec((tm, d), lambda i: (i, 0)), _resident(w_kv)],
        out_specs=pl.BlockSpec((tm, width), lambda i: (i, 0)),
        out_shape=jax.ShapeDtypeStruct((n, width), BF16),
        scratch_shapes=[_bf16_copy(w_kv)],
        compiler_params=_params("arbitrary"),
        name="mem_kv",
    )(mem2d, *_stacks(w_kv))


def _cross_attn_kernel(x_ref, kv_ref, wq_f32_ref, wo_f32_ref, ln_g_ref, ln_b_ref, o_ref, att_s, y_s,
                       wq_ref, wo_ref):
    _cast_weights_once(2, (wq_f32_ref, wq_ref), (wo_f32_ref, wo_ref))
    half = x_ref.shape[1] // 2
    slice_rows = half // MEM_HEADS

    def attend(rows):
        q = (_dot(x_ref[0, rows, :].astype(BF16), wq_ref[...]) * (MEM_HDIM ** -0.5)).astype(BF16)
        done = []
        for h in range(MEM_HEADS):
            lanes = slice(h * MEM_HDIM, (h + 1) * MEM_HDIM)
            k_h = kv_ref[0, :, h * MEM_HDIM:(h + 1) * MEM_HDIM]
            v_h = kv_ref[0, :, D_MODEL + h * MEM_HDIM:D_MODEL + (h + 1) * MEM_HDIM]
            s = _dot_nt(q[:, lanes], k_h)
            p = jnp.exp(s - jnp.max(s, axis=-1, keepdims=True))
            denom = jnp.sum(p, axis=-1, keepdims=True)
            weighted = _dot(p.astype(BF16), v_h)
            att_s[rows, lanes] = (weighted / denom).astype(BF16)
            done.append(weighted[0:1, 0:1])
        y_s[rows, :] = _dot(att_s[rows, :], wo_ref[...])
        return done

    def finish(rows, after=None):
        o_ref[0, rows, :] = _residual_norm(x_ref[0, rows, :], y_s[rows, :], ln_g_ref[...], ln_b_ref[...], after)

    attend(slice(0, half))
    done = attend(slice(half, 2 * half))
    for h in range(MEM_HEADS):
        finish(slice(h * slice_rows, (h + 1) * slice_rows), after=done[h])
    finish(slice(half, 2 * half))


def _cross_attn(x, kv, *params):
    bsz, t_len, d = x.shape
    tm = ROW_TILE
    return pl.pallas_call(
        _cross_attn_kernel,
        grid=(bsz, t_len // tm),
        in_specs=[
            pl.BlockSpec((1, tm, d), lambda b, t: (b, t, 0)),
            pl.BlockSpec((1,) + kv.shape[1:], lambda b, t: (b, 0, 0)),
        ] + [_resident(p) for p in params],
        out_specs=pl.BlockSpec((1, tm, d), lambda b, t: (b, t, 0)),
        out_shape=jax.ShapeDtypeStruct(x.shape, F32),
        scratch_shapes=[pltpu.VMEM((tm, d), BF16), pltpu.VMEM((tm, d), F32),
                        _bf16_copy(params[0]), _bf16_copy(params[1])],
        compiler_params=_params("arbitrary", "arbitrary"),
        name="cross_attn",
    )(x, kv, *_stacks(*params))


def _ffn_kernel(x_ref, w_in_ref, w_out_ref, ln_g_ref, ln_b_ref, o_ref, acc_s):
    half = x_ref.shape[0] // 2
    n_chunks = D_FF // FFN_CHUNK
    slice_rows = half // FFN_NORM_SLICES

    def chunk(rows, xb, c):
        cols = slice(c * FFN_CHUNK, (c + 1) * FFN_CHUNK)
        gate = _dot(xb, w_in_ref[:, cols])
        up = _dot(xb, w_in_ref[:, D_FF + c * FFN_CHUNK:D_FF + (c + 1) * FFN_CHUNK])
        act = (jax.nn.silu(gate) * up).astype(BF16)
        part = _dot(act, w_out_ref[cols, :])
        if c == 0:
            acc_s[rows, :] = part
        else:
            acc_s[rows, :] += part
        return part[0:1, :]

    def finish(rows, after=None):
        o_ref[rows, :] = _residual_norm(x_ref[rows, :], acc_s[rows, :], ln_g_ref[...], ln_b_ref[...], after)

    first, second = slice(0, half), slice(half, 2 * half)
    xb_first = x_ref[first, :].astype(BF16)
    xb_second = x_ref[second, :].astype(BF16)
    for c in range(n_chunks):
        chunk(first, xb_first, c)
    for c in range(n_chunks):
        done = chunk(second, xb_second, c)
        if c < FFN_NORM_SLICES:
            finish(slice(c * slice_rows, (c + 1) * slice_rows), after=done)
    finish(second)


def _ffn(x2d, *params):
    n, d = x2d.shape
    tm = FFN_ROW_TILE
    assert n % tm == 0
    return pl.pallas_call(
        _ffn_kernel,
        grid=(n // tm,),
        in_specs=[pl.BlockSpec((tm, d), lambda i: (i, 0))] + [_resident(p) for p in params],
        out_specs=pl.BlockSpec((tm, d), lambda i: (i, 0)),
        out_shape=jax.ShapeDtypeStruct((n, d), F32),
        scratch_shapes=[pltpu.VMEM((tm, d), F32)],
        compiler_params=_params("parallel"),
        name="ffn",
    )(x2d, *_stacks(*params))


LOG2_E = 1.4426950408889634
ALIBI_LANES = 3


def _moba_qkv_kernel(x_ref, w_f32_ref, fill_ref, q0_ref, q1_ref, k0_ref, k1_ref, vt_ref, kmean_ref, w_ref,
                     *, tiles_per_seq):
    _cast_weights_once(1, (w_f32_ref, w_ref))
    tm = x_ref.shape[0]
    xb = x_ref[...].astype(BF16)
    q = _dot(xb, w_ref[:, 0:D_MODEL]) * (MOBA_HDIM ** -0.5 * LOG2_E)
    k = _dot(xb, w_ref[:, D_MODEL:2 * D_MODEL])
    for i in range(tm // MOBA_BLOCK):
        kmean_ref[i] = jnp.mean(k[i * MOBA_BLOCK:(i + 1) * MOBA_BLOCK], axis=0, keepdims=True)
    vt_ref[0] = lax.dot_general(w_ref[:, 2 * D_MODEL:3 * D_MODEL], xb, (((0,), (1,)), ((), ())),
                                preferred_element_type=F32).astype(BF16)

    lane = lax.broadcasted_iota(jnp.int32, (tm, MOBA_PAIR), 1)
    pos = lax.broadcasted_iota(jnp.int32, (tm, MOBA_PAIR), 0) + (pl.program_id(0) % tiles_per_seq) * tm
    pos_block = ((pos // MOBA_BLOCK) * MOBA_BLOCK).astype(F32)
    pos_offset = (pos % MOBA_BLOCK).astype(F32)
    for e, (q_ref, k_ref) in enumerate(((q0_ref, k0_ref), (q1_ref, k1_ref))):
        own = (lane < MOBA_HDIM) if e == 0 else (lane >= MOBA_HDIM)
        partner = MOBA_HDIM * (1 - e)
        key_fill = jnp.where((lane >= partner) & (lane < partner + ALIBI_LANES), pos_block,
                             jnp.where((lane >= partner + ALIBI_LANES) & (lane < partner + 2 * ALIBI_LANES),
                                       pos_offset, 0.0))
        for p in range(MOBA_HEADS // 2):
            slab = slice(p * MOBA_PAIR, (p + 1) * MOBA_PAIR)
            q_ref[:, slab] = jnp.where(own, q[:, slab], fill_ref[e:e + 1, slab]).astype(BF16)
            k_ref[:, slab] = jnp.where(own, k[:, slab], key_fill).astype(BF16)


def _moba_qkv(x2d, w_qkv, fill, bsz, t_len):
    n, d = x2d.shape
    tm = ROW_TILE
    tiles_per_seq = t_len // tm
    rows = pl.BlockSpec((tm, d), lambda i: (i, 0))
    return pl.pallas_call(
        functools.partial(_moba_qkv_kernel, tiles_per_seq=tiles_per_seq),
        grid=(n // tm,),
        in_specs=[rows, _resident(w_qkv), _resident(fill)],
        out_specs=[
            rows, rows, rows, rows,
            pl.BlockSpec((1, d, tm), lambda i: (i // tiles_per_seq, 0, i % tiles_per_seq)),
            pl.BlockSpec((tm // MOBA_BLOCK, 1, d), lambda i: (i, 0, 0)),
        ],
        out_shape=[
            jax.ShapeDtypeStruct((n, d), BF16), jax.ShapeDtypeStruct((n, d), BF16),
            jax.ShapeDtypeStruct((n, d), BF16), jax.ShapeDtypeStruct((n, d), BF16),
            jax.ShapeDtypeStruct((bsz, d, t_len), BF16),
            jax.ShapeDtypeStruct((n // MOBA_BLOCK, 1, d), F32),
        ],
        scratch_shapes=[_bf16_copy(w_qkv)],
        compiler_params=_params("arbitrary"),
        name="moba_qkv",
    )(x2d, *_stacks(w_qkv, fill))


def _alibi_query_fill():
    slopes = jnp.asarray([2.0 ** (-8.0 * (h + 1) / MOBA_HEADS) for h in range(MOBA_HEADS)], F32) * LOG2_E
    pieces = []
    rest = slopes
    for _ in range(ALIBI_LANES):
        piece = rest.astype(BF16).astype(F32)
        pieces.append(piece)
        rest = rest - piece
    pieces = jnp.stack(pieces + pieces, axis=1)
    fill = jnp.zeros((2, MOBA_HEADS // 2, MOBA_PAIR), F32)
    for e in range(2):
        partner = MOBA_HDIM * (1 - e)
        fill = fill.at[e, :, partner:partner + 2 * ALIBI_LANES].set(pieces[e::2])
    return fill.reshape(2, D_MODEL)


def _moba_attn_kernel(*refs, first_block, n_query_blocks, fills_prior):
    if fills_prior:
        refs = refs[1:]
    (x_ref, q0_ref, q1_ref, k0_ref, k1_ref, vt_ref, kmean_ref, w_out_ref, ln_g_ref, ln_b_ref,
     o_ref, sa_s, sb_s, sha_s, shb_s, pa_s, pb_s, att_s) = refs
    n_blocks = kmean_ref.shape[1]
    blk = MOBA_BLOCK
    q_refs = (q0_ref, q1_ref)
    k_refs = (k0_ref, k1_ref)
    key_i = lax.broadcasted_iota(jnp.int32, (blk, blk), 0)
    qry_i = lax.broadcasted_iota(jnp.int32, (blk, blk), 1)
    causal = key_i <= qry_i
    blk_i = lax.broadcasted_iota(jnp.int32, (n_blocks, blk), 0)
    mean_lane = lax.broadcasted_iota(jnp.int32, (n_blocks, MOBA_PAIR), 1)

    def attend(n_past, query_rows, att_buf, buffers_read):
        n_keys = (n_past + 1) * blk
        select = n_past > MOBA_TOPK

        def pair_lanes(hp):
            return pl.ds(pl.multiple_of(hp * MOBA_PAIR, MOBA_PAIR), MOBA_PAIR)

        def scores_and_shifts(hp, s_buf, sh_buf, after=None):
            lanes = pair_lanes(hp)
            for e in range(2):
                cols = slice(e * blk, (e + 1) * blk)
                q_e = q_refs[e][0, query_rows, lanes]
                if after is not None:
                    q_e = (q_e.astype(F32) + _zero_after(after)).astype(BF16)
                if select:
                    own = (mean_lane < MOBA_HDIM) if e == 0 else (mean_lane >= MOBA_HDIM)
                    kmean = jnp.where(own, kmean_ref[0, :, lanes], 0.0)
                    kmean_hi = kmean.astype(BF16)
                    kmean_lo = (kmean - kmean_hi.astype(F32)).astype(BF16)
                    aff = _dot_nt(kmean_hi, q_e) + _dot_nt(kmean_lo, q_e)
                    rank = jnp.zeros((n_blocks, blk), F32)
                    for jp in range(n_past):
                        other = aff[jp:jp + 1, :]
                        beats = (other > aff) | ((other == aff) & (jp < blk_i))
                        rank = rank + jnp.where(beats, 1.0, 0.0)
                    chosen = rank < MOBA_TOPK
                m = None
                for j in range(n_past + 1):
                    rows = slice(j * blk, (j + 1) * blk)
                    t = _dot_nt(k_refs[e][0, rows, lanes], q_e)
                    if j == n_past:
                        t = jnp.where(causal, t, MASKED)
                    s_buf[rows, cols] = t
                    m_j = jnp.max(t, axis=0, keepdims=True)
                    if select and j < n_past:
                        m_j = jnp.where(chosen[j:j + 1, :], m_j, MASKED)
                    m = m_j if m is None else jnp.maximum(m, m_j)
                shifts = jnp.broadcast_to(m, (n_blocks, blk))
                if select:
                    shifts = jnp.where(chosen | (blk_i == n_past), shifts, -MASKED)
                sh_buf[e] = shifts

        def probabilities(s_buf, sh_buf, p_buf, after=None):
            hold = 0.0 if after is None else _zero_after(after)
            for e in range(2):
                cols = slice(e * blk, (e + 1) * blk)
                for j in range(n_past + 1):
                    rows = slice(j * blk, (j + 1) * blk)
                    shift = sh_buf[e, j:j + 1, :] + hold
                    p_buf[e, rows, :] = jnp.exp2(s_buf[rows, cols] - shift).astype(BF16)

        def weighted_values(hp, p_buf):
            for e in range(2):
                feat = pl.ds(pl.multiple_of(hp * MOBA_PAIR + e * MOBA_HDIM, MOBA_HDIM), MOBA_HDIM)
                values = jnp.concatenate([vt_ref[0, feat, 0:n_keys], jnp.ones((16, n_keys), BF16)], axis=0)
                acc = _dot(values, p_buf[e, 0:n_keys, :])
                denom = acc[MOBA_HDIM:MOBA_HDIM + 1]
                att_buf[feat, :] = acc[0:MOBA_HDIM] / denom
            return denom[:, 0:1]

        n_pairs = MOBA_HEADS // 2
        read_a, read_b = buffers_read
        scores_and_shifts(0, sa_s, sha_s)
        scores_and_shifts(1, sb_s, shb_s, after=read_b)
        probabilities(sa_s, sha_s, pa_s, after=read_a)

        def two_pairs(i, carry):
            hp = 2 * i + 1
            scores_and_shifts(hp + 1, sa_s, sha_s)
            probabilities(sb_s, shb_s, pb_s)
            consumed = weighted_values(hp - 1, pa_s)
            scores_and_shifts(hp + 2, sb_s, shb_s)
            probabilities(sa_s, sha_s, pa_s, after=consumed)
            weighted_values(hp, pb_s)
            return carry

        lax.fori_loop(0, n_pairs // 2 - 1, two_pairs, 0)
        probabilities(sb_s, shb_s, pb_s)
        return weighted_values(n_pairs - 2, pa_s), weighted_values(n_pairs - 1, pb_s)

    buffers_read = (None, None)
    for j in range(n_query_blocks):
        query_rows = slice(j * blk, (j + 1) * blk)
        buffers_read = attend(first_block + j, query_rows, att_s.at[j], buffers_read)
        att = att_s[j].T.astype(BF16)
        y = _dot(att, w_out_ref[...])
        o_ref[0, query_rows, :] = _residual_norm(x_ref[0, query_rows, :], y, ln_g_ref[...], ln_b_ref[...])


def _moba_attn(prior, first_block, n_query_blocks, x, q0, q1, k0, k1, vt, kmean, *params):
    bsz, t_len, d = x.shape
    blk = MOBA_BLOCK
    n_blocks = t_len // blk
    n_keys = (first_block + n_query_blocks) * blk
    assert first_block % n_query_blocks == 0
    query_rows = pl.BlockSpec((1, n_query_blocks * blk, d), lambda b: (b, first_block // n_query_blocks, 0))
    visible_keys = pl.BlockSpec((1, n_keys, d), lambda b: (b, 0, 0))
    fills_prior = prior is not None
    return pl.pallas_call(
        functools.partial(_moba_attn_kernel, first_block=first_block, n_query_blocks=n_query_blocks,
                          fills_prior=fills_prior),
        grid=(bsz,),
        in_specs=([pl.BlockSpec(memory_space=pl.ANY)] if fills_prior else []) + [
            query_rows, query_rows, query_rows, visible_keys, visible_keys,
            pl.BlockSpec((1, d, n_keys), lambda b: (b, 0, 0)),
            pl.BlockSpec((1, n_blocks, d), lambda b: (b, 0, 0)),
        ] + [_resident(p) for p in params],
        out_specs=query_rows,
        out_shape=jax.ShapeDtypeStruct(x.shape, F32),
        input_output_aliases={0: 0} if fills_prior else {},
        scratch_shapes=[
            pltpu.VMEM((n_keys, 2 * blk), F32),
            pltpu.VMEM((n_keys, 2 * blk), F32),
            pltpu.VMEM((2, n_blocks, blk), F32),
            pltpu.VMEM((2, n_blocks, blk), F32),
            pltpu.VMEM((2, n_keys, blk), BF16),
            pltpu.VMEM((2, n_keys, blk), BF16),
            pltpu.VMEM((n_query_blocks, d, blk), F32),
        ],
        compiler_params=_params("arbitrary"),
        name=f"moba_attn_from_block_{first_block}",
    )(*([prior] if fills_prior else []), x, q0, q1, k0, k1, vt, kmean, *_stacks(*params))


def kernel(x, mem, ln_g, ln_b, x_wq, x_wkv, x_wo, ffn_w_in, ffn_w_out, ev_w_in, ev_w_out, a_ws, a_bs,
           a_ln_g, a_ln_b, b_norm_g, hgrn_lb_logits, od_w_qkv, od_w_out):
    bsz, t_len, d = x.shape
    assert d == D_MODEL and t_len % ROW_TILE == 0 and t_len % MOBA_BLOCK == 0
    assert ROW_TILE % MOBA_BLOCK == 0 and t_len // MOBA_BLOCK > 1
    n = bsz * t_len
    mem2d = mem.reshape(bsz * mem.shape[1], d)

    def rows_of(v):
        return v.reshape(-1, 1, v.shape[-1])

    def seq(v):
        return v.reshape(bsz, t_len, d)

    ffn_w_in, ffn_w_out = ffn_w_in.astype(BF16), ffn_w_out.astype(BF16)
    od_w_out = od_w_out.astype(BF16)
    ln_g, ln_b = rows_of(ln_g), rows_of(ln_b)
    a_ln_g, a_ln_b, b_norm_g = rows_of(a_ln_g), rows_of(a_ln_b), rows_of(b_norm_g)
    a_bs_t = jnp.swapaxes(a_bs, 1, 2)
    lb_logits = hgrn_lb_logits[None]
    alibi_fill = _alibi_query_fill()[None]

    for layer in range(DEPTH):
        j = layer // 2
        norm = [(ln_g, 3 * layer), (ln_b, 3 * layer)]
        if layer % 2 == 0:
            x = _even_mixer(
                x, j, (ev_w_in, j), (ev_w_out, j), (a_ws, j), (a_bs_t, j), (a_ln_g, j), (a_ln_b, j),
                (b_norm_g, j), (lb_logits, 0), *norm)
        else:
            q0, q1, k0, k1, vt, kmean = _moba_qkv(x.reshape(n, d), (od_w_qkv, j), (alibi_fill, 0), bsz, t_len)
            attn_args = (x, seq(q0), seq(q1), seq(k0), seq(k1), vt,
                         kmean.reshape(bsz, t_len // MOBA_BLOCK, d), (od_w_out, j), *norm)
            n_blocks = t_len // MOBA_BLOCK
            filled = None
            for first_block in range(0, n_blocks, MOBA_BLOCKS_PER_CALL):
                filled = _moba_attn(filled, first_block, min(MOBA_BLOCKS_PER_CALL, n_blocks - first_block),
                                    *attn_args)
            x = filled
        kv = _mem_kv(mem2d, (x_wkv, layer)).reshape(bsz, mem.shape[1], 2 * d)
        x = _cross_attn(x, kv, (x_wq, layer), (x_wo, layer), (ln_g, 3 * layer + 1), (ln_b, 3 * layer + 1))
        x = _ffn(x.reshape(n, d), (ffn_w_in, layer), (ffn_w_out, layer),
                 (ln_g, 3 * layer + 2), (ln_b, 3 * layer + 2)).reshape(bsz, t_len, d)
    return x
```

```python
import functools
import math

import jax
import jax.numpy as jnp
from jax import lax
from jax.experimental import pallas as pl
from jax.experimental.pallas import tpu as pltpu

D_MODEL = 1024
DEPTH = 2
ALPHA = (2.0 * DEPTH) ** 0.25
LN_EPS = 1e-5

GMLP_WIDTH = D_MODEL // 2
GMLP_GROUPS = 4
GMLP_GDIM = GMLP_WIDTH // GMLP_GROUPS
GMLP_CHUNK = 128
HGRN_WIDTH = D_MODEL // 2
HGRN_HEADS = 4
HGRN_DK = HGRN_WIDTH // HGRN_HEADS
HGRN_CHUNK = 64
EVEN_IN_WIDTH = 2 * GMLP_WIDTH + 4 * HGRN_WIDTH

MOBA_HEADS = 16
MOBA_HDIM = D_MODEL // MOBA_HEADS
MOBA_BLOCK = 256
MOBA_TOPK = 3
MOBA_PAIR = 2 * MOBA_HDIM
MOBA_BLOCKS_PER_CALL = 2

MEM_HEADS = 4
MEM_HDIM = D_MODEL // MEM_HEADS

D_FF = int(math.ceil(8 * D_MODEL / 3 / 256)) * 256
FFN_CHUNK = 256

ROW_TILE = 512
WIDE_ROW_TILE = 1024
FFN_ROW_TILE = WIDE_ROW_TILE
FFN_NORM_SLICES = 8
V7X_VMEM_LIMIT = 56 * 1024 * 1024

MASKED = -1e30

BF16 = jnp.bfloat16
F32 = jnp.float32


def _dot(a, b):
    return jnp.dot(a, b, preferred_element_type=F32)


def _dot_nt(a, b):
    return lax.dot_general(a, b, (((1,), (1,)), ((), ())), preferred_element_type=F32)


def _dot_tn(a, b):
    return lax.dot_general(a, b, (((0,), (0,)), ((), ())), preferred_element_type=F32)


def _layer_norm(z, g, b):
    mu = jnp.mean(z, axis=-1, keepdims=True)
    zc = z - mu
    var = jnp.mean(zc * zc, axis=-1, keepdims=True)
    return zc * lax.rsqrt(var + LN_EPS) * g + b


def _zero_after(token):
    bits = lax.bitcast_convert_type(token, jnp.int32)
    return lax.shift_right_logical(lax.shift_right_logical(bits, 16), 16).astype(F32)


def _residual_norm(x, y, g, b, after=None):
    alpha = ALPHA if after is None else ALPHA + _zero_after(after)
    return _layer_norm(x * alpha + y, g, b)


def _resident(picked):
    stack, index = picked
    return pl.BlockSpec((None,) + stack.shape[1:], lambda *_: (index,) + (0,) * (stack.ndim - 1),
                        pipeline_mode=pl.Buffered(1))


def _stacks(*picked):
    return [stack for stack, _ in picked]


def _bf16_copy(picked):
    stack, _ = picked
    return pltpu.VMEM(stack.shape[1:], BF16)


def _cast_weights_once(n_grid_axes, *pairs):
    first = pl.program_id(0) == 0
    for axis in range(1, n_grid_axes):
        first = jnp.logical_and(first, pl.program_id(axis) == 0)

    @pl.when(first)
    def _():
        for src, dst in pairs:
            dst[...] = src[...].astype(BF16)


def _params(*semantics):
    return pltpu.CompilerParams(dimension_semantics=semantics, vmem_limit_bytes=V7X_VMEM_LIMIT)


def _even_mixer_kernel(x_ref, w_in_f32_ref, w_out_f32_ref, ws_ref, bs_ref, aln_g_ref, aln_b_ref, bnorm_ref,
                       lb_logits_ref, ln_g_ref, ln_b_ref, o_ref,
                       q_s, f_s, i_s, g_s, y_s, state_s, w_in_ref, w_out_ref, *, lb_index):
    tm = x_ref.shape[1]
    _cast_weights_once(2, (w_in_f32_ref, w_in_ref), (w_out_f32_ref, w_out_ref))
    x = x_ref[0]
    xb = x.astype(BF16)

    @pl.when(pl.program_id(1) == 0)
    def _():
        state_s[...] = jnp.zeros_like(state_s)

    base = 2 * GMLP_WIDTH
    u_pre = _dot(xb, w_in_ref[:, 0:GMLP_WIDTH])
    v_pre = _dot(xb, w_in_ref[:, GMLP_WIDTH:2 * GMLP_WIDTH])
    q_pre = _dot(xb, w_in_ref[:, base:base + HGRN_WIDTH])
    q_s[...] = q_pre
    f_pre = _dot(xb, w_in_ref[:, base + HGRN_WIDTH:base + 2 * HGRN_WIDTH])
    f_s[...] = f_pre
    i_s[...] = jax.nn.silu(_dot(xb, w_in_ref[:, base + 2 * HGRN_WIDTH:base + 3 * HGRN_WIDTH]))
    g_s[...] = jax.nn.silu(_dot(xb, w_in_ref[:, base + 3 * HGRN_WIDTH:base + 4 * HGRN_WIDTH]))

    u = jax.nn.gelu(u_pre + _zero_after(q_pre[0:1, 0:1]))
    v = jax.nn.gelu(v_pre + _zero_after(f_pre[0:1, 0:1]))
    row = lax.broadcasted_iota(jnp.int32, (GMLP_CHUNK, GMLP_CHUNK), 0)
    col = lax.broadcasted_iota(jnp.int32, (GMLP_CHUNK, GMLP_CHUNK), 1)
    for g in range(GMLP_GROUPS):
        lanes = slice(g * GMLP_GDIM, (g + 1) * GMLP_GDIM)
        vn = _layer_norm(v[:, lanes], aln_g_ref[:, lanes], aln_b_ref[:, lanes]).astype(BF16)
        wg = jnp.where(col <= row, ws_ref[g], 0.0).astype(BF16)
        bias = bs_ref[:, g:g + 1]
        for c in range(tm // GMLP_CHUNK):
            rows = slice(c * GMLP_CHUNK, (c + 1) * GMLP_CHUNK)
            s = _dot(wg, vn[rows]) + bias
            y_s[rows, lanes] = (u[rows, lanes] * s).astype(BF16)

    n_chunks = tm // HGRN_CHUNK
    logits = lb_logits_ref[...]
    e = jnp.exp(logits - jnp.max(logits, axis=0, keepdims=True))
    lb = jnp.sum(e[0:lb_index + 1], axis=0, keepdims=True) / jnp.sum(e, axis=0, keepdims=True)

    crow =lax.broadcasted_iota(jnp.int32, (HGRN_CHUNK, HGRN_CHUNK), 0)
    ccol = lax.broadcasted_iota(jnp.int32, (HGRN_CHUNK, HGRN_CHUNK), 1)
    causal = ccol <= crow
    tril_ones = jnp.where(causal, 1.0, 0.0).astype(BF16)
    norm_g = bnorm_ref[...]

    for c in range(n_chunks):
        rows = slice(c * HGRN_CHUNK, (c + 1) * HGRN_CHUNK)
        f = lb + (1.0 - lb) * jax.nn.sigmoid(f_s[rows, :])
        log_f = jnp.log(f)
        log_f_hi = log_f.astype(BF16)
        log_f_lo = (log_f - log_f_hi.astype(F32)).astype(BF16)
        cum = _dot(tril_ones, log_f_hi) + _dot(tril_ones, log_f_lo)
        chunk_decay = jnp.exp(cum[HGRN_CHUNK - 1:HGRN_CHUNK, :])
        k_back = (1.0 - f) * jnp.exp(-cum)
        q_dec = (q_s[rows, :] * jnp.exp(cum)).astype(BF16)
        k_dec = k_back.astype(BF16)
        k_tail = (k_back * chunk_decay).astype(BF16)
        val = i_s[rows, :].astype(BF16)
        gate = g_s[rows, :]
        for h in range(HGRN_HEADS):
            lanes = slice(h * HGRN_DK, (h + 1) * HGRN_DK)
            attn = jnp.where(causal, _dot_nt(q_dec[:, lanes], k_dec[:, lanes]), 0.0).astype(BF16)
            state_t = state_s[h]
            o = _dot(attn, val[:, lanes]) + _dot_nt(q_dec[:, lanes], state_t.astype(BF16))
            state_s[h] = state_t * chunk_decay[:, lanes] + _dot_tn(val[:, lanes], k_tail[:, lanes])
            rms = lax.rsqrt(jnp.mean(o * o, axis=-1, keepdims=True) + LN_EPS)
            y_b = o * rms * norm_g[:, lanes] * gate[:, lanes]
            y_s[rows, GMLP_WIDTH + h * HGRN_DK:GMLP_WIDTH + (h + 1) * HGRN_DK] = y_b.astype(BF16)

    for rows in (slice(0, tm // 2), slice(tm // 2, tm)):
        y = _dot(y_s[rows, :], w_out_ref[...])
        o_ref[0, rows, :] = _residual_norm(x_ref[0, rows, :], y, ln_g_ref[...], ln_b_ref[...])


def _even_mixer(x, lb_index, *params):
    bsz, t_len, d = x.shape
    tm = ROW_TILE
    kern = functools.partial(_even_mixer_kernel, lb_index=lb_index)
    return pl.pallas_call(
        kern,
        grid=(bsz, t_len // tm),
        in_specs=[pl.BlockSpec((1, tm, d), lambda b, t: (b, t, 0))] + [_resident(p) for p in params],
        out_specs=pl.BlockSpec((1, tm, d), lambda b, t: (b, t, 0)),
        out_shape=jax.ShapeDtypeStruct(x.shape, F32),
        scratch_shapes=[
            pltpu.VMEM((tm, HGRN_WIDTH), F32), pltpu.VMEM((tm, HGRN_WIDTH), F32),
            pltpu.VMEM((tm, HGRN_WIDTH), F32), pltpu.VMEM((tm, HGRN_WIDTH), F32),
            pltpu.VMEM((tm, GMLP_WIDTH + HGRN_WIDTH), BF16),
            pltpu.VMEM((HGRN_HEADS, HGRN_DK, HGRN_DK), F32),
            _bf16_copy(params[0]), _bf16_copy(params[1]),
        ],
        compiler_params=_params("arbitrary", "arbitrary"),
        name="even_mixer",
    )(x, *_stacks(*params))


def _mem_kv_kernel(mem_ref, w_f32_ref, o_ref, w_ref):
    _cast_weights_once(1, (w_f32_ref, w_ref))
    o_ref[...] = _dot(mem_ref[...].astype(BF16), w_ref[...]).astype(BF16)


def _mem_kv(mem2d, w_kv):
    n, d = mem2d.shape
    tm = ROW_TILE
    width = w_kv[0].shape[-1]
    return pl.pallas_call(
        _mem_kv_kernel,
        grid=(n // tm,),
        in_specs=[pl.BlockSpec((tm, d), lambda i: (i, 0)), _resident(w_kv)],
        out_specs=pl.BlockSpec((tm, width), lambda i: (i, 0)),
        out_shape=jax.ShapeDtypeStruct((n, width), BF16),
        scratch_shapes=[_bf16_copy(w_kv)],
        compiler_params=_params("arbitrary"),
        name="mem_kv",
    )(mem2d, *_stacks(w_kv))


def _cross_attn_kernel(x_ref, kv_ref, wq_f32_ref, wo_f32_ref, ln_g_ref, ln_b_ref, o_ref, att_s, y_s,
                       wq_ref, wo_ref):
    _cast_weights_once(2, (wq_f32_ref, wq_ref), (wo_f32_ref, wo_ref))
    half = x_ref.shape[1] // 2
    slice_rows = half // MEM_HEADS

    def attend(rows):
        q = (_dot(x_ref[0, rows, :].astype(BF16), wq_ref[...]) * (MEM_HDIM ** -0.5)).astype(BF16)
        done = []
        for h in range(MEM_HEADS):
            lanes = slice(h * MEM_HDIM, (h + 1) * MEM_HDIM)
            k_h = kv_ref[0, :, h * MEM_HDIM:(h + 1) * MEM_HDIM]
            v_h = kv_ref[0, :, D_MODEL + h * MEM_HDIM:D_MODEL + (h + 1) * MEM_HDIM]
            s = _dot_nt(q[:, lanes], k_h)
            p = jnp.exp(s - jnp.max(s, axis=-1, keepdims=True))
            denom = jnp.sum(p, axis=-1, keepdims=True)
            weighted = _dot(p.astype(BF16), v_h)
            att_s[rows, lanes] = (weighted / denom).astype(BF16)
            done.append(weighted[0:1, 0:1])
        y_s[rows, :] = _dot(att_s[rows, :], wo_ref[...])
        return done

    def finish(rows, after=None):
        o_ref[0, rows, :] = _residual_norm(x_ref[0, rows, :], y_s[rows, :], ln_g_ref[...], ln_b_ref[...], after)

    attend(slice(0, half))
    done = attend(slice(half, 2 * half))
    for h in range(MEM_HEADS):
        finish(slice(h * slice_rows, (h + 1) * slice_rows), after=done[h])
    finish(slice(half, 2 * half))


def _cross_attn(x, kv, *params):
    bsz, t_len, d = x.shape
    tm = WIDE_ROW_TILE
    return pl.pallas_call(
        _cross_attn_kernel,
        grid=(bsz, t_len // tm),
        in_specs=[
            pl.BlockSpec((1, tm, d), lambda b, t: (b, t, 0)),
            pl.BlockSpec((1,) + kv.shape[1:], lambda b, t: (b, 0, 0)),
        ] + [_resident(p) for p in params],
        out_specs=pl.BlockSpec((1, tm, d), lambda b, t: (b, t, 0)),
        out_shape=jax.ShapeDtypeStruct(x.shape, F32),
        scratch_shapes=[pltpu.VMEM((tm, d), BF16), pltpu.VMEM((tm, d), F32),
                        _bf16_copy(params[0]), _bf16_copy(params[1])],
        compiler_params=_params("arbitrary", "arbitrary"),
        name="cross_attn",
    )(x, kv, *_stacks(*params))


def _ffn_kernel(x_ref, w_in_ref, w_out_ref, ln_g_ref, ln_b_ref, o_ref, acc_s):
    half = x_ref.shape[0] // 2
    n_chunks = D_FF // FFN_CHUNK
    slice_rows = half // FFN_NORM_SLICES

    def chunk(rows, xb, c):
        cols = slice(c * FFN_CHUNK, (c + 1) * FFN_CHUNK)
        gate = _dot(xb, w_in_ref[:, cols])
        up = _dot(xb, w_in_ref[:, D_FF + c * FFN_CHUNK:D_FF + (c + 1) * FFN_CHUNK])
        act = (jax.nn.silu(gate) * up).astype(BF16)
        part = _dot(act, w_out_ref[cols, :])
        if c == 0:
            acc_s[rows, :] = part
        else:
            acc_s[rows, :] += part
        return part[0:1, :]

    def finish(rows, after=None):
        o_ref[rows, :] = _residual_norm(x_ref[rows, :], acc_s[rows, :], ln_g_ref[...], ln_b_ref[...], after)

    first, second = slice(0, half), slice(half, 2 * half)
    xb_first = x_ref[first, :].astype(BF16)
    xb_second = x_ref[second, :].astype(BF16)
    for c in range(n_chunks):
        chunk(first, xb_first, c)
    for c in range(n_chunks):
        done = chunk(second, xb_second, c)
        if c < FFN_NORM_SLICES:
            finish(slice(c * slice_rows, (c + 1) * slice_rows), after=done)
    finish(second)


def _ffn(x2d, *params):
    n, d = x2d.shape
    tm = FFN_ROW_TILE
    assert n % tm == 0
    return pl.pallas_call(
        _ffn_kernel,
        grid=(n // tm,),
        in_specs=[pl.BlockSpec((tm, d), lambda i: (i, 0))] + [_resident(p) for p in params],
        out_specs=pl.BlockSpec((tm, d), lambda i: (i, 0)),
        out_shape=jax.ShapeDtypeStruct((n, d), F32),
        scratch_shapes=[pltpu.VMEM((tm, d), F32)],
        compiler_params=_params("parallel"),
        name="ffn",
    )(x2d, *_stacks(*params))


LOG2_E = 1.4426950408889634
ALIBI_LANES = 3


def _moba_qkv_kernel(x_ref, w_f32_ref, fill_ref, q0_ref, q1_ref, k0_ref, k1_ref, vt_ref, kmean_ref, w_ref,
                     *, tiles_per_seq):
    _cast_weights_once(1, (w_f32_ref, w_ref))
    tm = x_ref.shape[0]
    xb = x_ref[...].astype(BF16)
    q = _dot(xb, w_ref[:, 0:D_MODEL]) * (MOBA_HDIM ** -0.5 * LOG2_E)
    k = _dot(xb, w_ref[:, D_MODEL:2 * D_MODEL])
    for i in range(tm // MOBA_BLOCK):
        kmean_ref[i] = jnp.mean(k[i * MOBA_BLOCK:(i + 1) * MOBA_BLOCK], axis=0, keepdims=True)
    vt_ref[0] = lax.dot_general(w_ref[:, 2 * D_MODEL:3 * D_MODEL], xb, (((0,), (1,)), ((), ())),
                                preferred_element_type=F32).astype(BF16)

    lane = lax.broadcasted_iota(jnp.int32, (tm, MOBA_PAIR), 1)
    pos = lax.broadcasted_iota(jnp.int32, (tm, MOBA_PAIR), 0) + (pl.program_id(0) % tiles_per_seq) * tm
    pos_block = ((pos // MOBA_BLOCK) * MOBA_BLOCK).astype(F32)
    pos_offset = (pos % MOBA_BLOCK).astype(F32)
    for e, (q_ref, k_ref) in enumerate(((q0_ref, k0_ref), (q1_ref, k1_ref))):
        own = (lane < MOBA_HDIM) if e == 0 else (lane >= MOBA_HDIM)
        partner = MOBA_HDIM * (1 - e)
        key_fill = jnp.where((lane >= partner) & (lane < partner + ALIBI_LANES), pos_block,
                             jnp.where((lane >= partner + ALIBI_LANES) & (lane < partner + 2 * ALIBI_LANES),
                                       pos_offset, 0.0))
        for p in range(MOBA_HEADS // 2):
            slab = slice(p * MOBA_PAIR, (p + 1) * MOBA_PAIR)
            q_ref[:, slab] = jnp.where(own, q[:, slab], fill_ref[e:e + 1, slab]).astype(BF16)
            k_ref[:, slab] = jnp.where(own, k[:, slab], key_fill).astype(BF16)


def _moba_qkv(x2d, w_qkv, fill, bsz, t_len):
    n, d = x2d.shape
    tm = WIDE_ROW_TILE
    tiles_per_seq = t_len // tm
    rows = pl.BlockSpec((tm, d), lambda i: (i, 0))
    return pl.pallas_call(
        functools.partial(_moba_qkv_kernel, tiles_per_seq=tiles_per_seq),
        grid=(n // tm,),
        in_specs=[rows, _resident(w_qkv), _resident(fill)],
        out_specs=[
            rows, rows, rows, rows,
            pl.BlockSpec((1, d, tm), lambda i: (i // tiles_per_seq, 0, i % tiles_per_seq)),
            pl.BlockSpec((tm // MOBA_BLOCK, 1, d), lambda i: (i, 0, 0)),
        ],
        out_shape=[
            jax.ShapeDtypeStruct((n, d), BF16), jax.ShapeDtypeStruct((n, d), BF16),
            jax.ShapeDtypeStruct((n, d), BF16), jax.ShapeDtypeStruct((n, d), BF16),
            jax.ShapeDtypeStruct((bsz, d, t_len), BF16),
            jax.ShapeDtypeStruct((n // MOBA_BLOCK, 1, d), F32),
        ],
        scratch_shapes=[_bf16_copy(w_qkv)],
        compiler_params=_params("arbitrary"),
        name="moba_qkv",
    )(x2d, *_stacks(w_qkv, fill))


def _alibi_query_fill():
    slopes = jnp.asarray([2.0 ** (-8.0 * (h + 1) / MOBA_HEADS) for h in range(MOBA_HEADS)], F32) * LOG2_E
    pieces = []
    rest = slopes
    for _ in range(ALIBI_LANES):
        piece = rest.astype(BF16).astype(F32)
        pieces.append(piece)
        rest = rest - piece
    pieces = jnp.stack(pieces + pieces, axis=1)
    fill = jnp.zeros((2, MOBA_HEADS // 2, MOBA_PAIR), F32)
    for e in range(2):
        partner = MOBA_HDIM * (1 - e)
        fill = fill.at[e, :, partner:partner + 2 * ALIBI_LANES].set(pieces[e::2])
    return fill.reshape(2, D_MODEL)


def _moba_attn_kernel(*refs, first_block, n_query_blocks, fills_prior):
    if fills_prior:
        refs = refs[1:]
    (x_ref, q0_ref, q1_ref, k0_ref, k1_ref, vt_ref, kmean_ref, w_out_ref, ln_g_ref, ln_b_ref,
     o_ref, sa_s, sb_s, sha_s, shb_s, pa_s, pb_s, att_s) = refs
    n_blocks = kmean_ref.shape[1]
    blk = MOBA_BLOCK
    q_refs = (q0_ref, q1_ref)
    k_refs = (k0_ref, k1_ref)
    key_i = lax.broadcasted_iota(jnp.int32, (blk, blk), 0)
    qry_i = lax.broadcasted_iota(jnp.int32, (blk, blk), 1)
    causal = key_i <= qry_i
    blk_i = lax.broadcasted_iota(jnp.int32, (n_blocks, blk), 0)
    mean_lane = lax.broadcasted_iota(jnp.int32, (n_blocks, MOBA_PAIR), 1)

    def attend(n_past, query_rows, att_buf, buffers_read):
        n_keys = (n_past + 1) * blk
        select = n_past > MOBA_TOPK

        def pair_lanes(hp):
            return pl.ds(pl.multiple_of(hp * MOBA_PAIR, MOBA_PAIR), MOBA_PAIR)

        def scores_and_shifts(hp, s_buf, sh_buf, after=None):
            lanes = pair_lanes(hp)
            for e in range(2):
                cols = slice(e * blk, (e + 1) * blk)
                q_e = q_refs[e][0, query_rows, lanes]
                if after is not None:
                    q_e = (q_e.astype(F32) + _zero_after(after)).astype(BF16)
                if select:
                    own = (mean_lane < MOBA_HDIM) if e == 0 else (mean_lane >= MOBA_HDIM)
                    kmean = jnp.where(own, kmean_ref[0, :, lanes], 0.0)
                    kmean_hi = kmean.astype(BF16)
                    kmean_lo = (kmean - kmean_hi.astype(F32)).astype(BF16)
                    aff = _dot_nt(kmean_hi, q_e) + _dot_nt(kmean_lo, q_e)
                    rank = jnp.zeros((n_blocks, blk), F32)
                    for jp in range(n_past):
                        other = aff[jp:jp + 1, :]
                        beats = (other > aff) | ((other == aff) & (jp < blk_i))
                        rank = rank + jnp.where(beats, 1.0, 0.0)
                    chosen = rank < MOBA_TOPK
                m = None
                for j in range(n_past + 1):
                    rows = slice(j * blk, (j + 1) * blk)
                    t = _dot_nt(k_refs[e][0, rows, lanes], q_e)
                    if j == n_past:
                        t = jnp.where(causal, t, MASKED)
                    s_buf[rows, cols] = t
                    m_j = jnp.max(t, axis=0, keepdims=True)
                    if select and j < n_past:
                        m_j = jnp.where(chosen[j:j + 1, :], m_j, MASKED)
                    m = m_j if m is None else jnp.maximum(m, m_j)
                shifts = jnp.broadcast_to(m, (n_blocks, blk))
                if select:
                    shifts = jnp.where(chosen | (blk_i == n_past), shifts, -MASKED)
                sh_buf[e] = shifts

        def probabilities(s_buf, sh_buf, p_buf, after=None):
            hold = 0.0 if after is None else _zero_after(after)
            for e in range(2):
                cols = slice(e * blk, (e + 1) * blk)
                for j in range(n_past + 1):
                    rows = slice(j * blk, (j + 1) * blk)
                    shift = sh_buf[e, j:j + 1, :] + hold
                    p_buf[e, rows, :] = jnp.exp2(s_buf[rows, cols] - shift).astype(BF16)

        def weighted_values(hp, p_buf):
            for e in range(2):
                feat = pl.ds(pl.multiple_of(hp * MOBA_PAIR + e * MOBA_HDIM, MOBA_HDIM), MOBA_HDIM)
                values = jnp.concatenate([vt_ref[0, feat, 0:n_keys], jnp.ones((16, n_keys), BF16)], axis=0)
                acc = _dot(values, p_buf[e, 0:n_keys, :])
                denom = acc[MOBA_HDIM:MOBA_HDIM + 1]
                att_buf[feat, :] = acc[0:MOBA_HDIM] / denom
            return denom[:, 0:1]

        n_pairs = MOBA_HEADS // 2
        read_a, read_b = buffers_read
        scores_and_shifts(0, sa_s, sha_s)
        scores_and_shifts(1, sb_s, shb_s, after=read_b)
        probabilities(sa_s, sha_s, pa_s, after=read_a)

        def two_pairs(i, carry):
            hp = 2 * i + 1
            scores_and_shifts(hp + 1, sa_s, sha_s)
            probabilities(sb_s, shb_s, pb_s)
            consumed = weighted_values(hp - 1, pa_s)
            scores_and_shifts(hp + 2, sb_s, shb_s)
            probabilities(sa_s, sha_s, pa_s, after=consumed)
            weighted_values(hp, pb_s)
            return carry

        lax.fori_loop(0, n_pairs // 2 - 1, two_pairs, 0)
        probabilities(sb_s, shb_s, pb_s)
        return weighted_values(n_pairs - 2, pa_s), weighted_values(n_pairs - 1, pb_s)

    buffers_read = (None, None)
    for j in range(n_query_blocks):
        query_rows = slice(j * blk, (j + 1) * blk)
        buffers_read = attend(first_block + j, query_rows, att_s.at[j], buffers_read)
        att = att_s[j].T.astype(BF16)
        y = _dot(att, w_out_ref[...])
        o_ref[0, query_rows, :] = _residual_norm(x_ref[0, query_rows, :], y, ln_g_ref[...], ln_b_ref[...])


def _moba_attn(prior, first_block, n_query_blocks, x, q0, q1, k0, k1, vt, kmean, *params):
    bsz, t_len, d = x.shape
    blk = MOBA_BLOCK
    n_blocks = t_len // blk
    n_keys = (first_block + n_query_blocks) * blk
    assert first_block % n_query_blocks == 0
    query_rows = pl.BlockSpec((1, n_query_blocks * blk, d), lambda b: (b, first_block // n_query_blocks, 0))
    visible_keys = pl.BlockSpec((1, n_keys, d), lambda b: (b, 0, 0))
    fills_prior = prior is not None
    return pl.pallas_call(
        functools.partial(_moba_attn_kernel, first_block=first_block, n_query_blocks=n_query_blocks,
                          fills_prior=fills_prior),
        grid=(bsz,),
        in_specs=([pl.BlockSpec(memory_space=pl.ANY)] if fills_prior else []) + [
            query_rows, query_rows, query_rows, visible_keys, visible_keys,
            pl.BlockSpec((1, d, n_keys), lambda b: (b, 0, 0)),
            pl.BlockSpec((1, n_blocks, d), lambda b: (b, 0, 0)),
        ] + [_resident(p) for p in params],
        out_specs=query_rows,
        out_shape=jax.ShapeDtypeStruct(x.shape, F32),
        input_output_aliases={0: 0} if fills_prior else {},
        scratch_shapes=[
            pltpu.VMEM((n_keys, 2 * blk), F32),
            pltpu.VMEM((n_keys, 2 * blk), F32),
            pltpu.VMEM((2, n_blocks, blk), F32),
            pltpu.VMEM((2, n_blocks, blk), F32),
            pltpu.VMEM((2, n_keys, blk), BF16),
            pltpu.VMEM((2, n_keys, blk), BF16),
            pltpu.VMEM((n_query_blocks, d, blk), F32),
        ],
        compiler_params=_params("arbitrary"),
        name=f"moba_attn_from_block_{first_block}",
    )(*([prior] if fills_prior else []), x, q0, q1, k0, k1, vt, kmean, *_stacks(*params))


def kernel(x, mem, ln_g, ln_b, x_wq, x_wkv, x_wo, ffn_w_in, ffn_w_out, ev_w_in, ev_w_out, a_ws, a_bs,
           a_ln_g, a_ln_b, b_norm_g, hgrn_lb_logits, od_w_qkv, od_w_out):
    bsz, t_len, d = x.shape
    assert d == D_MODEL and t_len % ROW_TILE == 0 and t_len % WIDE_ROW_TILE == 0 and t_len % MOBA_BLOCK == 0
    assert ROW_TILE % MOBA_BLOCK == 0 and t_len // MOBA_BLOCK > 1
    n = bsz * t_len
    mem2d = mem.reshape(bsz * mem.shape[1], d)

    def rows_of(v):
        return v.reshape(-1, 1, v.shape[-1])

    def seq(v):
        return v.reshape(bsz, t_len, d)

    ffn_w_in, ffn_w_out = ffn_w_in.astype(BF16), ffn_w_out.astype(BF16)
    od_w_out = od_w_out.astype(BF16)
    ln_g, ln_b = rows_of(ln_g), rows_of(ln_b)
    a_ln_g, a_ln_b, b_norm_g = rows_of(a_ln_g), rows_of(a_ln_b), rows_of(b_norm_g)
    a_bs_t = jnp.swapaxes(a_bs, 1, 2)
    lb_logits = hgrn_lb_logits[None]
    alibi_fill = _alibi_query_fill()[None]

    for layer in range(DEPTH):
        j = layer // 2
        norm = [(ln_g, 3 * layer), (ln_b, 3 * layer)]
        if layer % 2 == 0:
            x = _even_mixer(
                x, j, (ev_w_in, j), (ev_w_out, j), (a_ws, j), (a_bs_t, j), (a_ln_g, j), (a_ln_b, j),
                (b_norm_g, j), (lb_logits, 0), *norm)
        else:
            q0, q1, k0, k1, vt, kmean = _moba_qkv(x.reshape(n, d), (od_w_qkv, j), (alibi_fill, 0), bsz, t_len)
            attn_args = (x, seq(q0), seq(q1), seq(k0), seq(k1), vt,
                         kmean.reshape(bsz, t_len // MOBA_BLOCK, d), (od_w_out, j), *norm)
            n_blocks = t_len // MOBA_BLOCK
            filled = None
            for first_block in range(0, n_blocks, MOBA_BLOCKS_PER_CALL):
                filled = _moba_attn(filled, first_block, min(MOBA_BLOCKS_PER_CALL, n_blocks - first_block),
                                    *attn_args)
            x = filled
        kv = _mem_kv(mem2d, (x_wkv, layer)).reshape(bsz, mem.shape[1], 2 * d)
        x = _cross_attn(x, kv, (x_wq, layer), (x_wo, layer), (ln_g, 3 * layer + 1), (ln_b, 3 * layer + 1))
        x = _ffn(x.reshape(n, d), (ffn_w_in, layer), (ffn_w_out, layer),
                 (ln_g, 3 * layer + 2), (ln_b, 3 * layer + 2)).reshape(bsz, t_len, d)
    return x
```

```python
import functools
import math

import jax
import jax.numpy as jnp
from jax import lax
from jax.experimental import pallas as pl
from jax.experimental.pallas import tpu as pltpu

D_MODEL = 1024
DEPTH = 2
ALPHA = (2.0 * DEPTH) ** 0.25
LN_EPS = 1e-5

GMLP_WIDTH = D_MODEL // 2
GMLP_GROUPS = 4
GMLP_GDIM = GMLP_WIDTH // GMLP_GROUPS
GMLP_CHUNK = 128
HGRN_WIDTH = D_MODEL // 2
HGRN_HEADS = 4
HGRN_DK = HGRN_WIDTH // HGRN_HEADS
HGRN_CHUNK = 64
EVEN_IN_WIDTH = 2 * GMLP_WIDTH + 4 * HGRN_WIDTH

MOBA_HEADS = 16
MOBA_HDIM = D_MODEL // MOBA_HEADS
MOBA_BLOCK = 256
MOBA_TOPK = 3
MOBA_PAIR = 2 * MOBA_HDIM
MOBA_BLOCKS_PER_CALL = 2

MEM_HEADS = 4
MEM_HDIM = D_MODEL // MEM_HEADS

D_FF = int(math.ceil(8 * D_MODEL / 3 / 256)) * 256
FFN_CHUNK = 256

ROW_TILE = 512
WIDE_ROW_TILE = 1024
FFN_ROW_TILE = WIDE_ROW_TILE
FFN_NORM_SLICES = 8
V7X_VMEM_LIMIT = 56 * 1024 * 1024

MASKED = -1e30

BF16 = jnp.bfloat16
F32 = jnp.float32


def _dot(a, b):
    return jnp.dot(a, b, preferred_element_type=F32)


def _dot_nt(a, b):
    return lax.dot_general(a, b, (((1,), (1,)), ((), ())), preferred_element_type=F32)


def _dot_tn(a, b):
    return lax.dot_general(a, b, (((0,), (0,)), ((), ())), preferred_element_type=F32)


def _layer_norm(z, g, b):
    mu = jnp.mean(z, axis=-1, keepdims=True)
    zc = z - mu
    var = jnp.mean(zc * zc, axis=-1, keepdims=True)
    return zc * lax.rsqrt(var + LN_EPS) * g + b


def _zero_after(token):
    bits = lax.bitcast_convert_type(token, jnp.int32)
    return lax.shift_right_logical(lax.shift_right_logical(bits, 16), 16).astype(F32)


def _residual_norm(x, y, g, b, after=None):
    alpha = ALPHA if after is None else ALPHA + _zero_after(after)
    return _layer_norm(x * alpha + y, g, b)


def _resident(picked):
    stack, index = picked
    return pl.BlockSpec((None,) + stack.shape[1:], lambda *_: (index,) + (0,) * (stack.ndim - 1),
                        pipeline_mode=pl.Buffered(1))


def _stacks(*picked):
    return [stack for stack, _ in picked]


def _bf16_copy(picked):
    stack, _ = picked
    return pltpu.VMEM(stack.shape[1:], BF16)


def _cast_weights_once(n_grid_axes, *pairs):
    first = pl.program_id(0) == 0
    for axis in range(1, n_grid_axes):
        first = jnp.logical_and(first, pl.program_id(axis) == 0)

    @pl.when(first)
    def _():
        for src, dst in pairs:
            dst[...] = src[...].astype(BF16)


def _params(*semantics):
    return pltpu.CompilerParams(dimension_semantics=semantics, vmem_limit_bytes=V7X_VMEM_LIMIT)


def _even_mixer_kernel(x_ref, w_in_f32_ref, w_out_f32_ref, ws_ref, bs_ref, aln_g_ref, aln_b_ref, bnorm_ref,
                       lb_logits_ref, ln_g_ref, ln_b_ref, o_ref,
                       q_s, f_s, i_s, g_s, y_s, state_s, w_in_ref, w_out_ref, *, lb_index):
    tm = x_ref.shape[1]
    _cast_weights_once(2, (w_in_f32_ref, w_in_ref), (w_out_f32_ref, w_out_ref))
    x = x_ref[0]
    xb = x.astype(BF16)

    @pl.when(pl.program_id(1) == 0)
    def _():
        state_s[...] = jnp.zeros_like(state_s)

    base = 2 * GMLP_WIDTH
    u_pre = _dot(xb, w_in_ref[:, 0:GMLP_WIDTH])
    v_pre = _dot(xb, w_in_ref[:, GMLP_WIDTH:2 * GMLP_WIDTH])
    q_pre = _dot(xb, w_in_ref[:, base:base + HGRN_WIDTH])
    q_s[...] = q_pre
    f_pre = _dot(xb, w_in_ref[:, base + HGRN_WIDTH:base + 2 * HGRN_WIDTH])
    f_s[...] = f_pre
    i_s[...] = jax.nn.silu(_dot(xb, w_in_ref[:, base + 2 * HGRN_WIDTH:base + 3 * HGRN_WIDTH]))
    g_s[...] = jax.nn.silu(_dot(xb, w_in_ref[:, base + 3 * HGRN_WIDTH:base + 4 * HGRN_WIDTH]))

    u = jax.nn.gelu(u_pre + _zero_after(q_pre[0:1, 0:1]))
    v = jax.nn.gelu(v_pre + _zero_after(f_pre[0:1, 0:1]))
    row = lax.broadcasted_iota(jnp.int32, (GMLP_CHUNK, GMLP_CHUNK), 0)
    col = lax.broadcasted_iota(jnp.int32, (GMLP_CHUNK, GMLP_CHUNK), 1)
    for g in range(GMLP_GROUPS):
        lanes = slice(g * GMLP_GDIM, (g + 1) * GMLP_GDIM)
        vn = _layer_norm(v[:, lanes], aln_g_ref[:, lanes], aln_b_ref[:, lanes]).astype(BF16)
        wg = jnp.where(col <= row, ws_ref[g], 0.0).astype(BF16)
        bias = bs_ref[:, g:g + 1]
        for c in range(tm // GMLP_CHUNK):
            rows = slice(c * GMLP_CHUNK, (c + 1) * GMLP_CHUNK)
            s = _dot(wg, vn[rows]) + bias
            y_s[rows, lanes] = (u[rows, lanes] * s).astype(BF16)

    n_chunks = tm // HGRN_CHUNK
    logits = lb_logits_ref[...]
    e = jnp.exp(logits - jnp.max(logits, axis=0, keepdims=True))
    lb = jnp.sum(e[0:lb_index + 1], axis=0, keepdims=True) / jnp.sum(e, axis=0, keepdims=True)

    crow =lax.broadcasted_iota(jnp.int32, (HGRN_CHUNK, HGRN_CHUNK), 0)
    ccol = lax.broadcasted_iota(jnp.int32, (HGRN_CHUNK, HGRN_CHUNK), 1)
    causal = ccol <= crow
    tril_ones = jnp.where(causal, 1.0, 0.0).astype(BF16)
    norm_g = bnorm_ref[...]

    for c in range(n_chunks):
        rows = slice(c * HGRN_CHUNK, (c + 1) * HGRN_CHUNK)
        f = lb + (1.0 - lb) * jax.nn.sigmoid(f_s[rows, :])
        log_f = jnp.log(f)
        log_f_hi = log_f.astype(BF16)
        log_f_lo = (log_f - log_f_hi.astype(F32)).astype(BF16)
        cum = _dot(tril_ones, log_f_hi) + _dot(tril_ones, log_f_lo)
        chunk_decay = jnp.exp(cum[HGRN_CHUNK - 1:HGRN_CHUNK, :])
        k_back = (1.0 - f) * jnp.exp(-cum)
        q_dec = (q_s[rows, :] * jnp.exp(cum)).astype(BF16)
        k_dec = k_back.astype(BF16)
        k_tail = (k_back * chunk_decay).astype(BF16)
        val = i_s[rows, :].astype(BF16)
        gate = g_s[rows, :]
        for h in range(HGRN_HEADS):
            lanes = slice(h * HGRN_DK, (h + 1) * HGRN_DK)
            attn = jnp.where(causal, _dot_nt(q_dec[:, lanes], k_dec[:, lanes]), 0.0).astype(BF16)
            state_t = state_s[h]
            o = _dot(attn, val[:, lanes]) + _dot_nt(q_dec[:, lanes], state_t.astype(BF16))
            state_s[h] = state_t * chunk_decay[:, lanes] + _dot_tn(val[:, lanes], k_tail[:, lanes])
            rms = lax.rsqrt(jnp.mean(o * o, axis=-1, keepdims=True) + LN_EPS)
            y_b = o * rms * norm_g[:, lanes] * gate[:, lanes]
            y_s[rows, GMLP_WIDTH + h * HGRN_DK:GMLP_WIDTH + (h + 1) * HGRN_DK] = y_b.astype(BF16)

    for rows in (slice(0, tm // 2), slice(tm // 2, tm)):
        y = _dot(y_s[rows, :], w_out_ref[...])
        o_ref[0, rows, :] = _residual_norm(x_ref[0, rows, :], y, ln_g_ref[...], ln_b_ref[...])


def _even_mixer(x, lb_index, *params):
    bsz, t_len, d = x.shape
    tm = ROW_TILE
    kern = functools.partial(_even_mixer_kernel, lb_index=lb_index)
    return pl.pallas_call(
        kern,
        grid=(bsz, t_len // tm),
        in_specs=[pl.BlockSpec((1, tm, d), lambda b, t: (b, t, 0))] + [_resident(p) for p in params],
        out_specs=pl.BlockSpec((1, tm, d), lambda b, t: (b, t, 0)),
        out_shape=jax.ShapeDtypeStruct(x.shape, F32),
        scratch_shapes=[
            pltpu.VMEM((tm, HGRN_WIDTH), F32), pltpu.VMEM((tm, HGRN_WIDTH), F32),
            pltpu.VMEM((tm, HGRN_WIDTH), F32), pltpu.VMEM((tm, HGRN_WIDTH), F32),
            pltpu.VMEM((tm, GMLP_WIDTH + HGRN_WIDTH), BF16),
            pltpu.VMEM((HGRN_HEADS, HGRN_DK, HGRN_DK), F32),
            _bf16_copy(params[0]), _bf16_copy(params[1]),
        ],
        compiler_params=_params("arbitrary", "arbitrary"),
        name="even_mixer",
    )(x, *_stacks(*params))


def _mem_kv_kernel(mem_ref, w_f32_ref, o_ref, w_ref):
    _cast_weights_once(1, (w_f32_ref, w_ref))
    o_ref[...] = _dot(mem_ref[...].astype(BF16), w_ref[...]).astype(BF16)


def _mem_kv(mem2d, w_kv):
    n, d = mem2d.shape
    tm = ROW_TILE
    width = w_kv[0].shape[-1]
    return pl.pallas_call(
        _mem_kv_kernel,
        grid=(n // tm,),
        in_specs=[pl.BlockSpec((tm, d), lambda i: (i, 0)), _resident(w_kv)],
        out_specs=pl.BlockSpec((tm, width), lambda i: (i, 0)),
        out_shape=jax.ShapeDtypeStruct((n, width), BF16),
        scratch_shapes=[_bf16_copy(w_kv)],
        compiler_params=_params("arbitrary"),
        name="mem_kv",
    )(mem2d, *_stacks(w_kv))


def _cross_attn_kernel(x_ref, kv_ref, wq_f32_ref, wo_f32_ref, ln_g_ref, ln_b_ref, o_ref, att_s, y_s,
                       wq_ref, wo_ref):
    _cast_weights_once(2, (wq_f32_ref, wq_ref), (wo_f32_ref, wo_ref))
    half = x_ref.shape[1] // 2
    slice_rows = half // MEM_HEADS

    def attend(rows):
        q = (_dot(x_ref[0, rows, :].astype(BF16), wq_ref[...]) * (MEM_HDIM ** -0.5)).astype(BF16)
        done = []
        for h in range(MEM_HEADS):
            lanes = slice(h * MEM_HDIM, (h + 1) * MEM_HDIM)
            k_h = kv_ref[0, :, h * MEM_HDIM:(h + 1) * MEM_HDIM]
            v_h = kv_ref[0, :, D_MODEL + h * MEM_HDIM:D_MODEL + (h + 1) * MEM_HDIM]
            s = _dot_nt(q[:, lanes], k_h)
            p = jnp.exp(s - jnp.max(s, axis=-1, keepdims=True))
            denom = jnp.sum(p, axis=-1, keepdims=True)
            weighted = _dot(p.astype(BF16), v_h)
            att_s[rows, lanes] = (weighted / denom).astype(BF16)
            done.append(weighted[0:1, 0:1])
        y_s[rows, :] = _dot(att_s[rows, :], wo_ref[...])
        return done

    def finish(rows, after=None):
        o_ref[0, rows, :] = _residual_norm(x_ref[0, rows, :], y_s[rows, :], ln_g_ref[...], ln_b_ref[...], after)

    attend(slice(0, half))
    done = attend(slice(half, 2 * half))
    for h in range(MEM_HEADS):
        finish(slice(h * slice_rows, (h + 1) * slice_rows), after=done[h])
    finish(slice(half, 2 * half))


def _cross_attn(x, kv, *params):
    bsz, t_len, d = x.shape
    tm = WIDE_ROW_TILE
    return pl.pallas_call(
        _cross_attn_kernel,
        grid=(bsz, t_len // tm),
        in_specs=[
            pl.BlockSpec((1, tm, d), lambda b, t: (b, t, 0)),
            pl.BlockSpec((1,) + kv.shape[1:], lambda b, t: (b, 0, 0)),
        ] + [_resident(p) for p in params],
        out_specs=pl.BlockSpec((1, tm, d), lambda b, t: (b, t, 0)),
        out_shape=jax.ShapeDtypeStruct(x.shape, F32),
        scratch_shapes=[pltpu.VMEM((tm, d), BF16), pltpu.VMEM((tm, d), F32),
                        _bf16_copy(params[0]), _bf16_copy(params[1])],
        compiler_params=_params("arbitrary", "arbitrary"),
        name="cross_attn",
    )(x, kv, *_stacks(*params))


def _ffn_kernel(x_ref, w_in_ref, w_out_ref, ln_g_ref, ln_b_ref, o_ref, acc_s):
    half = x_ref.shape[0] // 2
    n_chunks = D_FF // FFN_CHUNK
    slice_rows = half // FFN_NORM_SLICES

    def chunk(rows, xb, c):
        cols = slice(c * FFN_CHUNK, (c + 1) * FFN_CHUNK)
        gate = _dot(xb, w_in_ref[:, cols])
        up = _dot(xb, w_in_ref[:, D_FF + c * FFN_CHUNK:D_FF + (c + 1) * FFN_CHUNK])
        act = (jax.nn.silu(gate) * up).astype(BF16)
        part = _dot(act, w_out_ref[cols, :])
        if c == 0:
            acc_s[rows, :] = part
        else:
            acc_s[rows, :] += part
        return part[0:1, :]

    def finish(rows, after=None):
        o_ref[rows, :] = _residual_norm(x_ref[rows, :], acc_s[rows, :], ln_g_ref[...], ln_b_ref[...], after)

    first, second = slice(0, half), slice(half, 2 * half)
    xb_first = x_ref[first, :].astype(BF16)
    xb_second = x_ref[second, :].astype(BF16)
    for c in range(n_chunks):
        chunk(first, xb_first, c)
    for c in range(n_chunks):
        done = chunk(second, xb_second, c)
        if c < FFN_NORM_SLICES:
            finish(slice(c * slice_rows, (c + 1) * slice_rows), after=done)
    finish(second)


def _ffn(x2d, *params):
    n, d = x2d.shape
    tm = FFN_ROW_TILE
    assert n % tm == 0
    return pl.pallas_call(
        _ffn_kernel,
        grid=(n // tm,),
        in_specs=[pl.BlockSpec((tm, d), lambda i: (i, 0))] + [_resident(p) for p in params],
        out_specs=pl.BlockSpec((tm, d), lambda i: (i, 0)),
        out_shape=jax.ShapeDtypeStruct((n, d), F32),
        scratch_shapes=[pltpu.VMEM((tm, d), F32)],
        compiler_params=_params("parallel"),
        name="ffn",
    )(x2d, *_stacks(*params))


LOG2_E = 1.4426950408889634
ALIBI_LANES = 3


def _moba_qkv_kernel(x_ref, w_f32_ref, fill_ref, q0_ref, q1_ref, k0_ref, k1_ref, vt_ref, kmean_ref, w_ref,
                     *, tiles_per_seq):
    _cast_weights_once(1, (w_f32_ref, w_ref))
    tm = x_ref.shape[0]
    xb = x_ref[...].astype(BF16)
    q = _dot(xb, w_ref[:, 0:D_MODEL]) * (MOBA_HDIM ** -0.5 * LOG2_E)
    k = _dot(xb, w_ref[:, D_MODEL:2 * D_MODEL])
    for i in range(tm // MOBA_BLOCK):
        kmean_ref[i] = jnp.mean(k[i * MOBA_BLOCK:(i + 1) * MOBA_BLOCK], axis=0, keepdims=True)
    vt_ref[0] = lax.dot_general(w_ref[:, 2 * D_MODEL:3 * D_MODEL], xb, (((0,), (1,)), ((), ())),
                                preferred_element_type=F32).astype(BF16)

    lane = lax.broadcasted_iota(jnp.int32, (tm, MOBA_PAIR), 1)
    pos = lax.broadcasted_iota(jnp.int32, (tm, MOBA_PAIR), 0) + (pl.program_id(0) % tiles_per_seq) * tm
    pos_block = ((pos // MOBA_BLOCK) * MOBA_BLOCK).astype(F32)
    pos_offset = (pos % MOBA_BLOCK).astype(F32)
    for e, (q_ref, k_ref) in enumerate(((q0_ref, k0_ref), (q1_ref, k1_ref))):
        own = (lane < MOBA_HDIM) if e == 0 else (lane >= MOBA_HDIM)
        partner = MOBA_HDIM * (1 - e)
        key_fill = jnp.where((lane >= partner) & (lane < partner + ALIBI_LANES), pos_block,
                             jnp.where((lane >= partner + ALIBI_LANES) & (lane < partner + 2 * ALIBI_LANES),
                                       pos_offset, 0.0))
        for p in range(MOBA_HEADS // 2):
            slab = slice(p * MOBA_PAIR, (p + 1) * MOBA_PAIR)
            q_ref[:, slab] = jnp.where(own, q[:, slab], fill_ref[e:e + 1, slab]).astype(BF16)
            k_ref[:, slab] = jnp.where(own, k[:, slab], key_fill).astype(BF16)


def _moba_qkv(x2d, w_qkv, fill, bsz, t_len):
    n, d = x2d.shape
    tm = WIDE_ROW_TILE
    tiles_per_seq = t_len // tm
    rows = pl.BlockSpec((tm, d), lambda i: (i, 0))
    return pl.pallas_call(
        functools.partial(_moba_qkv_kernel, tiles_per_seq=tiles_per_seq),
        grid=(n // tm,),
        in_specs=[rows, _resident(w_qkv), _resident(fill)],
        out_specs=[
            rows, rows, rows, rows,
            pl.BlockSpec((1, d, tm), lambda i: (i // tiles_per_seq, 0, i % tiles_per_seq)),
            pl.BlockSpec((tm // MOBA_BLOCK, 1, d), lambda i: (i, 0, 0)),
        ],
        out_shape=[
            jax.ShapeDtypeStruct((n, d), BF16), jax.ShapeDtypeStruct((n, d), BF16),
            jax.ShapeDtypeStruct((n, d), BF16), jax.ShapeDtypeStruct((n, d), BF16),
            jax.ShapeDtypeStruct((bsz, d, t_len), BF16),
            jax.ShapeDtypeStruct((n // MOBA_BLOCK, 1, d), F32),
        ],
        scratch_shapes=[_bf16_copy(w_qkv)],
        compiler_params=_params("arbitrary"),
        name="moba_qkv",
    )(x2d, *_stacks(w_qkv, fill))


def _alibi_query_fill():
    slopes = jnp.asarray([2.0 ** (-8.0 * (h + 1) / MOBA_HEADS) for h in range(MOBA_HEADS)], F32) * LOG2_E
    pieces = []
    rest = slopes
    for _ in range(ALIBI_LANES):
        piece = rest.astype(BF16).astype(F32)
        pieces.append(piece)
        rest = rest - piece
    pieces = jnp.stack(pieces + pieces, axis=1)
    fill = jnp.zeros((2, MOBA_HEADS // 2, MOBA_PAIR), F32)
    for e in range(2):
        partner = MOBA_HDIM * (1 - e)
        fill = fill.at[e, :, partner:partner + 2 * ALIBI_LANES].set(pieces[e::2])
    return fill.reshape(2, D_MODEL)


def _moba_attn_kernel(x_ref, q0_ref, q1_ref, k0_ref, k1_ref, vt_ref, kmean_ref, w_out_ref, ln_g_ref, ln_b_ref,
                      o_ref, sa_s, sb_s, sha_s, shb_s, pa_s, pb_s, att_s, *, first_block, n_query_blocks):
    n_blocks = kmean_ref.shape[1]
    blk = MOBA_BLOCK
    q_refs = (q0_ref, q1_ref)
    k_refs = (k0_ref, k1_ref)
    key_i = lax.broadcasted_iota(jnp.int32, (blk, blk), 0)
    qry_i = lax.broadcasted_iota(jnp.int32, (blk, blk), 1)
    causal = key_i <= qry_i
    blk_i = lax.broadcasted_iota(jnp.int32, (n_blocks, blk), 0)
    mean_lane = lax.broadcasted_iota(jnp.int32, (n_blocks, MOBA_PAIR), 1)

    def attend(n_past, query_rows, att_buf, buffers_read):
        n_keys = (n_past + 1) * blk
        select = n_past > MOBA_TOPK

        def pair_lanes(hp):
            return pl.ds(pl.multiple_of(hp * MOBA_PAIR, MOBA_PAIR), MOBA_PAIR)

        def scores_and_shifts(hp, s_buf, sh_buf, after=None):
            lanes = pair_lanes(hp)
            for e in range(2):
                cols = slice(e * blk, (e + 1) * blk)
                q_e = q_refs[e][0, query_rows, lanes]
                if after is not None:
                    q_e = (q_e.astype(F32) + _zero_after(after)).astype(BF16)
                if select:
                    own = (mean_lane < MOBA_HDIM) if e == 0 else (mean_lane >= MOBA_HDIM)
                    kmean = jnp.where(own, kmean_ref[0, :, lanes], 0.0)
                    kmean_hi = kmean.astype(BF16)
                    kmean_lo = (kmean - kmean_hi.astype(F32)).astype(BF16)
                    aff = _dot_nt(kmean_hi, q_e) + _dot_nt(kmean_lo, q_e)
                    rank = jnp.zeros((n_blocks, blk), F32)
                    for jp in range(n_past):
                        other = aff[jp:jp + 1, :]
                        beats = (other > aff) | ((other == aff) & (jp < blk_i))
                        rank = rank + jnp.where(beats, 1.0, 0.0)
                    chosen = rank < MOBA_TOPK
                m = None
                for j in range(n_past + 1):
                    rows = slice(j * blk, (j + 1) * blk)
                    t = _dot_nt(k_refs[e][0, rows, lanes], q_e)
                    if j == n_past:
                        t = jnp.where(causal, t, MASKED)
                    s_buf[rows, cols] = t
                    m_j = jnp.max(t, axis=0, keepdims=True)
                    if select and j < n_past:
                        m_j = jnp.where(chosen[j:j + 1, :], m_j, MASKED)
                    m = m_j if m is None else jnp.maximum(m, m_j)
                shifts = jnp.broadcast_to(m, (n_blocks, blk))
                if select:
                    shifts = jnp.where(chosen | (blk_i == n_past), shifts, -MASKED)
                sh_buf[e] = shifts

        def probabilities(s_buf, sh_buf, p_buf, after=None):
            hold = 0.0 if after is None else _zero_after(after)
            for e in range(2):
                cols = slice(e * blk, (e + 1) * blk)
                for j in range(n_past + 1):
                    rows = slice(j * blk, (j + 1) * blk)
                    shift = sh_buf[e, j:j + 1, :] + hold
                    p_buf[e, rows, :] = jnp.exp2(s_buf[rows, cols] - shift).astype(BF16)

        def weighted_values(hp, p_buf):
            for e in range(2):
                feat = pl.ds(pl.multiple_of(hp * MOBA_PAIR + e * MOBA_HDIM, MOBA_HDIM), MOBA_HDIM)
                values = jnp.concatenate([vt_ref[0, feat, 0:n_keys], jnp.ones((16, n_keys), BF16)], axis=0)
                acc = _dot(values, p_buf[e, 0:n_keys, :])
                denom = acc[MOBA_HDIM:MOBA_HDIM + 1]
                att_buf[feat, :] = acc[0:MOBA_HDIM] / denom
            return denom[:, 0:1]

        n_pairs = MOBA_HEADS // 2
        read_a, read_b = buffers_read
        scores_and_shifts(0, sa_s, sha_s)
        scores_and_shifts(1, sb_s, shb_s, after=read_b)
        probabilities(sa_s, sha_s, pa_s, after=read_a)

        def two_pairs(i, carry):
            hp = 2 * i + 1
            scores_and_shifts(hp + 1, sa_s, sha_s)
            probabilities(sb_s, shb_s, pb_s)
            consumed = weighted_values(hp - 1, pa_s)
            scores_and_shifts(hp + 2, sb_s, shb_s)
            probabilities(sa_s, sha_s, pa_s, after=consumed)
            weighted_values(hp, pb_s)
            return carry

        lax.fori_loop(0, n_pairs // 2 - 1, two_pairs, 0)
        probabilities(sb_s, shb_s, pb_s)
        return weighted_values(n_pairs - 2, pa_s), weighted_values(n_pairs - 1, pb_s)

    buffers_read = (None, None)
    for j in range(n_query_blocks):
        query_rows = slice(j * blk, (j + 1) * blk)
        buffers_read = attend(first_block + j, query_rows, att_s.at[j], buffers_read)
        att = att_s[j].T.astype(BF16)
        y = _dot(att, w_out_ref[...])
        o_ref[0, query_rows, :] = _residual_norm(x_ref[0, query_rows, :], y, ln_g_ref[...], ln_b_ref[...])


def _moba_attn(x, first_block, n_query_blocks, q0, q1, k0, k1, vt, kmean, *params):
    bsz, t_len, d = x.shape
    blk = MOBA_BLOCK
    n_blocks = t_len // blk
    n_keys = (first_block + n_query_blocks) * blk
    assert first_block % n_query_blocks == 0
    query_rows = pl.BlockSpec((1, n_query_blocks * blk, d), lambda b: (b, first_block // n_query_blocks, 0))
    visible_keys = pl.BlockSpec((1, n_keys, d), lambda b: (b, 0, 0))
    return pl.pallas_call(
        functools.partial(_moba_attn_kernel, first_block=first_block, n_query_blocks=n_query_blocks),
        grid=(bsz,),
        in_specs=[
            query_rows, query_rows, query_rows, visible_keys, visible_keys,
            pl.BlockSpec((1, d, n_keys), lambda b: (b, 0, 0)),
            pl.BlockSpec((1, n_blocks, d), lambda b: (b, 0, 0)),
        ] + [_resident(p) for p in params],
        out_specs=query_rows,
        out_shape=jax.ShapeDtypeStruct(x.shape, F32),
        input_output_aliases={0: 0},
        scratch_shapes=[
            pltpu.VMEM((n_keys, 2 * blk), F32),
            pltpu.VMEM((n_keys, 2 * blk), F32),
            pltpu.VMEM((2, n_blocks, blk), F32),
            pltpu.VMEM((2, n_blocks, blk), F32),
            pltpu.VMEM((2, n_keys, blk), BF16),
            pltpu.VMEM((2, n_keys, blk), BF16),
            pltpu.VMEM((n_query_blocks, d, blk), F32),
        ],
        compiler_params=_params("arbitrary"),
        name=f"moba_attn_from_block_{first_block}",
    )(x, q0, q1, k0, k1, vt, kmean, *_stacks(*params))


def kernel(x, mem, ln_g, ln_b, x_wq, x_wkv, x_wo, ffn_w_in, ffn_w_out, ev_w_in, ev_w_out, a_ws, a_bs,
           a_ln_g, a_ln_b, b_norm_g, hgrn_lb_logits, od_w_qkv, od_w_out):
    bsz, t_len, d = x.shape
    assert d == D_MODEL and t_len % ROW_TILE == 0 and t_len % WIDE_ROW_TILE == 0 and t_len % MOBA_BLOCK == 0
    assert ROW_TILE % MOBA_BLOCK == 0 and t_len // MOBA_BLOCK > 1
    n = bsz * t_len
    mem2d = mem.reshape(bsz * mem.shape[1], d)

    def rows_of(v):
        return v.reshape(-1, 1, v.shape[-1])

    def seq(v):
        return v.reshape(bsz, t_len, d)

    ffn_w_in, ffn_w_out = ffn_w_in.astype(BF16), ffn_w_out.astype(BF16)
    od_w_out = od_w_out.astype(BF16)
    ln_g, ln_b = rows_of(ln_g), rows_of(ln_b)
    a_ln_g, a_ln_b, b_norm_g = rows_of(a_ln_g), rows_of(a_ln_b), rows_of(b_norm_g)
    a_bs_t = jnp.swapaxes(a_bs, 1, 2)
    lb_logits = hgrn_lb_logits[None]
    alibi_fill = _alibi_query_fill()[None]

    for layer in range(DEPTH):
        j = layer // 2
        norm = [(ln_g, 3 * layer), (ln_b, 3 * layer)]
        if layer % 2 == 0:
            x = _even_mixer(
                x, j, (ev_w_in, j), (ev_w_out, j), (a_ws, j), (a_bs_t, j), (a_ln_g, j), (a_ln_b, j),
                (b_norm_g, j), (lb_logits, 0), *norm)
        else:
            q0, q1, k0, k1, vt, kmean = _moba_qkv(x.reshape(n, d), (od_w_qkv, j), (alibi_fill, 0), bsz, t_len)
            attn_args = (seq(q0), seq(q1), seq(k0), seq(k1), vt,
                         kmean.reshape(bsz, t_len // MOBA_BLOCK, d), (od_w_out, j), *norm)
            n_blocks = t_len // MOBA_BLOCK
            for first_block in range(0, n_blocks, MOBA_BLOCKS_PER_CALL):
                x = _moba_attn(x, first_block, min(MOBA_BLOCKS_PER_CALL, n_blocks - first_block), *attn_args)
        kv = _mem_kv(mem2d, (x_wkv, layer)).reshape(bsz, mem.shape[1], 2 * d)
        x = _cross_attn(x, kv, (x_wq, layer), (x_wo, layer), (ln_g, 3 * layer + 1), (ln_b, 3 * layer + 1))
        x = _ffn(x.reshape(n, d), (ffn_w_in, layer), (ffn_w_out, layer),
                 (ln_g, 3 * layer + 2), (ln_b, 3 * layer + 2)).reshape(bsz, t_len, d)
    return x
```

```python
import functools
import math

import jax
import jax.numpy as jnp
from jax import lax
from jax.experimental import pallas as pl
from jax.experimental.pallas import tpu as pltpu

D_MODEL = 1024
DEPTH = 2
ALPHA = (2.0 * DEPTH) ** 0.25
LN_EPS = 1e-5

GMLP_WIDTH = D_MODEL // 2
GMLP_GROUPS = 4
GMLP_GDIM = GMLP_WIDTH // GMLP_GROUPS
GMLP_CHUNK = 128
HGRN_WIDTH = D_MODEL // 2
HGRN_HEADS = 4
HGRN_DK = HGRN_WIDTH // HGRN_HEADS
HGRN_CHUNK = 64
EVEN_IN_WIDTH = 2 * GMLP_WIDTH + 4 * HGRN_WIDTH

MOBA_HEADS = 16
MOBA_HDIM = D_MODEL // MOBA_HEADS
MOBA_BLOCK = 256
MOBA_TOPK = 3
MOBA_PAIR = 2 * MOBA_HDIM
MOBA_BLOCKS_PER_CALL = 2

MEM_HEADS = 4
MEM_HDIM = D_MODEL // MEM_HEADS

D_FF = int(math.ceil(8 * D_MODEL / 3 / 256)) * 256
FFN_CHUNK = 256

ROW_TILE = 1024
MEM_ROW_TILE = 512
EVEN_ROW_TILE = 512
FFN_NORM_SLICES = 8
V7X_VMEM_LIMIT = 56 * 1024 * 1024

MASKED = -1e30

BF16 = jnp.bfloat16
F32 = jnp.float32
BF16_SUBLANES = 16


def _dot(a, b):
    return jnp.dot(a, b, preferred_element_type=F32)


def _dot_nt(a, b):
    return lax.dot_general(a, b, (((1,), (1,)), ((), ())), preferred_element_type=F32)


def _dot_tn(a, b):
    return lax.dot_general(a, b, (((0,), (0,)), ((), ())), preferred_element_type=F32)


def _layer_norm(z, g, b):
    mu = jnp.mean(z, axis=-1, keepdims=True)
    zc = z - mu
    var = jnp.mean(zc * zc, axis=-1, keepdims=True)
    return zc * lax.rsqrt(var + LN_EPS) * g + b


def _zero_after(token):
    bits = lax.bitcast_convert_type(token, jnp.int32)
    return lax.shift_right_logical(lax.shift_right_logical(bits, 16), 16).astype(F32)


def _residual_norm(x, y, g, b, after=None):
    alpha = ALPHA if after is None else ALPHA + _zero_after(after)
    return _layer_norm(x * alpha + y, g, b)


def _resident(picked):
    stack, index = picked
    return pl.BlockSpec((None,) + stack.shape[1:], lambda *_: (index,) + (0,) * (stack.ndim - 1),
                        pipeline_mode=pl.Buffered(1))


def _stacks(*picked):
    return [stack for stack, _ in picked]


def _bf16_copy(picked):
    stack, _ = picked
    return pltpu.VMEM(stack.shape[1:], BF16)


def _cast_weights_once(n_grid_axes, *pairs):
    first = pl.program_id(0) == 0
    for axis in range(1, n_grid_axes):
        first = jnp.logical_and(first, pl.program_id(axis) == 0)

    @pl.when(first)
    def _():
        for src, dst in pairs:
            dst[...] = src[...].astype(BF16)


def _params(*semantics):
    return pltpu.CompilerParams(dimension_semantics=semantics, vmem_limit_bytes=V7X_VMEM_LIMIT)


def _even_mixer_kernel(x_ref, w_in_f32_ref, w_out_f32_ref, ws_ref, bs_ref, aln_g_ref, aln_b_ref, bnorm_ref,
                       lb_logits_ref, ln_g_ref, ln_b_ref, o_ref,
                       q_s, f_s, i_s, g_s, y_s, state_s, w_in_ref, w_out_ref, *, lb_index):
    tm = x_ref.shape[1]
    _cast_weights_once(2, (w_in_f32_ref, w_in_ref), (w_out_f32_ref, w_out_ref))
    x = x_ref[0]
    xb = x.astype(BF16)

    @pl.when(pl.program_id(1) == 0)
    def _():
        state_s[...] = jnp.zeros_like(state_s)

    base = 2 * GMLP_WIDTH
    u_pre = _dot(xb, w_in_ref[:, 0:GMLP_WIDTH])
    v_pre = _dot(xb, w_in_ref[:, GMLP_WIDTH:2 * GMLP_WIDTH])
    q_pre = _dot(xb, w_in_ref[:, base:base + HGRN_WIDTH])
    q_s[...] = q_pre
    f_pre = _dot(xb, w_in_ref[:, base + HGRN_WIDTH:base + 2 * HGRN_WIDTH])
    f_s[...] = f_pre
    i_s[...] = jax.nn.silu(_dot(xb, w_in_ref[:, base + 2 * HGRN_WIDTH:base + 3 * HGRN_WIDTH]))
    g_s[...] = jax.nn.silu(_dot(xb, w_in_ref[:, base + 3 * HGRN_WIDTH:base + 4 * HGRN_WIDTH]))

    u = jax.nn.gelu(u_pre + _zero_after(q_pre[0:1, 0:1]))
    v = jax.nn.gelu(v_pre + _zero_after(f_pre[0:1, 0:1]))
    row = lax.broadcasted_iota(jnp.int32, (GMLP_CHUNK, GMLP_CHUNK), 0)
    col = lax.broadcasted_iota(jnp.int32, (GMLP_CHUNK, GMLP_CHUNK), 1)
    for g in range(GMLP_GROUPS):
        lanes = slice(g * GMLP_GDIM, (g + 1) * GMLP_GDIM)
        vn = _layer_norm(v[:, lanes], aln_g_ref[:, lanes], aln_b_ref[:, lanes]).astype(BF16)
        wg = jnp.where(col <= row, ws_ref[g], 0.0).astype(BF16)
        bias = bs_ref[:, g:g + 1]
        for c in range(tm // GMLP_CHUNK):
            rows = slice(c * GMLP_CHUNK, (c + 1) * GMLP_CHUNK)
            s = _dot(wg, vn[rows]) + bias
            y_s[rows, lanes] = (u[rows, lanes] * s).astype(BF16)

    n_chunks = tm // HGRN_CHUNK
    logits = lb_logits_ref[...]
    e = jnp.exp(logits - jnp.max(logits, axis=0, keepdims=True))
    lb = jnp.sum(e[0:lb_index + 1], axis=0, keepdims=True) / jnp.sum(e, axis=0, keepdims=True)

    crow =lax.broadcasted_iota(jnp.int32, (HGRN_CHUNK, HGRN_CHUNK), 0)
    ccol = lax.broadcasted_iota(jnp.int32, (HGRN_CHUNK, HGRN_CHUNK), 1)
    causal = ccol <= crow
    tril_ones = jnp.where(causal, 1.0, 0.0).astype(BF16)
    norm_g = bnorm_ref[...]

    for c in range(n_chunks):
        rows = slice(c * HGRN_CHUNK, (c + 1) * HGRN_CHUNK)
        f = lb + (1.0 - lb) * jax.nn.sigmoid(f_s[rows, :])
        log_f = jnp.log(f)
        log_f_hi = log_f.astype(BF16)
        log_f_lo = (log_f - log_f_hi.astype(F32)).astype(BF16)
        cum = _dot(tril_ones, log_f_hi) + _dot(tril_ones, log_f_lo)
        chunk_decay = jnp.exp(cum[HGRN_CHUNK - 1:HGRN_CHUNK, :])
        k_back = (1.0 - f) * jnp.exp(-cum)
        q_dec = (q_s[rows, :] * jnp.exp(cum)).astype(BF16)
        k_dec = k_back.astype(BF16)
        k_tail = (k_back * chunk_decay).astype(BF16)
        val = i_s[rows, :].astype(BF16)
        gate = g_s[rows, :]
        for h in range(HGRN_HEADS):
            lanes = slice(h * HGRN_DK, (h + 1) * HGRN_DK)
            attn = jnp.where(causal, _dot_nt(q_dec[:, lanes], k_dec[:, lanes]), 0.0).astype(BF16)
            state_t = state_s[h]
            o = _dot(attn, val[:, lanes]) + _dot_nt(q_dec[:, lanes], state_t.astype(BF16))
            state_s[h] = state_t * chunk_decay[:, lanes] + _dot_tn(val[:, lanes], k_tail[:, lanes])
            rms = lax.rsqrt(jnp.mean(o * o, axis=-1, keepdims=True) + LN_EPS)
            y_b = o * rms * norm_g[:, lanes] * gate[:, lanes]
            y_s[rows, GMLP_WIDTH + h * HGRN_DK:GMLP_WIDTH + (h + 1) * HGRN_DK] = y_b.astype(BF16)

    for rows in (slice(0, tm // 2), slice(tm // 2, tm)):
        y = _dot(y_s[rows, :], w_out_ref[...])
        o_ref[0, rows, :] = _residual_norm(x_ref[0, rows, :], y, ln_g_ref[...], ln_b_ref[...])


def _even_mixer(x, lb_index, *params):
    bsz, t_len, d = x.shape
    tm = EVEN_ROW_TILE
    assert t_len % tm == 0
    kern = functools.partial(_even_mixer_kernel, lb_index=lb_index)
    return pl.pallas_call(
        kern,
        grid=(bsz, t_len // tm),
        in_specs=[pl.BlockSpec((1, tm, d), lambda b, t: (b, t, 0))] + [_resident(p) for p in params],
        out_specs=pl.BlockSpec((1, tm, d), lambda b, t: (b, t, 0)),
        out_shape=jax.ShapeDtypeStruct(x.shape, F32),
        scratch_shapes=[
            pltpu.VMEM((tm, HGRN_WIDTH), F32), pltpu.VMEM((tm, HGRN_WIDTH), F32),
            pltpu.VMEM((tm, HGRN_WIDTH), F32), pltpu.VMEM((tm, HGRN_WIDTH), F32),
            pltpu.VMEM((tm, GMLP_WIDTH + HGRN_WIDTH), BF16),
            pltpu.VMEM((HGRN_HEADS, HGRN_DK, HGRN_DK), F32),
            _bf16_copy(params[0]), _bf16_copy(params[1]),
        ],
        compiler_params=_params("arbitrary", "arbitrary"),
        name="even_mixer",
    )(x, *_stacks(*params))


def _mem_kv_kernel(mem_ref, w_f32_ref, o_ref, w_ref):
    _cast_weights_once(1, (w_f32_ref, w_ref))
    o_ref[...] = _dot(mem_ref[...].astype(BF16), w_ref[...]).astype(BF16)


def _mem_kv(mem2d, w_kv):
    n, d = mem2d.shape
    tm = MEM_ROW_TILE
    assert n % tm == 0
    width = w_kv[0].shape[-1]
    return pl.pallas_call(
        _mem_kv_kernel,
        grid=(n // tm,),
        in_specs=[pl.BlockSpec((tm, d), lambda i: (i, 0)), _resident(w_kv)],
        out_specs=pl.BlockSpec((tm, width), lambda i: (i, 0)),
        out_shape=jax.ShapeDtypeStruct((n, width), BF16),
        scratch_shapes=[_bf16_copy(w_kv)],
        compiler_params=_params("arbitrary"),
        name="mem_kv",
    )(mem2d, *_stacks(w_kv))


def _cross_attn_kernel(x_ref, kv_ref, wq_f32_ref, wo_f32_ref, ln_g_ref, ln_b_ref, o_ref, att_s, y_s,
                       wq_ref, wo_ref):
    _cast_weights_once(2, (wq_f32_ref, wq_ref), (wo_f32_ref, wo_ref))
    half = x_ref.shape[1] // 2
    slice_rows = half // MEM_HEADS

    def attend(rows):
        q = (_dot(x_ref[0, rows, :].astype(BF16), wq_ref[...]) * (MEM_HDIM ** -0.5)).astype(BF16)
        done = []
        for h in range(MEM_HEADS):
            lanes = slice(h * MEM_HDIM, (h + 1) * MEM_HDIM)
            k_h = kv_ref[0, :, h * MEM_HDIM:(h + 1) * MEM_HDIM]
            v_h = kv_ref[0, :, D_MODEL + h * MEM_HDIM:D_MODEL + (h + 1) * MEM_HDIM]
            s = _dot_nt(q[:, lanes], k_h)
            p = jnp.exp(s - jnp.max(s, axis=-1, keepdims=True))
            denom = jnp.sum(p, axis=-1, keepdims=True)
            weighted = _dot(p.astype(BF16), v_h)
            att_s[rows, lanes] = (weighted / denom).astype(BF16)
            done.append(weighted[0:1, 0:1])
        y_s[rows, :] = _dot(att_s[rows, :], wo_ref[...])
        return done

    def finish(rows, after=None):
        o_ref[0, rows, :] = _residual_norm(x_ref[0, rows, :], y_s[rows, :], ln_g_ref[...], ln_b_ref[...], after)

    attend(slice(0, half))
    done = attend(slice(half, 2 * half))
    for h in range(MEM_HEADS):
        finish(slice(h * slice_rows, (h + 1) * slice_rows), after=done[h])
    finish(slice(half, 2 * half))


def _cross_attn(x, kv, *params):
    bsz, t_len, d = x.shape
    tm = ROW_TILE
    return pl.pallas_call(
        _cross_attn_kernel,
        grid=(bsz, t_len // tm),
        in_specs=[
            pl.BlockSpec((1, tm, d), lambda b, t: (b, t, 0)),
            pl.BlockSpec((1,) + kv.shape[1:], lambda b, t: (b, 0, 0)),
        ] + [_resident(p) for p in params],
        out_specs=pl.BlockSpec((1, tm, d), lambda b, t: (b, t, 0)),
        out_shape=jax.ShapeDtypeStruct(x.shape, F32),
        scratch_shapes=[pltpu.VMEM((tm, d), BF16), pltpu.VMEM((tm, d), F32),
                        _bf16_copy(params[0]), _bf16_copy(params[1])],
        compiler_params=_params("arbitrary", "arbitrary"),
        name="cross_attn",
    )(x, kv, *_stacks(*params))


def _ffn_kernel(x_ref, w_in_ref, w_out_ref, ln_g_ref, ln_b_ref, o_ref, acc_s):
    half = x_ref.shape[0] // 2
    n_chunks = D_FF // FFN_CHUNK
    slice_rows = half // FFN_NORM_SLICES

    def chunk(rows, xb, c):
        cols = slice(c * FFN_CHUNK, (c + 1) * FFN_CHUNK)
        gate = _dot(xb, w_in_ref[:, cols])
        up = _dot(xb, w_in_ref[:, D_FF + c * FFN_CHUNK:D_FF + (c + 1) * FFN_CHUNK])
        act = (jax.nn.silu(gate) * up).astype(BF16)
        part = _dot(act, w_out_ref[cols, :])
        if c == 0:
            acc_s[rows, :] = part
        else:
            acc_s[rows, :] += part
        return part[0:1, :]

    def finish(rows, after=None):
        o_ref[rows, :] = _residual_norm(x_ref[rows, :], acc_s[rows, :], ln_g_ref[...], ln_b_ref[...], after)

    first, second = slice(0, half), slice(half, 2 * half)
    xb_first = x_ref[first, :].astype(BF16)
    xb_second = x_ref[second, :].astype(BF16)
    for c in range(n_chunks):
        chunk(first, xb_first, c)
    for c in range(n_chunks):
        done = chunk(second, xb_second, c)
        if c < FFN_NORM_SLICES:
            finish(slice(c * slice_rows, (c + 1) * slice_rows), after=done)
    finish(second)


def _ffn(x2d, *params):
    n, d = x2d.shape
    tm = ROW_TILE
    assert n % tm == 0
    return pl.pallas_call(
        _ffn_kernel,
        grid=(n // tm,),
        in_specs=[pl.BlockSpec((tm, d), lambda i: (i, 0))] + [_resident(p) for p in params],
        out_specs=pl.BlockSpec((tm, d), lambda i: (i, 0)),
        out_shape=jax.ShapeDtypeStruct((n, d), F32),
        scratch_shapes=[pltpu.VMEM((tm, d), F32)],
        compiler_params=_params("parallel"),
        name="ffn",
    )(x2d, *_stacks(*params))


LOG2_E = 1.4426950408889634
ALIBI_LANES = 3


def _moba_qkv_kernel(x_ref, w_f32_ref, fill_ref, q0_ref, q1_ref, k0_ref, k1_ref, vt_ref, kmean_ref, w_ref,
                     *, tiles_per_seq):
    _cast_weights_once(1, (w_f32_ref, w_ref))
    tm = x_ref.shape[0]
    xb = x_ref[...].astype(BF16)
    q = _dot(xb, w_ref[:, 0:D_MODEL]) * (MOBA_HDIM ** -0.5 * LOG2_E)
    k = _dot(xb, w_ref[:, D_MODEL:2 * D_MODEL])
    for i in range(tm // MOBA_BLOCK):
        kmean_ref[i] = jnp.mean(k[i * MOBA_BLOCK:(i + 1) * MOBA_BLOCK], axis=0, keepdims=True)
    vt_ref[0] = lax.dot_general(w_ref[:, 2 * D_MODEL:3 * D_MODEL], xb, (((0,), (1,)), ((), ())),
                                preferred_element_type=F32).astype(BF16)

    lane = lax.broadcasted_iota(jnp.int32, (tm, MOBA_PAIR), 1)
    pos = lax.broadcasted_iota(jnp.int32, (tm, MOBA_PAIR), 0) + (pl.program_id(0) % tiles_per_seq) * tm
    pos_block = ((pos // MOBA_BLOCK) * MOBA_BLOCK).astype(F32)
    pos_offset = (pos % MOBA_BLOCK).astype(F32)
    for e, (q_ref, k_ref) in enumerate(((q0_ref, k0_ref), (q1_ref, k1_ref))):
        own = (lane < MOBA_HDIM) if e == 0 else (lane >= MOBA_HDIM)
        partner = MOBA_HDIM * (1 - e)
        key_fill = jnp.where((lane >= partner) & (lane < partner + ALIBI_LANES), pos_block,
                             jnp.where((lane >= partner + ALIBI_LANES) & (lane < partner + 2 * ALIBI_LANES),
                                       pos_offset, 0.0))
        for p in range(MOBA_HEADS // 2):
            slab = slice(p * MOBA_PAIR, (p + 1) * MOBA_PAIR)
            q_ref[:, slab] = jnp.where(own, q[:, slab], fill_ref[e:e + 1, slab]).astype(BF16)
            k_ref[:, slab] = jnp.where(own, k[:, slab], key_fill).astype(BF16)


def _moba_qkv(x2d, w_qkv, fill, bsz, t_len):
    n, d = x2d.shape
    tm = ROW_TILE
    tiles_per_seq = t_len // tm
    rows = pl.BlockSpec((tm, d), lambda i: (i, 0))
    return pl.pallas_call(
        functools.partial(_moba_qkv_kernel, tiles_per_seq=tiles_per_seq),
        grid=(n // tm,),
        in_specs=[rows, _resident(w_qkv), _resident(fill)],
        out_specs=[
            rows, rows, rows, rows,
            pl.BlockSpec((1, d, tm), lambda i: (i // tiles_per_seq, 0, i % tiles_per_seq)),
            pl.BlockSpec((tm // MOBA_BLOCK, 1, d), lambda i: (i, 0, 0)),
        ],
        out_shape=[
            jax.ShapeDtypeStruct((n, d), BF16), jax.ShapeDtypeStruct((n, d), BF16),
            jax.ShapeDtypeStruct((n, d), BF16), jax.ShapeDtypeStruct((n, d), BF16),
            jax.ShapeDtypeStruct((bsz, d, t_len), BF16),
            jax.ShapeDtypeStruct((n // MOBA_BLOCK, 1, d), F32),
        ],
        scratch_shapes=[_bf16_copy(w_qkv)],
        compiler_params=_params("arbitrary"),
        name="moba_qkv",
    )(x2d, *_stacks(w_qkv, fill))


def _alibi_query_fill():
    slopes = jnp.asarray([2.0 ** (-8.0 * (h + 1) / MOBA_HEADS) for h in range(MOBA_HEADS)], F32) * LOG2_E
    pieces = []
    rest = slopes
    for _ in range(ALIBI_LANES):
        piece = rest.astype(BF16).astype(F32)
        pieces.append(piece)
        rest = rest - piece
    pieces = jnp.stack(pieces + pieces, axis=1)
    fill = jnp.zeros((2, MOBA_HEADS // 2, MOBA_PAIR), F32)
    for e in range(2):
        partner = MOBA_HDIM * (1 - e)
        fill = fill.at[e, :, partner:partner + 2 * ALIBI_LANES].set(pieces[e::2])
    return fill.reshape(2, D_MODEL)


def _moba_attn_kernel(x_ref, q0_ref, q1_ref, k0_ref, k1_ref, vt_ref, kmean_ref, w_out_ref, ln_g_ref, ln_b_ref,
                      o_ref, sa_s, sb_s, sha_s, shb_s, pa_s, pb_s, att_s, *, first_block, n_query_blocks):
    n_blocks = kmean_ref.shape[1]
    blk = MOBA_BLOCK
    q_refs = (q0_ref, q1_ref)
    k_refs = (k0_ref, k1_ref)
    key_i = lax.broadcasted_iota(jnp.int32, (blk, blk), 0)
    qry_i = lax.broadcasted_iota(jnp.int32, (blk, blk), 1)
    causal = key_i <= qry_i
    blk_i = lax.broadcasted_iota(jnp.int32, (n_blocks, blk), 0)
    mean_lane = lax.broadcasted_iota(jnp.int32, (n_blocks, MOBA_PAIR), 1)

    def attend(n_past, query_rows, att_buf, buffers_read):
        n_keys = (n_past + 1) * blk
        select = n_past > MOBA_TOPK

        def pair_lanes(hp):
            return pl.ds(pl.multiple_of(hp * MOBA_PAIR, MOBA_PAIR), MOBA_PAIR)

        def scores_and_shifts(hp, s_buf, sh_buf, after=None):
            lanes = pair_lanes(hp)
            for e in range(2):
                cols = slice(e * blk, (e + 1) * blk)
                q_e = q_refs[e][0, query_rows, lanes]
                if after is not None:
                    q_e = (q_e.astype(F32) + _zero_after(after)).astype(BF16)
                if select:
                    own = (mean_lane < MOBA_HDIM) if e == 0 else (mean_lane >= MOBA_HDIM)
                    kmean = jnp.where(own, kmean_ref[0, :, lanes], 0.0)
                    kmean_hi = kmean.astype(BF16)
                    kmean_lo = (kmean - kmean_hi.astype(F32)).astype(BF16)
                    aff = _dot_nt(kmean_hi, q_e) + _dot_nt(kmean_lo, q_e)
                    rank = jnp.zeros((n_blocks, blk), F32)
                    for jp in range(n_past):
                        other = aff[jp:jp + 1, :]
                        beats = (other > aff) | ((other == aff) & (jp < blk_i))
                        rank = rank + jnp.where(beats, 1.0, 0.0)
                    chosen = rank < MOBA_TOPK
                m = None
                for j in range(n_past + 1):
                    rows = slice(j * blk, (j + 1) * blk)
                    t = _dot_nt(k_refs[e][0, rows, lanes], q_e)
                    if j == n_past:
                        t = jnp.where(causal, t, MASKED)
                    s_buf[rows, cols] = t
                    m_j = jnp.max(t, axis=0, keepdims=True)
                    if select and j < n_past:
                        m_j = jnp.where(chosen[j:j + 1, :], m_j, MASKED)
                    m = m_j if m is None else jnp.maximum(m, m_j)
                shifts = jnp.broadcast_to(m, (n_blocks, blk))
                if select:
                    shifts = jnp.where(chosen | (blk_i == n_past), shifts, -MASKED)
                sh_buf[e] = shifts

        def probabilities(s_buf, sh_buf, p_buf, after=None):
            hold = 0.0 if after is None else _zero_after(after)
            for e in range(2):
                cols = slice(e * blk, (e + 1) * blk)
                for j in range(n_past + 1):
                    rows = slice(j * blk, (j + 1) * blk)
                    shift = sh_buf[e, j:j + 1, :] + hold
                    p_buf[e, rows, :] = jnp.exp2(s_buf[rows, cols] - shift).astype(BF16)

        def weighted_values(hp, p_buf):
            for e in range(2):
                feat = pl.ds(pl.multiple_of(hp * MOBA_PAIR + e * MOBA_HDIM, MOBA_HDIM), MOBA_HDIM)
                ones = jnp.ones((BF16_SUBLANES, n_keys), BF16)
                values = jnp.concatenate([vt_ref[0, feat, 0:n_keys], ones], axis=0)
                acc = _dot(values, p_buf[e, 0:n_keys, :])
                denom = acc[MOBA_HDIM:MOBA_HDIM + 1]
                att_buf[feat, :] = acc[0:MOBA_HDIM] / denom
            return denom[:, 0:1]

        n_pairs = MOBA_HEADS // 2
        read_a, read_b = buffers_read
        scores_and_shifts(0, sa_s, sha_s)
        scores_and_shifts(1, sb_s, shb_s, after=read_b)
        probabilities(sa_s, sha_s, pa_s, after=read_a)

        def two_pairs(i, carry):
            hp = 2 * i + 1
            scores_and_shifts(hp + 1, sa_s, sha_s)
            probabilities(sb_s, shb_s, pb_s)
            consumed = weighted_values(hp - 1, pa_s)
            scores_and_shifts(hp + 2, sb_s, shb_s)
            probabilities(sa_s, sha_s, pa_s, after=consumed)
            weighted_values(hp, pb_s)
            return carry

        lax.fori_loop(0, n_pairs // 2 - 1, two_pairs, 0)
        probabilities(sb_s, shb_s, pb_s)
        return weighted_values(n_pairs - 2, pa_s), weighted_values(n_pairs - 1, pb_s)

    buffers_read = (None, None)
    for j in range(n_query_blocks):
        query_rows = slice(j * blk, (j + 1) * blk)
        buffers_read = attend(first_block + j, query_rows, att_s.at[j], buffers_read)
        att = att_s[j].T.astype(BF16)
        y = _dot(att, w_out_ref[...])
        o_ref[0, query_rows, :] = _residual_norm(x_ref[0, query_rows, :], y, ln_g_ref[...], ln_b_ref[...])


def _moba_attn(x, first_block, n_query_blocks, q0, q1, k0, k1, vt, kmean, *params):
    bsz, t_len, d = x.shape
    blk = MOBA_BLOCK
    n_blocks = t_len // blk
    n_keys = (first_block + n_query_blocks) * blk
    assert first_block % n_query_blocks == 0
    query_rows = pl.BlockSpec((1, n_query_blocks * blk, d), lambda b: (b, first_block // n_query_blocks, 0))
    visible_keys = pl.BlockSpec((1, n_keys, d), lambda b: (b, 0, 0))
    return pl.pallas_call(
        functools.partial(_moba_attn_kernel, first_block=first_block, n_query_blocks=n_query_blocks),
        grid=(bsz,),
        in_specs=[
            query_rows, query_rows, query_rows, visible_keys, visible_keys,
            pl.BlockSpec((1, d, n_keys), lambda b: (b, 0, 0)),
            pl.BlockSpec((1, n_blocks, d), lambda b: (b, 0, 0)),
        ] + [_resident(p) for p in params],
        out_specs=query_rows,
        out_shape=jax.ShapeDtypeStruct(x.shape, F32),
        input_output_aliases={0: 0},
        scratch_shapes=[
            pltpu.VMEM((n_keys, 2 * blk), F32),
            pltpu.VMEM((n_keys, 2 * blk), F32),
            pltpu.VMEM((2, n_blocks, blk), F32),
            pltpu.VMEM((2, n_blocks, blk), F32),
            pltpu.VMEM((2, n_keys, blk), BF16),
            pltpu.VMEM((2, n_keys, blk), BF16),
            pltpu.VMEM((n_query_blocks, d, blk), F32),
        ],
        compiler_params=_params("arbitrary"),
        name=f"moba_attn_from_block_{first_block}",
    )(x, q0, q1, k0, k1, vt, kmean, *_stacks(*params))


def kernel(x, mem, ln_g, ln_b, x_wq, x_wkv, x_wo, ffn_w_in, ffn_w_out, ev_w_in, ev_w_out, a_ws, a_bs,
           a_ln_g, a_ln_b, b_norm_g, hgrn_lb_logits, od_w_qkv, od_w_out):
    bsz, t_len, d = x.shape
    assert d == D_MODEL and t_len % ROW_TILE == 0 and t_len % MOBA_BLOCK == 0
    assert ROW_TILE % MOBA_BLOCK == 0 and t_len // MOBA_BLOCK > 1
    n = bsz * t_len
    mem2d = mem.reshape(bsz * mem.shape[1], d)

    def rows_of(v):
        return v.reshape(-1, 1, v.shape[-1])

    def seq(v):
        return v.reshape(bsz, t_len, d)

    ffn_w_in, ffn_w_out = ffn_w_in.astype(BF16), ffn_w_out.astype(BF16)
    od_w_out = od_w_out.astype(BF16)
    ln_g, ln_b = rows_of(ln_g), rows_of(ln_b)
    a_ln_g, a_ln_b, b_norm_g = rows_of(a_ln_g), rows_of(a_ln_b), rows_of(b_norm_g)
    a_bs_t = jnp.swapaxes(a_bs, 1, 2)
    lb_logits = hgrn_lb_logits[None]
    alibi_fill = _alibi_query_fill()[None]

    for layer in range(DEPTH):
        j = layer // 2
        norm = [(ln_g, 3 * layer), (ln_b, 3 * layer)]
        if layer % 2 == 0:
            x = _even_mixer(
                x, j, (ev_w_in, j), (ev_w_out, j), (a_ws, j), (a_bs_t, j), (a_ln_g, j), (a_ln_b, j),
                (b_norm_g, j), (lb_logits, 0), *norm)
        else:
            q0, q1, k0, k1, vt, kmean = _moba_qkv(x.reshape(n, d), (od_w_qkv, j), (alibi_fill, 0), bsz, t_len)
            attn_args = (seq(q0), seq(q1), seq(k0), seq(k1), vt,
                         kmean.reshape(bsz, t_len // MOBA_BLOCK, d), (od_w_out, j), *norm)
            n_blocks = t_len // MOBA_BLOCK
            for first_block in range(0, n_blocks, MOBA_BLOCKS_PER_CALL):
                x = _moba_attn(x, first_block, min(MOBA_BLOCKS_PER_CALL, n_blocks - first_block), *attn_args)
        kv = _mem_kv(mem2d, (x_wkv, layer)).reshape(bsz, mem.shape[1], 2 * d)
        x = _cross_attn(x, kv, (x_wq, layer), (x_wo, layer), (ln_g, 3 * layer + 1), (ln_b, 3 * layer + 1))
        x = _ffn(x.reshape(n, d), (ffn_w_in, layer), (ffn_w_out, layer),
                 (ln_g, 3 * layer + 2), (ln_b, 3 * layer + 2)).reshape(bsz, t_len, d)
    return x
```

```python
import functools
import math

import jax
import jax.numpy as jnp
from jax import lax
from jax.experimental import pallas as pl
from jax.experimental.pallas import tpu as pltpu

D_MODEL = 1024
DEPTH = 2
ALPHA = (2.0 * DEPTH) ** 0.25
LN_EPS = 1e-5

GMLP_WIDTH = D_MODEL // 2
GMLP_GROUPS = 4
GMLP_GDIM = GMLP_WIDTH // GMLP_GROUPS
GMLP_CHUNK = 128
HGRN_WIDTH = D_MODEL // 2
HGRN_HEADS = 4
HGRN_DK = HGRN_WIDTH // HGRN_HEADS
HGRN_CHUNK = 64
EVEN_IN_WIDTH = 2 * GMLP_WIDTH + 4 * HGRN_WIDTH

MOBA_HEADS = 16
MOBA_HDIM = D_MODEL // MOBA_HEADS
MOBA_BLOCK = 256
MOBA_TOPK = 3
MOBA_PAIR = 2 * MOBA_HDIM
MOBA_BLOCKS_PER_CALL = 2

MEM_HEADS = 4
MEM_HDIM = D_MODEL // MEM_HEADS

D_FF = int(math.ceil(8 * D_MODEL / 3 / 256)) * 256
FFN_CHUNK = 256

ROW_TILE = 1024
MEM_ROW_TILE = 512
EVEN_ROW_TILE = 512
FFN_NORM_SLICES = 8
V7X_VMEM_LIMIT = 56 * 1024 * 1024

MASKED = -1e30

BF16 = jnp.bfloat16
F32 = jnp.float32
BF16_SUBLANES = 16


def _dot(a, b):
    return jnp.dot(a, b, preferred_element_type=F32)


def _dot_nt(a, b):
    return lax.dot_general(a, b, (((1,), (1,)), ((), ())), preferred_element_type=F32)


def _dot_tn(a, b):
    return lax.dot_general(a, b, (((0,), (0,)), ((), ())), preferred_element_type=F32)


def _layer_norm(z, g, b):
    mu = jnp.mean(z, axis=-1, keepdims=True)
    zc = z - mu
    var = jnp.mean(zc * zc, axis=-1, keepdims=True)
    return zc * lax.rsqrt(var + LN_EPS) * g + b


def _zero_after(token):
    bits = lax.bitcast_convert_type(token, jnp.int32)
    return lax.shift_right_logical(lax.shift_right_logical(bits, 16), 16).astype(F32)


def _residual_norm(x, y, g, b, after=None):
    alpha = ALPHA if after is None else ALPHA + _zero_after(after)
    return _layer_norm(x * alpha + y, g, b)


def _resident(picked):
    stack, index = picked
    return pl.BlockSpec((None,) + stack.shape[1:], lambda *_: (index,) + (0,) * (stack.ndim - 1),
                        pipeline_mode=pl.Buffered(1))


def _stacks(*picked):
    return [stack for stack, _ in picked]


def _bf16_copy(picked):
    stack, _ = picked
    return pltpu.VMEM(stack.shape[1:], BF16)


def _cast_weights_once(n_grid_axes, *pairs):
    first = pl.program_id(0) == 0
    for axis in range(1, n_grid_axes):
        first = jnp.logical_and(first, pl.program_id(axis) == 0)

    @pl.when(first)
    def _():
        for src, dst in pairs:
            dst[...] = src[...].astype(BF16)


def _params(*semantics):
    return pltpu.CompilerParams(dimension_semantics=semantics, vmem_limit_bytes=V7X_VMEM_LIMIT)


def _even_mixer_kernel(x_ref, w_in_f32_ref, w_out_f32_ref, ws_ref, bs_ref, aln_g_ref, aln_b_ref, bnorm_ref,
                       lb_logits_ref, ln_g_ref, ln_b_ref, o_ref,
                       q_s, f_s, i_s, g_s, y_s, state_s, w_in_ref, w_out_ref, *, lb_index):
    tm = x_ref.shape[1]
    _cast_weights_once(2, (w_in_f32_ref, w_in_ref), (w_out_f32_ref, w_out_ref))
    x = x_ref[0]
    xb = x.astype(BF16)

    @pl.when(pl.program_id(1) == 0)
    def _():
        state_s[...] = jnp.zeros_like(state_s)

    base = 2 * GMLP_WIDTH
    u_pre = _dot(xb, w_in_ref[:, 0:GMLP_WIDTH])
    v_pre = _dot(xb, w_in_ref[:, GMLP_WIDTH:2 * GMLP_WIDTH])
    q_pre = _dot(xb, w_in_ref[:, base:base + HGRN_WIDTH])
    q_s[...] = q_pre
    f_pre = _dot(xb, w_in_ref[:, base + HGRN_WIDTH:base + 2 * HGRN_WIDTH])
    f_s[...] = f_pre
    i_s[...] = jax.nn.silu(_dot(xb, w_in_ref[:, base + 2 * HGRN_WIDTH:base + 3 * HGRN_WIDTH]))
    g_s[...] = jax.nn.silu(_dot(xb, w_in_ref[:, base + 3 * HGRN_WIDTH:base + 4 * HGRN_WIDTH]))

    u = jax.nn.gelu(u_pre + _zero_after(q_pre[0:1, 0:1]))
    v = jax.nn.gelu(v_pre + _zero_after(f_pre[0:1, 0:1]))
    row = lax.broadcasted_iota(jnp.int32, (GMLP_CHUNK, GMLP_CHUNK), 0)
    col = lax.broadcasted_iota(jnp.int32, (GMLP_CHUNK, GMLP_CHUNK), 1)
    for g in range(GMLP_GROUPS):
        lanes = slice(g * GMLP_GDIM, (g + 1) * GMLP_GDIM)
        vn = _layer_norm(v[:, lanes], aln_g_ref[:, lanes], aln_b_ref[:, lanes]).astype(BF16)
        wg = jnp.where(col <= row, ws_ref[g], 0.0).astype(BF16)
        bias = bs_ref[:, g:g + 1]
        for c in range(tm // GMLP_CHUNK):
            rows = slice(c * GMLP_CHUNK, (c + 1) * GMLP_CHUNK)
            s = _dot(wg, vn[rows]) + bias
            y_s[rows, lanes] = (u[rows, lanes] * s).astype(BF16)

    n_chunks = tm // HGRN_CHUNK
    logits = lb_logits_ref[...]
    e = jnp.exp(logits - jnp.max(logits, axis=0, keepdims=True))
    lb = jnp.sum(e[0:lb_index + 1], axis=0, keepdims=True) / jnp.sum(e, axis=0, keepdims=True)

    crow =lax.broadcasted_iota(jnp.int32, (HGRN_CHUNK, HGRN_CHUNK), 0)
    ccol = lax.broadcasted_iota(jnp.int32, (HGRN_CHUNK, HGRN_CHUNK), 1)
    causal = ccol <= crow
    tril_ones = jnp.where(causal, 1.0, 0.0).astype(BF16)
    norm_g = bnorm_ref[...]

    for c in range(n_chunks):
        rows = slice(c * HGRN_CHUNK, (c + 1) * HGRN_CHUNK)
        f = lb + (1.0 - lb) * jax.nn.sigmoid(f_s[rows, :])
        log_f = jnp.log(f)
        log_f_hi = log_f.astype(BF16)
        log_f_lo = (log_f - log_f_hi.astype(F32)).astype(BF16)
        cum = _dot(tril_ones, log_f_hi) + _dot(tril_ones, log_f_lo)
        chunk_decay = jnp.exp(cum[HGRN_CHUNK - 1:HGRN_CHUNK, :])
        k_back = (1.0 - f) * jnp.exp(-cum)
        q_dec = (q_s[rows, :] * jnp.exp(cum)).astype(BF16)
        k_dec = k_back.astype(BF16)
        k_tail = (k_back * chunk_decay).astype(BF16)
        val = i_s[rows, :].astype(BF16)
        gate = g_s[rows, :]
        for h in range(HGRN_HEADS):
            lanes = slice(h * HGRN_DK, (h + 1) * HGRN_DK)
            attn = jnp.where(causal, _dot_nt(q_dec[:, lanes], k_dec[:, lanes]), 0.0).astype(BF16)
            state_t = state_s[h]
            o = _dot(attn, val[:, lanes]) + _dot_nt(q_dec[:, lanes], state_t.astype(BF16))
            state_s[h] = state_t * chunk_decay[:, lanes] + _dot_tn(val[:, lanes], k_tail[:, lanes])
            rms = lax.rsqrt(jnp.mean(o * o, axis=-1, keepdims=True) + LN_EPS)
            y_b = o * rms * norm_g[:, lanes] * gate[:, lanes]
            y_s[rows, GMLP_WIDTH + h * HGRN_DK:GMLP_WIDTH + (h + 1) * HGRN_DK] = y_b.astype(BF16)

    for rows in (slice(0, tm // 2), slice(tm // 2, tm)):
        y = _dot(y_s[rows, :], w_out_ref[...])
        o_ref[0, rows, :] = _residual_norm(x_ref[0, rows, :], y, ln_g_ref[...], ln_b_ref[...])


def _even_mixer(x, lb_index, *params):
    bsz, t_len, d = x.shape
    tm = EVEN_ROW_TILE
    assert t_len % tm == 0
    kern = functools.partial(_even_mixer_kernel, lb_index=lb_index)
    return pl.pallas_call(
        kern,
        grid=(bsz, t_len // tm),
        in_specs=[pl.BlockSpec((1, tm, d), lambda b, t: (b, t, 0))] + [_resident(p) for p in params],
        out_specs=pl.BlockSpec((1, tm, d), lambda b, t: (b, t, 0)),
        out_shape=jax.ShapeDtypeStruct(x.shape, F32),
        scratch_shapes=[
            pltpu.VMEM((tm, HGRN_WIDTH), F32), pltpu.VMEM((tm, HGRN_WIDTH), F32),
            pltpu.VMEM((tm, HGRN_WIDTH), F32), pltpu.VMEM((tm, HGRN_WIDTH), F32),
            pltpu.VMEM((tm, GMLP_WIDTH + HGRN_WIDTH), BF16),
            pltpu.VMEM((HGRN_HEADS, HGRN_DK, HGRN_DK), F32),
            _bf16_copy(params[0]), _bf16_copy(params[1]),
        ],
        compiler_params=_params("arbitrary", "arbitrary"),
        name="even_mixer",
    )(x, *_stacks(*params))


def _mem_kv_kernel(mem_ref, w_f32_ref, o_ref, w_ref):
    _cast_weights_once(1, (w_f32_ref, w_ref))
    o_ref[...] = _dot(mem_ref[...].astype(BF16), w_ref[...]).astype(BF16)


def _mem_kv(mem2d, w_kv):
    n, d = mem2d.shape
    tm = MEM_ROW_TILE
    assert n % tm == 0
    width = w_kv[0].shape[-1]
    return pl.pallas_call(
        _mem_kv_kernel,
        grid=(n // tm,),
        in_specs=[pl.BlockSpec((tm, d), lambda i: (i, 0)), _resident(w_kv)],
        out_specs=pl.BlockSpec((tm, width), lambda i: (i, 0)),
        out_shape=jax.ShapeDtypeStruct((n, width), BF16),
        scratch_shapes=[_bf16_copy(w_kv)],
        compiler_params=_params("arbitrary"),
        name="mem_kv",
    )(mem2d, *_stacks(w_kv))


def _cross_attn_kernel(x_ref, kv_ref, wq_f32_ref, wo_f32_ref, ln_g_ref, ln_b_ref, o_ref, att_s, y_s,
                       wq_ref, wo_ref):
    _cast_weights_once(2, (wq_f32_ref, wq_ref), (wo_f32_ref, wo_ref))
    half = x_ref.shape[1] // 2
    slice_rows = half // MEM_HEADS

    def attend(rows):
        q = (_dot(x_ref[0, rows, :].astype(BF16), wq_ref[...]) * (MEM_HDIM ** -0.5)).astype(BF16)
        done = []
        for h in range(MEM_HEADS):
            lanes = slice(h * MEM_HDIM, (h + 1) * MEM_HDIM)
            k_h = kv_ref[0, :, h * MEM_HDIM:(h + 1) * MEM_HDIM]
            v_h = kv_ref[0, :, D_MODEL + h * MEM_HDIM:D_MODEL + (h + 1) * MEM_HDIM]
            s = _dot_nt(q[:, lanes], k_h)
            p = jnp.exp(s - jnp.max(s, axis=-1, keepdims=True))
            denom = jnp.sum(p, axis=-1, keepdims=True)
            weighted = _dot(p.astype(BF16), v_h)
            att_s[rows, lanes] = (weighted / denom).astype(BF16)
            done.append(weighted[0:1, 0:1])
        y_s[rows, :] = _dot(att_s[rows, :], wo_ref[...])
        return done

    def finish(rows, after=None):
        o_ref[0, rows, :] = _residual_norm(x_ref[0, rows, :], y_s[rows, :], ln_g_ref[...], ln_b_ref[...], after)

    attend(slice(0, half))
    done = attend(slice(half, 2 * half))
    for h in range(MEM_HEADS):
        finish(slice(h * slice_rows, (h + 1) * slice_rows), after=done[h])
    finish(slice(half, 2 * half))


def _cross_attn(x, kv, *params):
    bsz, t_len, d = x.shape
    tm = ROW_TILE
    return pl.pallas_call(
        _cross_attn_kernel,
        grid=(bsz, t_len // tm),
        in_specs=[
            pl.BlockSpec((1, tm, d), lambda b, t: (b, t, 0)),
            pl.BlockSpec((1,) + kv.shape[1:], lambda b, t: (b, 0, 0)),
        ] + [_resident(p) for p in params],
        out_specs=pl.BlockSpec((1, tm, d), lambda b, t: (b, t, 0)),
        out_shape=jax.ShapeDtypeStruct(x.shape, F32),
        scratch_shapes=[pltpu.VMEM((tm, d), BF16), pltpu.VMEM((tm, d), F32),
                        _bf16_copy(params[0]), _bf16_copy(params[1])],
        compiler_params=_params("arbitrary", "arbitrary"),
        name="cross_attn",
    )(x, kv, *_stacks(*params))


def _ffn_kernel(x_ref, w_in_ref, w_out_ref, ln_g_ref, ln_b_ref, o_ref, acc_s):
    half = x_ref.shape[0] // 2
    n_chunks = D_FF // FFN_CHUNK
    slice_rows = half // FFN_NORM_SLICES

    def chunk(rows, xb, c):
        cols = slice(c * FFN_CHUNK, (c + 1) * FFN_CHUNK)
        gate = _dot(xb, w_in_ref[:, cols])
        up = _dot(xb, w_in_ref[:, D_FF + c * FFN_CHUNK:D_FF + (c + 1) * FFN_CHUNK])
        act = (jax.nn.silu(gate) * up).astype(BF16)
        part = _dot(act, w_out_ref[cols, :])
        if c == 0:
            acc_s[rows, :] = part
        else:
            acc_s[rows, :] += part
        return part[0:1, :]

    def finish(rows, after=None):
        o_ref[rows, :] = _residual_norm(x_ref[rows, :], acc_s[rows, :], ln_g_ref[...], ln_b_ref[...], after)

    first, second = slice(0, half), slice(half, 2 * half)
    xb_first = x_ref[first, :].astype(BF16)
    xb_second = x_ref[second, :].astype(BF16)
    for c in range(n_chunks):
        chunk(first, xb_first, c)
    for c in range(n_chunks):
        done = chunk(second, xb_second, c)
        if c < FFN_NORM_SLICES:
            finish(slice(c * slice_rows, (c + 1) * slice_rows), after=done)
    finish(second)


def _ffn(x2d, *params):
    n, d = x2d.shape
    tm = ROW_TILE
    assert n % tm == 0
    return pl.pallas_call(
        _ffn_kernel,
        grid=(n // tm,),
        in_specs=[pl.BlockSpec((tm, d), lambda i: (i, 0))] + [_resident(p) for p in params],
        out_specs=pl.BlockSpec((tm, d), lambda i: (i, 0)),
        out_shape=jax.ShapeDtypeStruct((n, d), F32),
        scratch_shapes=[pltpu.VMEM((tm, d), F32)],
        compiler_params=_params("parallel"),
        name="ffn",
    )(x2d, *_stacks(*params))


LOG2_E = 1.4426950408889634
ALIBI_LANES = 3


def _moba_qkv_kernel(x_ref, w_f32_ref, fill_ref, q0_ref, q1_ref, k0_ref, k1_ref, vt_ref, kmean_ref, w_ref,
                     *, tiles_per_seq):
    _cast_weights_once(1, (w_f32_ref, w_ref))
    tm = x_ref.shape[0]
    xb = x_ref[...].astype(BF16)
    transposed = (((0,), (1,)), ((), ()))
    qt = lax.dot_general(w_ref[:, 0:D_MODEL], xb, transposed, preferred_element_type=F32)
    qt = qt * (MOBA_HDIM ** -0.5 * LOG2_E)
    k = _dot(xb, w_ref[:, D_MODEL:2 * D_MODEL])
    for i in range(tm // MOBA_BLOCK):
        kmean_ref[i] = jnp.mean(k[i * MOBA_BLOCK:(i + 1) * MOBA_BLOCK], axis=0, keepdims=True)
    vt_ref[0] = lax.dot_general(w_ref[:, 2 * D_MODEL:3 * D_MODEL], xb, transposed,
                                preferred_element_type=F32).astype(BF16)

    lane = lax.broadcasted_iota(jnp.int32, (tm, MOBA_PAIR), 1)
    pos = lax.broadcasted_iota(jnp.int32, (tm, MOBA_PAIR), 0) + (pl.program_id(0) % tiles_per_seq) * tm
    pos_block = ((pos // MOBA_BLOCK) * MOBA_BLOCK).astype(F32)
    pos_offset = (pos % MOBA_BLOCK).astype(F32)
    feature = lax.broadcasted_iota(jnp.int32, (MOBA_PAIR, tm), 0)
    for e, (q_ref, k_ref) in enumerate(((q0_ref, k0_ref), (q1_ref, k1_ref))):
        own = (lane < MOBA_HDIM) if e == 0 else (lane >= MOBA_HDIM)
        own_feature = (feature < MOBA_HDIM) if e == 0 else (feature >= MOBA_HDIM)
        partner = MOBA_HDIM * (1 - e)
        key_fill = jnp.where((lane >= partner) & (lane < partner + ALIBI_LANES), pos_block,
                             jnp.where((lane >= partner + ALIBI_LANES) & (lane < partner + 2 * ALIBI_LANES),
                                       pos_offset, 0.0))
        for p in range(MOBA_HEADS // 2):
            slab = slice(p * MOBA_PAIR, (p + 1) * MOBA_PAIR)
            q_fill = jnp.concatenate([fill_ref[e, slab, :]] * (tm // fill_ref.shape[-1]), axis=1)
            q_ref[0, slab, :] = jnp.where(own_feature, qt[slab, :], q_fill).astype(BF16)
            k_ref[:, slab] = jnp.where(own, k[:, slab], key_fill).astype(BF16)


def _moba_qkv(x2d, w_qkv, fill, bsz, t_len):
    n, d = x2d.shape
    tm = ROW_TILE
    tiles_per_seq = t_len // tm
    rows = pl.BlockSpec((tm, d), lambda i: (i, 0))
    feature_major = pl.BlockSpec((1, d, tm), lambda i: (i // tiles_per_seq, 0, i % tiles_per_seq))
    return pl.pallas_call(
        functools.partial(_moba_qkv_kernel, tiles_per_seq=tiles_per_seq),
        grid=(n // tm,),
        in_specs=[rows, _resident(w_qkv), _resident(fill)],
        out_specs=[
            feature_major, feature_major, rows, rows, feature_major,
            pl.BlockSpec((tm // MOBA_BLOCK, 1, d), lambda i: (i, 0, 0)),
        ],
        out_shape=[
            jax.ShapeDtypeStruct((bsz, d, t_len), BF16), jax.ShapeDtypeStruct((bsz, d, t_len), BF16),
            jax.ShapeDtypeStruct((n, d), BF16), jax.ShapeDtypeStruct((n, d), BF16),
            jax.ShapeDtypeStruct((bsz, d, t_len), BF16),
            jax.ShapeDtypeStruct((n // MOBA_BLOCK, 1, d), F32),
        ],
        scratch_shapes=[_bf16_copy(w_qkv)],
        compiler_params=_params("arbitrary"),
        name="moba_qkv",
    )(x2d, *_stacks(w_qkv, fill))


def _alibi_query_fill():
    slopes = jnp.asarray([2.0 ** (-8.0 * (h + 1) / MOBA_HEADS) for h in range(MOBA_HEADS)], F32) * LOG2_E
    pieces = []
    rest = slopes
    for _ in range(ALIBI_LANES):
        piece = rest.astype(BF16).astype(F32)
        pieces.append(piece)
        rest = rest - piece
    pieces = jnp.stack(pieces + pieces, axis=1)
    fill = jnp.zeros((2, MOBA_HEADS // 2, MOBA_PAIR), F32)
    for e in range(2):
        partner = MOBA_HDIM * (1 - e)
        fill = fill.at[e, :, partner:partner + 2 * ALIBI_LANES].set(pieces[e::2])
    return jnp.broadcast_to(fill.reshape(2, D_MODEL, 1), (2, D_MODEL, MOBA_PAIR))


def _moba_attn_kernel(x_ref, q0_ref, q1_ref, k0_ref, k1_ref, vt_ref, kmean_ref, w_out_ref, ln_g_ref, ln_b_ref,
                      o_ref, sa_s, sb_s, sha_s, shb_s, pa_s, pb_s, att_s, *, first_block, n_query_blocks):
    n_blocks = kmean_ref.shape[1]
    blk = MOBA_BLOCK
    q_refs = (q0_ref, q1_ref)
    k_refs = (k0_ref, k1_ref)
    blk_i = lax.broadcasted_iota(jnp.int32, (n_blocks, blk), 0)
    mean_lane = lax.broadcasted_iota(jnp.int32, (n_blocks, MOBA_PAIR), 1)

    def attend(n_past, query_rows, att_buf, buffers_read):
        n_keys = (n_past + 1) * blk
        select = n_past > MOBA_TOPK

        def pair_lanes(hp):
            return pl.ds(pl.multiple_of(hp * MOBA_PAIR, MOBA_PAIR), MOBA_PAIR)

        def scores_and_shifts(hp, s_buf, sh_buf, after=None):
            lanes = pair_lanes(hp)
            for e in range(2):
                cols = slice(e * blk, (e + 1) * blk)
                q_e = q_refs[e][0, lanes, query_rows]
                if after is not None:
                    q_e = (q_e.astype(F32) + _zero_after(after)).astype(BF16)
                if select:
                    own = (mean_lane < MOBA_HDIM) if e == 0 else (mean_lane >= MOBA_HDIM)
                    kmean = jnp.where(own, kmean_ref[0, :, lanes], 0.0)
                    kmean_hi = kmean.astype(BF16)
                    kmean_lo = (kmean - kmean_hi.astype(F32)).astype(BF16)
                    aff = _dot(kmean_hi, q_e) + _dot(kmean_lo, q_e)
                    rank = jnp.zeros((n_blocks, blk), F32)
                    for jp in range(n_past):
                        other = aff[jp:jp + 1, :]
                        beats = (other > aff) | ((other == aff) & (jp < blk_i))
                        rank = rank + jnp.where(beats, 1.0, 0.0)
                    chosen = rank < MOBA_TOPK
                m = None
                for j in range(n_past + 1):
                    rows = slice(j * blk, (j + 1) * blk)
                    t = _dot(k_refs[e][0, rows, lanes], q_e)
                    if j == n_past:
                        key_i = lax.broadcasted_iota(jnp.int32, (blk, blk), 0)
                        qry_i = lax.broadcasted_iota(jnp.int32, (blk, blk), 1)
                        t = jnp.where(key_i <= qry_i, t, MASKED)
                    s_buf[rows, cols] = t
                    m_j = jnp.max(t, axis=0, keepdims=True)
                    if select and j < n_past:
                        m_j = jnp.where(chosen[j:j + 1, :], m_j, MASKED)
                    m = m_j if m is None else jnp.maximum(m, m_j)
                shifts = jnp.broadcast_to(m, (n_blocks, blk))
                if select:
                    shifts = jnp.where(chosen | (blk_i == n_past), shifts, -MASKED)
                sh_buf[e] = shifts

        def probabilities(s_buf, sh_buf, p_buf, after=None):
            hold = 0.0 if after is None else _zero_after(after)
            for e in range(2):
                cols = slice(e * blk, (e + 1) * blk)
                for j in range(n_past + 1):
                    rows = slice(j * blk, (j + 1) * blk)
                    shift = sh_buf[e, j:j + 1, :] + hold
                    p_buf[e, rows, :] = jnp.exp2(s_buf[rows, cols] - shift).astype(BF16)

        def weighted_values(hp, p_buf):
            for e in range(2):
                feat = pl.ds(pl.multiple_of(hp * MOBA_PAIR + e * MOBA_HDIM, MOBA_HDIM), MOBA_HDIM)
                ones = jnp.ones((BF16_SUBLANES, n_keys), BF16)
                values = jnp.concatenate([vt_ref[0, feat, 0:n_keys], ones], axis=0)
                acc = _dot(values, p_buf[e, 0:n_keys, :])
                denom = acc[MOBA_HDIM:MOBA_HDIM + 1]
                att_buf[feat, :] = acc[0:MOBA_HDIM] / denom
            return denom[:, 0:1]

        n_pairs = MOBA_HEADS // 2
        read_a, read_b = buffers_read
        scores_and_shifts(0, sa_s, sha_s)
        scores_and_shifts(1, sb_s, shb_s, after=read_b)
        probabilities(sa_s, sha_s, pa_s, after=read_a)

        def two_pairs(i, carry):
            hp = 2 * i + 1
            scores_and_shifts(hp + 1, sa_s, sha_s)
            probabilities(sb_s, shb_s, pb_s)
            consumed = weighted_values(hp - 1, pa_s)
            scores_and_shifts(hp + 2, sb_s, shb_s)
            probabilities(sa_s, sha_s, pa_s, after=consumed)
            weighted_values(hp, pb_s)
            return carry

        lax.fori_loop(0, n_pairs // 2 - 1, two_pairs, 0)
        probabilities(sb_s, shb_s, pb_s)
        return weighted_values(n_pairs - 2, pa_s), weighted_values(n_pairs - 1, pb_s)

    buffers_read = (None, None)
    for j in range(n_query_blocks):
        query_rows = slice(j * blk, (j + 1) * blk)
        buffers_read = attend(first_block + j, query_rows, att_s.at[j], buffers_read)
        att = att_s[j].T.astype(BF16)
        y = _dot(att, w_out_ref[...])
        o_ref[0, query_rows, :] = _residual_norm(x_ref[0, query_rows, :], y, ln_g_ref[...], ln_b_ref[...])


def _moba_attn(x, first_block, n_query_blocks, q0, q1, k0, k1, vt, kmean, *params):
    bsz, t_len, d = x.shape
    blk = MOBA_BLOCK
    n_blocks = t_len // blk
    n_keys = (first_block + n_query_blocks) * blk
    assert first_block % n_query_blocks == 0
    query_rows = pl.BlockSpec((1, n_query_blocks * blk, d), lambda b: (b, first_block // n_query_blocks, 0))
    query_cols = pl.BlockSpec((1, d, n_query_blocks * blk), lambda b: (b, 0, first_block // n_query_blocks))
    visible_keys = pl.BlockSpec((1, n_keys, d), lambda b: (b, 0, 0))
    return pl.pallas_call(
        functools.partial(_moba_attn_kernel, first_block=first_block, n_query_blocks=n_query_blocks),
        grid=(bsz,),
        in_specs=[
            query_rows, query_cols, query_cols, visible_keys, visible_keys,
            pl.BlockSpec((1, d, n_keys), lambda b: (b, 0, 0)),
            pl.BlockSpec((1, n_blocks, d), lambda b: (b, 0, 0)),
        ] + [_resident(p) for p in params],
        out_specs=query_rows,
        out_shape=jax.ShapeDtypeStruct(x.shape, F32),
        input_output_aliases={0: 0},
        scratch_shapes=[
            pltpu.VMEM((n_keys, 2 * blk), F32),
            pltpu.VMEM((n_keys, 2 * blk), F32),
            pltpu.VMEM((2, n_blocks, blk), F32),
            pltpu.VMEM((2, n_blocks, blk), F32),
            pltpu.VMEM((2, n_keys, blk), BF16),
            pltpu.VMEM((2, n_keys, blk), BF16),
            pltpu.VMEM((n_query_blocks, d, blk), F32),
        ],
        compiler_params=_params("arbitrary"),
        name=f"moba_attn_from_block_{first_block}",
    )(x, q0, q1, k0, k1, vt, kmean, *_stacks(*params))


def kernel(x, mem, ln_g, ln_b, x_wq, x_wkv, x_wo, ffn_w_in, ffn_w_out, ev_w_in, ev_w_out, a_ws, a_bs,
           a_ln_g, a_ln_b, b_norm_g, hgrn_lb_logits, od_w_qkv, od_w_out):
    bsz, t_len, d = x.shape
    assert d == D_MODEL and t_len % ROW_TILE == 0 and t_len % MOBA_BLOCK == 0
    assert ROW_TILE % MOBA_BLOCK == 0 and t_len // MOBA_BLOCK > 1
    n = bsz * t_len
    mem2d = mem.reshape(bsz * mem.shape[1], d)

    def rows_of(v):
        return v.reshape(-1, 1, v.shape[-1])

    def seq(v):
        return v.reshape(bsz, t_len, d)

    ffn_w_in, ffn_w_out = ffn_w_in.astype(BF16), ffn_w_out.astype(BF16)
    od_w_out = od_w_out.astype(BF16)
    ln_g, ln_b = rows_of(ln_g), rows_of(ln_b)
    a_ln_g, a_ln_b, b_norm_g = rows_of(a_ln_g), rows_of(a_ln_b), rows_of(b_norm_g)
    a_bs_t = jnp.swapaxes(a_bs, 1, 2)
    lb_logits = hgrn_lb_logits[None]
    alibi_fill = _alibi_query_fill()[None]

    for layer in range(DEPTH):
        j = layer // 2
        norm = [(ln_g, 3 * layer), (ln_b, 3 * layer)]
        if layer % 2 == 0:
            x = _even_mixer(
                x, j, (ev_w_in, j), (ev_w_out, j), (a_ws, j), (a_bs_t, j), (a_ln_g, j), (a_ln_b, j),
                (b_norm_g, j), (lb_logits, 0), *norm)
        else:
            q0, q1, k0, k1, vt, kmean = _moba_qkv(x.reshape(n, d), (od_w_qkv, j), (alibi_fill, 0), bsz, t_len)
            attn_args = (q0, q1, seq(k0), seq(k1), vt,
                         kmean.reshape(bsz, t_len // MOBA_BLOCK, d), (od_w_out, j), *norm)
            n_blocks = t_len // MOBA_BLOCK
            for first_block in range(0, n_blocks, MOBA_BLOCKS_PER_CALL):
                x = _moba_attn(x, first_block, min(MOBA_BLOCKS_PER_CALL, n_blocks - first_block), *attn_args)
        kv = _mem_kv(mem2d, (x_wkv, layer)).reshape(bsz, mem.shape[1], 2 * d)
        x = _cross_attn(x, kv, (x_wq, layer), (x_wo, layer), (ln_g, 3 * layer + 1), (ln_b, 3 * layer + 1))
        x = _ffn(x.reshape(n, d), (ffn_w_in, layer), (ffn_w_out, layer),
                 (ln_g, 3 * layer + 2), (ln_b, 3 * layer + 2)).reshape(bsz, t_len, d)
    return x
```

```python
import functools
import math

import jax
import jax.numpy as jnp
from jax import lax
from jax.experimental import pallas as pl
from jax.experimental.pallas import tpu as pltpu

D_MODEL = 1024
DEPTH = 2
ALPHA = (2.0 * DEPTH) ** 0.25
LN_EPS = 1e-5

GMLP_WIDTH = D_MODEL // 2
GMLP_GROUPS = 4
GMLP_GDIM = GMLP_WIDTH // GMLP_GROUPS
GMLP_CHUNK = 128
HGRN_WIDTH = D_MODEL // 2
HGRN_HEADS = 4
HGRN_DK = HGRN_WIDTH // HGRN_HEADS
HGRN_CHUNK = 64
EVEN_IN_WIDTH = 2 * GMLP_WIDTH + 4 * HGRN_WIDTH

MOBA_HEADS = 16
MOBA_HDIM = D_MODEL // MOBA_HEADS
MOBA_BLOCK = 256
MOBA_TOPK = 3
MOBA_PAIR = 2 * MOBA_HDIM
MOBA_BLOCKS_PER_CALL = 2

MEM_HEADS = 4
MEM_HDIM = D_MODEL // MEM_HEADS

D_FF = int(math.ceil(8 * D_MODEL / 3 / 256)) * 256
FFN_CHUNK = 256

ROW_TILE = 1024
MEM_ROW_TILE = 512
EVEN_ROW_TILE = 512
FFN_NORM_SLICES = 8
V7X_VMEM_LIMIT = 56 * 1024 * 1024

MASKED = -1e30

BF16 = jnp.bfloat16
F32 = jnp.float32
BF16_SUBLANES = 16


def _dot(a, b):
    return jnp.dot(a, b, preferred_element_type=F32)


def _dot_nt(a, b):
    return lax.dot_general(a, b, (((1,), (1,)), ((), ())), preferred_element_type=F32)


def _dot_tn(a, b):
    return lax.dot_general(a, b, (((0,), (0,)), ((), ())), preferred_element_type=F32)


def _layer_norm(z, g, b):
    mu = jnp.mean(z, axis=-1, keepdims=True)
    zc = z - mu
    var = jnp.mean(zc * zc, axis=-1, keepdims=True)
    return zc * lax.rsqrt(var + LN_EPS) * g + b


def _zero_after(token):
    bits = lax.bitcast_convert_type(token, jnp.int32)
    return lax.shift_right_logical(lax.shift_right_logical(bits, 16), 16).astype(F32)


def _residual_norm(x, y, g, b, after=None):
    alpha = ALPHA if after is None else ALPHA + _zero_after(after)
    return _layer_norm(x * alpha + y, g, b)


def _resident(picked):
    stack, index = picked
    return pl.BlockSpec((None,) + stack.shape[1:], lambda *_: (index,) + (0,) * (stack.ndim - 1),
                        pipeline_mode=pl.Buffered(1))


def _stacks(*picked):
    return [stack for stack, _ in picked]


def _bf16_copy(picked):
    stack, _ = picked
    return pltpu.VMEM(stack.shape[1:], BF16)


def _cast_weights_once(n_grid_axes, *pairs):
    first = pl.program_id(0) == 0
    for axis in range(1, n_grid_axes):
        first = jnp.logical_and(first, pl.program_id(axis) == 0)

    @pl.when(first)
    def _():
        for src, dst in pairs:
            dst[...] = src[...].astype(BF16)


def _params(*semantics):
    return pltpu.CompilerParams(dimension_semantics=semantics, vmem_limit_bytes=V7X_VMEM_LIMIT)


def _even_mixer_kernel(x_ref, w_in_f32_ref, w_out_f32_ref, ws_ref, bs_ref, aln_g_ref, aln_b_ref, bnorm_ref,
                       lb_logits_ref, ln_g_ref, ln_b_ref, o_ref,
                       q_s, f_s, i_s, g_s, y_s, state_s, w_in_ref, w_out_ref, *, lb_index):
    tm = x_ref.shape[1]
    _cast_weights_once(2, (w_in_f32_ref, w_in_ref), (w_out_f32_ref, w_out_ref))
    x = x_ref[0]
    xb = x.astype(BF16)

    @pl.when(pl.program_id(1) == 0)
    def _():
        state_s[...] = jnp.zeros_like(state_s)

    base = 2 * GMLP_WIDTH
    u_pre = _dot(xb, w_in_ref[:, 0:GMLP_WIDTH])
    v_pre = _dot(xb, w_in_ref[:, GMLP_WIDTH:2 * GMLP_WIDTH])
    q_pre = _dot(xb, w_in_ref[:, base:base + HGRN_WIDTH])
    q_s[...] = q_pre
    f_pre = _dot(xb, w_in_ref[:, base + HGRN_WIDTH:base + 2 * HGRN_WIDTH])
    f_s[...] = f_pre
    i_s[...] = jax.nn.silu(_dot(xb, w_in_ref[:, base + 2 * HGRN_WIDTH:base + 3 * HGRN_WIDTH]))
    g_s[...] = jax.nn.silu(_dot(xb, w_in_ref[:, base + 3 * HGRN_WIDTH:base + 4 * HGRN_WIDTH]))

    u = jax.nn.gelu(u_pre + _zero_after(q_pre[0:1, 0:1]))
    v = jax.nn.gelu(v_pre + _zero_after(f_pre[0:1, 0:1]))
    row = lax.broadcasted_iota(jnp.int32, (GMLP_CHUNK, GMLP_CHUNK), 0)
    col = lax.broadcasted_iota(jnp.int32, (GMLP_CHUNK, GMLP_CHUNK), 1)
    for g in range(GMLP_GROUPS):
        lanes = slice(g * GMLP_GDIM, (g + 1) * GMLP_GDIM)
        vn = _layer_norm(v[:, lanes], aln_g_ref[:, lanes], aln_b_ref[:, lanes]).astype(BF16)
        wg = jnp.where(col <= row, ws_ref[g], 0.0).astype(BF16)
        bias = bs_ref[:, g:g + 1]
        for c in range(tm // GMLP_CHUNK):
            rows = slice(c * GMLP_CHUNK, (c + 1) * GMLP_CHUNK)
            s = _dot(wg, vn[rows]) + bias
            y_s[rows, lanes] = (u[rows, lanes] * s).astype(BF16)

    n_chunks = tm // HGRN_CHUNK
    logits = lb_logits_ref[...]
    e = jnp.exp(logits - jnp.max(logits, axis=0, keepdims=True))
    lb = jnp.sum(e[0:lb_index + 1], axis=0, keepdims=True) / jnp.sum(e, axis=0, keepdims=True)

    crow =lax.broadcasted_iota(jnp.int32, (HGRN_CHUNK, HGRN_CHUNK), 0)
    ccol = lax.broadcasted_iota(jnp.int32, (HGRN_CHUNK, HGRN_CHUNK), 1)
    causal = ccol <= crow
    tril_ones = jnp.where(causal, 1.0, 0.0).astype(BF16)
    norm_g = bnorm_ref[...]

    for c in range(n_chunks):
        rows = slice(c * HGRN_CHUNK, (c + 1) * HGRN_CHUNK)
        f = lb + (1.0 - lb) * jax.nn.sigmoid(f_s[rows, :])
        log_f = jnp.log(f)
        log_f_hi = log_f.astype(BF16)
        log_f_lo = (log_f - log_f_hi.astype(F32)).astype(BF16)
        cum = _dot(tril_ones, log_f_hi) + _dot(tril_ones, log_f_lo)
        chunk_decay = jnp.exp(cum[HGRN_CHUNK - 1:HGRN_CHUNK, :])
        k_back = (1.0 - f) * jnp.exp(-cum)
        q_dec = (q_s[rows, :] * jnp.exp(cum)).astype(BF16)
        k_dec = k_back.astype(BF16)
        k_tail = (k_back * chunk_decay).astype(BF16)
        val = i_s[rows, :].astype(BF16)
        gate = g_s[rows, :]
        for h in range(HGRN_HEADS):
            lanes = slice(h * HGRN_DK, (h + 1) * HGRN_DK)
            attn = jnp.where(causal, _dot_nt(q_dec[:, lanes], k_dec[:, lanes]), 0.0).astype(BF16)
            state_t = state_s[h]
            o = _dot(attn, val[:, lanes]) + _dot_nt(q_dec[:, lanes], state_t.astype(BF16))
            state_s[h] = state_t * chunk_decay[:, lanes] + _dot_tn(val[:, lanes], k_tail[:, lanes])
            rms = lax.rsqrt(jnp.mean(o * o, axis=-1, keepdims=True) + LN_EPS)
            y_b = o * rms * norm_g[:, lanes] * gate[:, lanes]
            y_s[rows, GMLP_WIDTH + h * HGRN_DK:GMLP_WIDTH + (h + 1) * HGRN_DK] = y_b.astype(BF16)

    for rows in (slice(0, tm // 2), slice(tm // 2, tm)):
        y = _dot(y_s[rows, :], w_out_ref[...])
        o_ref[0, rows, :] = _residual_norm(x_ref[0, rows, :], y, ln_g_ref[...], ln_b_ref[...])


def _even_mixer(x, lb_index, *params):
    bsz, t_len, d = x.shape
    tm = EVEN_ROW_TILE
    assert t_len % tm == 0
    kern = functools.partial(_even_mixer_kernel, lb_index=lb_index)
    return pl.pallas_call(
        kern,
        grid=(bsz, t_len // tm),
        in_specs=[pl.BlockSpec((1, tm, d), lambda b, t: (b, t, 0))] + [_resident(p) for p in params],
        out_specs=pl.BlockSpec((1, tm, d), lambda b, t: (b, t, 0)),
        out_shape=jax.ShapeDtypeStruct(x.shape, F32),
        scratch_shapes=[
            pltpu.VMEM((tm, HGRN_WIDTH), F32), pltpu.VMEM((tm, HGRN_WIDTH), F32),
            pltpu.VMEM((tm, HGRN_WIDTH), F32), pltpu.VMEM((tm, HGRN_WIDTH), F32),
            pltpu.VMEM((tm, GMLP_WIDTH + HGRN_WIDTH), BF16),
            pltpu.VMEM((HGRN_HEADS, HGRN_DK, HGRN_DK), F32),
            _bf16_copy(params[0]), _bf16_copy(params[1]),
        ],
        compiler_params=_params("arbitrary", "arbitrary"),
        name="even_mixer",
    )(x, *_stacks(*params))


def _mem_kv_kernel(mem_ref, w_f32_ref, o_ref, w_ref):
    _cast_weights_once(1, (w_f32_ref, w_ref))
    o_ref[...] = _dot(mem_ref[...].astype(BF16), w_ref[...]).astype(BF16)


def _mem_kv(mem2d, w_kv):
    n, d = mem2d.shape
    tm = MEM_ROW_TILE
    assert n % tm == 0
    width = w_kv[0].shape[-1]
    return pl.pallas_call(
        _mem_kv_kernel,
        grid=(n // tm,),
        in_specs=[pl.BlockSpec((tm, d), lambda i: (i, 0)), _resident(w_kv)],
        out_specs=pl.BlockSpec((tm, width), lambda i: (i, 0)),
        out_shape=jax.ShapeDtypeStruct((n, width), BF16),
        scratch_shapes=[_bf16_copy(w_kv)],
        compiler_params=_params("arbitrary"),
        name="mem_kv",
    )(mem2d, *_stacks(w_kv))


def _cross_attn_kernel(x_ref, kv_ref, wq_f32_ref, wo_f32_ref, ln_g_ref, ln_b_ref, o_ref, att_s, y_s,
                       wq_ref, wo_ref):
    _cast_weights_once(2, (wq_f32_ref, wq_ref), (wo_f32_ref, wo_ref))
    half = x_ref.shape[1] // 2
    slice_rows = half // MEM_HEADS

    def attend(rows):
        q = (_dot(x_ref[0, rows, :].astype(BF16), wq_ref[...]) * (MEM_HDIM ** -0.5)).astype(BF16)
        done = []
        for h in range(MEM_HEADS):
            lanes = slice(h * MEM_HDIM, (h + 1) * MEM_HDIM)
            k_h = kv_ref[0, :, h * MEM_HDIM:(h + 1) * MEM_HDIM]
            v_h = kv_ref[0, :, D_MODEL + h * MEM_HDIM:D_MODEL + (h + 1) * MEM_HDIM]
            s = _dot_nt(q[:, lanes], k_h)
            p = jnp.exp(s - jnp.max(s, axis=-1, keepdims=True))
            denom = jnp.sum(p, axis=-1, keepdims=True)
            weighted = _dot(p.astype(BF16), v_h)
            att_s[rows, lanes] = (weighted / denom).astype(BF16)
            done.append(weighted[0:1, 0:1])
        y_s[rows, :] = _dot(att_s[rows, :], wo_ref[...])
        return done

    def finish(rows, after=None):
        o_ref[0, rows, :] = _residual_norm(x_ref[0, rows, :], y_s[rows, :], ln_g_ref[...], ln_b_ref[...], after)

    attend(slice(0, half))
    done = attend(slice(half, 2 * half))
    for h in range(MEM_HEADS):
        finish(slice(h * slice_rows, (h + 1) * slice_rows), after=done[h])
    finish(slice(half, 2 * half))


def _cross_attn(x, kv, *params):
    bsz, t_len, d = x.shape
    tm = ROW_TILE
    return pl.pallas_call(
        _cross_attn_kernel,
        grid=(bsz, t_len // tm),
        in_specs=[
            pl.BlockSpec((1, tm, d), lambda b, t: (b, t, 0)),
            pl.BlockSpec((1,) + kv.shape[1:], lambda b, t: (b, 0, 0)),
        ] + [_resident(p) for p in params],
        out_specs=pl.BlockSpec((1, tm, d), lambda b, t: (b, t, 0)),
        out_shape=jax.ShapeDtypeStruct(x.shape, F32),
        scratch_shapes=[pltpu.VMEM((tm, d), BF16), pltpu.VMEM((tm, d), F32),
                        _bf16_copy(params[0]), _bf16_copy(params[1])],
        compiler_params=_params("arbitrary", "arbitrary"),
        name="cross_attn",
    )(x, kv, *_stacks(*params))


def _ffn_kernel(x_ref, w_in_ref, w_out_ref, ln_g_ref, ln_b_ref, o_ref, acc_s):
    half = x_ref.shape[0] // 2
    n_chunks = D_FF // FFN_CHUNK
    slice_rows = half // FFN_NORM_SLICES

    def chunk(rows, xb, c):
        cols = slice(c * FFN_CHUNK, (c + 1) * FFN_CHUNK)
        gate = _dot(xb, w_in_ref[:, cols])
        up = _dot(xb, w_in_ref[:, D_FF + c * FFN_CHUNK:D_FF + (c + 1) * FFN_CHUNK])
        act = (jax.nn.silu(gate) * up).astype(BF16)
        part = _dot(act, w_out_ref[cols, :])
        if c == 0:
            acc_s[rows, :] = part
        else:
            acc_s[rows, :] += part
        return part[0:1, :]

    def finish(rows, after=None):
        o_ref[rows, :] = _residual_norm(x_ref[rows, :], acc_s[rows, :], ln_g_ref[...], ln_b_ref[...], after)

    first, second = slice(0, half), slice(half, 2 * half)
    xb_first = x_ref[first, :].astype(BF16)
    xb_second = x_ref[second, :].astype(BF16)
    for c in range(n_chunks):
        chunk(first, xb_first, c)
    for c in range(n_chunks):
        done = chunk(second, xb_second, c)
        if c < FFN_NORM_SLICES:
            finish(slice(c * slice_rows, (c + 1) * slice_rows), after=done)
    finish(second)


def _ffn(x2d, *params):
    n, d = x2d.shape
    tm = ROW_TILE
    assert n % tm == 0
    return pl.pallas_call(
        _ffn_kernel,
        grid=(n // tm,),
        in_specs=[pl.BlockSpec((tm, d), lambda i: (i, 0))] + [_resident(p) for p in params],
        out_specs=pl.BlockSpec((tm, d), lambda i: (i, 0)),
        out_shape=jax.ShapeDtypeStruct((n, d), F32),
        scratch_shapes=[pltpu.VMEM((tm, d), F32)],
        compiler_params=_params("parallel"),
        name="ffn",
    )(x2d, *_stacks(*params))


LOG2_E = 1.4426950408889634
ALIBI_LANES = 3


def _moba_qkv_kernel(x_ref, w_f32_ref, fill_ref, q0_ref, q1_ref, k0_ref, k1_ref, vt_ref, kmean_ref, w_ref,
                     *, tiles_per_seq):
    _cast_weights_once(1, (w_f32_ref, w_ref))
    tm = x_ref.shape[0]
    xb = x_ref[...].astype(BF16)
    transposed = (((0,), (1,)), ((), ()))
    qt = lax.dot_general(w_ref[:, 0:D_MODEL], xb, transposed, preferred_element_type=F32)
    qt = qt * (MOBA_HDIM ** -0.5 * LOG2_E)
    k = _dot(xb, w_ref[:, D_MODEL:2 * D_MODEL])
    for i in range(tm // MOBA_BLOCK):
        kmean_ref[i] = jnp.mean(k[i * MOBA_BLOCK:(i + 1) * MOBA_BLOCK], axis=0, keepdims=True)
    vt_ref[0] = lax.dot_general(w_ref[:, 2 * D_MODEL:3 * D_MODEL], xb, transposed,
                                preferred_element_type=F32).astype(BF16)

    lane = lax.broadcasted_iota(jnp.int32, (tm, MOBA_PAIR), 1)
    pos = lax.broadcasted_iota(jnp.int32, (tm, MOBA_PAIR), 0) + (pl.program_id(0) % tiles_per_seq) * tm
    pos_block = ((pos // MOBA_BLOCK) * MOBA_BLOCK).astype(F32)
    pos_offset = (pos % MOBA_BLOCK).astype(F32)
    feature = lax.broadcasted_iota(jnp.int32, (MOBA_PAIR, tm), 0)
    for e, (q_ref, k_ref) in enumerate(((q0_ref, k0_ref), (q1_ref, k1_ref))):
        own = (lane < MOBA_HDIM) if e == 0 else (lane >= MOBA_HDIM)
        own_feature = (feature < MOBA_HDIM) if e == 0 else (feature >= MOBA_HDIM)
        partner = MOBA_HDIM * (1 - e)
        key_fill = jnp.where((lane >= partner) & (lane < partner + ALIBI_LANES), pos_block,
                             jnp.where((lane >= partner + ALIBI_LANES) & (lane < partner + 2 * ALIBI_LANES),
                                       pos_offset, 0.0))
        for p in range(MOBA_HEADS // 2):
            slab = slice(p * MOBA_PAIR, (p + 1) * MOBA_PAIR)
            q_fill = jnp.concatenate([fill_ref[e, slab, :]] * (tm // fill_ref.shape[-1]), axis=1)
            q_ref[0, slab, :] = jnp.where(own_feature, qt[slab, :], q_fill).astype(BF16)
            k_ref[:, slab] = jnp.where(own, k[:, slab], key_fill).astype(BF16)


def _moba_qkv(x2d, w_qkv, fill, bsz, t_len):
    n, d = x2d.shape
    tm = ROW_TILE
    tiles_per_seq = t_len // tm
    rows = pl.BlockSpec((tm, d), lambda i: (i, 0))
    feature_major = pl.BlockSpec((1, d, tm), lambda i: (i // tiles_per_seq, 0, i % tiles_per_seq))
    return pl.pallas_call(
        functools.partial(_moba_qkv_kernel, tiles_per_seq=tiles_per_seq),
        grid=(n // tm,),
        in_specs=[rows, _resident(w_qkv), _resident(fill)],
        out_specs=[
            feature_major, feature_major, rows, rows, feature_major,
            pl.BlockSpec((tm // MOBA_BLOCK, 1, d), lambda i: (i, 0, 0)),
        ],
        out_shape=[
            jax.ShapeDtypeStruct((bsz, d, t_len), BF16), jax.ShapeDtypeStruct((bsz, d, t_len), BF16),
            jax.ShapeDtypeStruct((n, d), BF16), jax.ShapeDtypeStruct((n, d), BF16),
            jax.ShapeDtypeStruct((bsz, d, t_len), BF16),
            jax.ShapeDtypeStruct((n // MOBA_BLOCK, 1, d), F32),
        ],
        scratch_shapes=[_bf16_copy(w_qkv)],
        compiler_params=_params("arbitrary"),
        name="moba_qkv",
    )(x2d, *_stacks(w_qkv, fill))


def _alibi_query_fill():
    slopes = jnp.asarray([2.0 ** (-8.0 * (h + 1) / MOBA_HEADS) for h in range(MOBA_HEADS)], F32) * LOG2_E
    pieces = []
    rest = slopes
    for _ in range(ALIBI_LANES):
        piece = rest.astype(BF16).astype(F32)
        pieces.append(piece)
        rest = rest - piece
    pieces = jnp.stack(pieces + pieces, axis=1)
    fill = jnp.zeros((2, MOBA_HEADS // 2, MOBA_PAIR), F32)
    for e in range(2):
        partner = MOBA_HDIM * (1 - e)
        fill = fill.at[e, :, partner:partner + 2 * ALIBI_LANES].set(pieces[e::2])
    return jnp.broadcast_to(fill.reshape(2, D_MODEL, 1), (2, D_MODEL, MOBA_PAIR))


def _moba_attn_kernel(x_ref, q0_ref, q1_ref, k0_ref, k1_ref, vt_ref, kmean_ref, w_out_ref, ln_g_ref, ln_b_ref,
                      o_ref, sa_s, sb_s, sha_s, shb_s, pa_s, pb_s, att_s, *, first_block, n_query_blocks):
    n_blocks = kmean_ref.shape[1]
    blk = MOBA_BLOCK
    q_refs = (q0_ref, q1_ref)
    k_refs = (k0_ref, k1_ref)
    blk_i = lax.broadcasted_iota(jnp.int32, (n_blocks, blk), 0)
    mean_lane = lax.broadcasted_iota(jnp.int32, (n_blocks, MOBA_PAIR), 1)

    def attend(n_past, query_rows, att_buf, buffers_read):
        n_keys = (n_past + 1) * blk
        select = n_past > MOBA_TOPK

        def pair_lanes(hp):
            return pl.ds(pl.multiple_of(hp * MOBA_PAIR, MOBA_PAIR), MOBA_PAIR)

        def scores_and_shifts(hp, s_buf, sh_buf, after=None):
            lanes = pair_lanes(hp)
            for e in range(2):
                cols = slice(e * blk, (e + 1) * blk)
                q_e = q_refs[e][0, lanes, query_rows]
                if after is not None:
                    q_e = (q_e.astype(F32) + _zero_after(after)).astype(BF16)
                if select:
                    own = (mean_lane < MOBA_HDIM) if e == 0 else (mean_lane >= MOBA_HDIM)
                    kmean = jnp.where(own, kmean_ref[0, :, lanes], 0.0)
                    kmean_hi = kmean.astype(BF16)
                    kmean_lo = (kmean - kmean_hi.astype(F32)).astype(BF16)
                    aff = _dot(kmean_hi, q_e) + _dot(kmean_lo, q_e)
                    rank = jnp.zeros((n_blocks, blk), F32)
                    for jp in range(n_past):
                        other = aff[jp:jp + 1, :]
                        beats = (other > aff) | ((other == aff) & (jp < blk_i))
                        rank = rank + jnp.where(beats, 1.0, 0.0)
                    chosen = rank < MOBA_TOPK
                m = None
                t_all = _dot(k_refs[e][0, 0:n_keys, lanes], q_e)
                for j in range(n_past + 1):
                    rows = slice(j * blk, (j + 1) * blk)
                    t = t_all[rows]
                    if j == n_past:
                        key_i = lax.broadcasted_iota(jnp.int32, (blk, blk), 0)
                        qry_i = lax.broadcasted_iota(jnp.int32, (blk, blk), 1)
                        t = jnp.where(key_i <= qry_i, t, MASKED)
                    s_buf[rows, cols] = t
                    m_j = jnp.max(t, axis=0, keepdims=True)
                    if select and j < n_past:
                        m_j = jnp.where(chosen[j:j + 1, :], m_j, MASKED)
                    m = m_j if m is None else jnp.maximum(m, m_j)
                shifts = jnp.broadcast_to(m, (n_blocks, blk))
                if select:
                    shifts = jnp.where(chosen | (blk_i == n_past), shifts, -MASKED)
                sh_buf[e] = shifts

        def probabilities(s_buf, sh_buf, p_buf, after=None):
            hold = 0.0 if after is None else _zero_after(after)
            for e in range(2):
                cols = slice(e * blk, (e + 1) * blk)
                for j in range(n_past + 1):
                    rows = slice(j * blk, (j + 1) * blk)
                    shift = sh_buf[e, j:j + 1, :] + hold
                    p_buf[e, rows, :] = jnp.exp2(s_buf[rows, cols] - shift).astype(BF16)

        def weighted_values(hp, p_buf):
            for e in range(2):
                feat = pl.ds(pl.multiple_of(hp * MOBA_PAIR + e * MOBA_HDIM, MOBA_HDIM), MOBA_HDIM)
                ones = jnp.ones((BF16_SUBLANES, n_keys), BF16)
                values = jnp.concatenate([vt_ref[0, feat, 0:n_keys], ones], axis=0)
                acc = _dot(values, p_buf[e, 0:n_keys, :])
                denom = acc[MOBA_HDIM:MOBA_HDIM + 1]
                att_buf[feat, :] = acc[0:MOBA_HDIM] / denom
            return denom[:, 0:1]

        n_pairs = MOBA_HEADS // 2
        read_a, read_b = buffers_read
        scores_and_shifts(0, sa_s, sha_s)
        scores_and_shifts(1, sb_s, shb_s, after=read_b)
        probabilities(sa_s, sha_s, pa_s, after=read_a)

        def two_pairs(i, carry):
            hp = 2 * i + 1
            scores_and_shifts(hp + 1, sa_s, sha_s)
            probabilities(sb_s, shb_s, pb_s)
            consumed = weighted_values(hp - 1, pa_s)
            scores_and_shifts(hp + 2, sb_s, shb_s)
            probabilities(sa_s, sha_s, pa_s, after=consumed)
            weighted_values(hp, pb_s)
            return carry

        lax.fori_loop(0, n_pairs // 2 - 1, two_pairs, 0)
        probabilities(sb_s, shb_s, pb_s)
        return weighted_values(n_pairs - 2, pa_s), weighted_values(n_pairs - 1, pb_s)

    buffers_read = (None, None)
    for j in range(n_query_blocks):
        query_rows = slice(j * blk, (j + 1) * blk)
        buffers_read = attend(first_block + j, query_rows, att_s.at[j], buffers_read)
        att = att_s[j].T.astype(BF16)
        y = _dot(att, w_out_ref[...])
        o_ref[0, query_rows, :] = _residual_norm(x_ref[0, query_rows, :], y, ln_g_ref[...], ln_b_ref[...])


def _moba_attn(x, first_block, n_query_blocks, q0, q1, k0, k1, vt, kmean, *params):
    bsz, t_len, d = x.shape
    blk = MOBA_BLOCK
    n_blocks = t_len // blk
    n_keys = (first_block + n_query_blocks) * blk
    assert first_block % n_query_blocks == 0
    query_rows = pl.BlockSpec((1, n_query_blocks * blk, d), lambda b: (b, first_block // n_query_blocks, 0))
    query_cols = pl.BlockSpec((1, d, n_query_blocks * blk), lambda b: (b, 0, first_block // n_query_blocks))
    visible_keys = pl.BlockSpec((1, n_keys, d), lambda b: (b, 0, 0))
    return pl.pallas_call(
        functools.partial(_moba_attn_kernel, first_block=first_block, n_query_blocks=n_query_blocks),
        grid=(bsz,),
        in_specs=[
            query_rows, query_cols, query_cols, visible_keys, visible_keys,
            pl.BlockSpec((1, d, n_keys), lambda b: (b, 0, 0)),
            pl.BlockSpec((1, n_blocks, d), lambda b: (b, 0, 0)),
        ] + [_resident(p) for p in params],
        out_specs=query_rows,
        out_shape=jax.ShapeDtypeStruct(x.shape, F32),
        input_output_aliases={0: 0},
        scratch_shapes=[
            pltpu.VMEM((n_keys, 2 * blk), F32),
            pltpu.VMEM((n_keys, 2 * blk), F32),
            pltpu.VMEM((2, n_blocks, blk), F32),
            pltpu.VMEM((2, n_blocks, blk), F32),
            pltpu.VMEM((2, n_keys, blk), BF16),
            pltpu.VMEM((2, n_keys, blk), BF16),
            pltpu.VMEM((n_query_blocks, d, blk), F32),
        ],
        compiler_params=_params("arbitrary"),
        name=f"moba_attn_from_block_{first_block}",
    )(x, q0, q1, k0, k1, vt, kmean, *_stacks(*params))


def kernel(x, mem, ln_g, ln_b, x_wq, x_wkv, x_wo, ffn_w_in, ffn_w_out, ev_w_in, ev_w_out, a_ws, a_bs,
           a_ln_g, a_ln_b, b_norm_g, hgrn_lb_logits, od_w_qkv, od_w_out):
    bsz, t_len, d = x.shape
    assert d == D_MODEL and t_len % ROW_TILE == 0 and t_len % MOBA_BLOCK == 0
    assert ROW_TILE % MOBA_BLOCK == 0 and t_len // MOBA_BLOCK > 1
    n = bsz * t_len
    mem2d = mem.reshape(bsz * mem.shape[1], d)

    def rows_of(v):
        return v.reshape(-1, 1, v.shape[-1])

    def seq(v):
        return v.reshape(bsz, t_len, d)

    ffn_w_in, ffn_w_out = ffn_w_in.astype(BF16), ffn_w_out.astype(BF16)
    od_w_out = od_w_out.astype(BF16)
    ln_g, ln_b = rows_of(ln_g), rows_of(ln_b)
    a_ln_g, a_ln_b, b_norm_g = rows_of(a_ln_g), rows_of(a_ln_b), rows_of(b_norm_g)
    a_bs_t = jnp.swapaxes(a_bs, 1, 2)
    lb_logits = hgrn_lb_logits[None]
    alibi_fill = _alibi_query_fill()[None]

    for layer in range(DEPTH):
        j = layer // 2
        norm = [(ln_g, 3 * layer), (ln_b, 3 * layer)]
        if layer % 2 == 0:
            x = _even_mixer(
                x, j, (ev_w_in, j), (ev_w_out, j), (a_ws, j), (a_bs_t, j), (a_ln_g, j), (a_ln_b, j),
                (b_norm_g, j), (lb_logits, 0), *norm)
        else:
            q0, q1, k0, k1, vt, kmean = _moba_qkv(x.reshape(n, d), (od_w_qkv, j), (alibi_fill, 0), bsz, t_len)
            attn_args = (q0, q1, seq(k0), seq(k1), vt,
                         kmean.reshape(bsz, t_len // MOBA_BLOCK, d), (od_w_out, j), *norm)
            n_blocks = t_len // MOBA_BLOCK
            for first_block in range(0, n_blocks, MOBA_BLOCKS_PER_CALL):
                x = _moba_attn(x, first_block, min(MOBA_BLOCKS_PER_CALL, n_blocks - first_block), *attn_args)
        kv = _mem_kv(mem2d, (x_wkv, layer)).reshape(bsz, mem.shape[1], 2 * d)
        x = _cross_attn(x, kv, (x_wq, layer), (x_wo, layer), (ln_g, 3 * layer + 1), (ln_b, 3 * layer + 1))
        x = _ffn(x.reshape(n, d), (ffn_w_in, layer), (ffn_w_out, layer),
                 (ln_g, 3 * layer + 2), (ln_b, 3 * layer + 2)).reshape(bsz, t_len, d)
    return x
```

```python
import functools
import math

import jax
import jax.numpy as jnp
from jax import lax
from jax.experimental import pallas as pl
from jax.experimental.pallas import tpu as pltpu

D_MODEL = 1024
DEPTH = 2
ALPHA = (2.0 * DEPTH) ** 0.25
LN_EPS = 1e-5

GMLP_WIDTH = D_MODEL // 2
GMLP_GROUPS = 4
GMLP_GDIM = GMLP_WIDTH // GMLP_GROUPS
GMLP_CHUNK = 128
HGRN_WIDTH = D_MODEL // 2
HGRN_HEADS = 4
HGRN_DK = HGRN_WIDTH // HGRN_HEADS
HGRN_CHUNK = 64
EVEN_IN_WIDTH = 2 * GMLP_WIDTH + 4 * HGRN_WIDTH

MOBA_HEADS = 16
MOBA_HDIM = D_MODEL // MOBA_HEADS
MOBA_BLOCK = 256
MOBA_TOPK = 3
MOBA_PAIR = 2 * MOBA_HDIM
MOBA_BLOCKS_PER_CALL = 2

MEM_HEADS = 4
MEM_HDIM = D_MODEL // MEM_HEADS

D_FF = int(math.ceil(8 * D_MODEL / 3 / 256)) * 256
FFN_CHUNK = 256

ROW_TILE = 1024
MEM_ROW_TILE = 512
EVEN_ROW_TILE = 512
FFN_NORM_SLICES = 8
V7X_VMEM_LIMIT = 56 * 1024 * 1024

MASKED = -1e30

BF16 = jnp.bfloat16
F32 = jnp.float32
BF16_SUBLANES = 16


def _dot(a, b):
    return jnp.dot(a, b, preferred_element_type=F32)


def _dot_nt(a, b):
    return lax.dot_general(a, b, (((1,), (1,)), ((), ())), preferred_element_type=F32)


def _dot_tn(a, b):
    return lax.dot_general(a, b, (((0,), (0,)), ((), ())), preferred_element_type=F32)


def _layer_norm(z, g, b):
    mu = jnp.mean(z, axis=-1, keepdims=True)
    zc = z - mu
    var = jnp.mean(zc * zc, axis=-1, keepdims=True)
    return zc * lax.rsqrt(var + LN_EPS) * g + b


def _zero_after(token):
    bits = lax.bitcast_convert_type(token, jnp.int32)
    return lax.shift_right_logical(lax.shift_right_logical(bits, 16), 16).astype(F32)


def _residual_norm(x, y, g, b, after=None):
    alpha = ALPHA if after is None else ALPHA + _zero_after(after)
    return _layer_norm(x * alpha + y, g, b)


def _resident(picked):
    stack, index = picked
    return pl.BlockSpec((None,) + stack.shape[1:], lambda *_: (index,) + (0,) * (stack.ndim - 1),
                        pipeline_mode=pl.Buffered(1))


def _stacks(*picked):
    return [stack for stack, _ in picked]


def _bf16_copy(picked):
    stack, _ = picked
    return pltpu.VMEM(stack.shape[1:], BF16)


def _cast_weights_once(n_grid_axes, *pairs):
    first = pl.program_id(0) == 0
    for axis in range(1, n_grid_axes):
        first = jnp.logical_and(first, pl.program_id(axis) == 0)

    @pl.when(first)
    def _():
        for src, dst in pairs:
            dst[...] = src[...].astype(BF16)


def _params(*semantics):
    return pltpu.CompilerParams(dimension_semantics=semantics, vmem_limit_bytes=V7X_VMEM_LIMIT)


def _even_mixer_kernel(x_ref, w_in_f32_ref, w_out_f32_ref, ws_ref, bs_ref, aln_g_ref, aln_b_ref, bnorm_ref,
                       lb_logits_ref, ln_g_ref, ln_b_ref, o_ref,
                       q_s, f_s, i_s, g_s, y_s, state_s, w_in_ref, w_out_ref, *, lb_index):
    tm = x_ref.shape[1]
    _cast_weights_once(2, (w_in_f32_ref, w_in_ref), (w_out_f32_ref, w_out_ref))
    x = x_ref[0]
    xb = x.astype(BF16)

    @pl.when(pl.program_id(1) == 0)
    def _():
        state_s[...] = jnp.zeros_like(state_s)

    base = 2 * GMLP_WIDTH
    u_pre = _dot(xb, w_in_ref[:, 0:GMLP_WIDTH])
    v_pre = _dot(xb, w_in_ref[:, GMLP_WIDTH:2 * GMLP_WIDTH])
    q_pre = _dot(xb, w_in_ref[:, base:base + HGRN_WIDTH])
    q_s[...] = q_pre
    f_pre = _dot(xb, w_in_ref[:, base + HGRN_WIDTH:base + 2 * HGRN_WIDTH])
    f_s[...] = f_pre
    i_s[...] = jax.nn.silu(_dot(xb, w_in_ref[:, base + 2 * HGRN_WIDTH:base + 3 * HGRN_WIDTH]))
    g_s[...] = jax.nn.silu(_dot(xb, w_in_ref[:, base + 3 * HGRN_WIDTH:base + 4 * HGRN_WIDTH]))

    u = jax.nn.gelu(u_pre + _zero_after(q_pre[0:1, 0:1]))
    v = jax.nn.gelu(v_pre + _zero_after(f_pre[0:1, 0:1]))
    row = lax.broadcasted_iota(jnp.int32, (GMLP_CHUNK, GMLP_CHUNK), 0)
    col = lax.broadcasted_iota(jnp.int32, (GMLP_CHUNK, GMLP_CHUNK), 1)
    for g in range(GMLP_GROUPS):
        lanes = slice(g * GMLP_GDIM, (g + 1) * GMLP_GDIM)
        vn = _layer_norm(v[:, lanes], aln_g_ref[:, lanes], aln_b_ref[:, lanes]).astype(BF16)
        wg = jnp.where(col <= row, ws_ref[g], 0.0).astype(BF16)
        bias = bs_ref[:, g:g + 1]
        for c in range(tm // GMLP_CHUNK):
            rows = slice(c * GMLP_CHUNK, (c + 1) * GMLP_CHUNK)
            s = _dot(wg, vn[rows]) + bias
            y_s[rows, lanes] = (u[rows, lanes] * s).astype(BF16)

    n_chunks = tm // HGRN_CHUNK
    logits = lb_logits_ref[...]
    e = jnp.exp(logits - jnp.max(logits, axis=0, keepdims=True))
    lb = jnp.sum(e[0:lb_index + 1], axis=0, keepdims=True) / jnp.sum(e, axis=0, keepdims=True)

    crow =lax.broadcasted_iota(jnp.int32, (HGRN_CHUNK, HGRN_CHUNK), 0)
    ccol = lax.broadcasted_iota(jnp.int32, (HGRN_CHUNK, HGRN_CHUNK), 1)
    causal = ccol <= crow
    tril_ones = jnp.where(causal, 1.0, 0.0).astype(BF16)
    norm_g = bnorm_ref[...]

    for c in range(n_chunks):
        rows = slice(c * HGRN_CHUNK, (c + 1) * HGRN_CHUNK)
        f = lb + (1.0 - lb) * jax.nn.sigmoid(f_s[rows, :])
        log_f = jnp.log(f)
        log_f_hi = log_f.astype(BF16)
        log_f_lo = (log_f - log_f_hi.astype(F32)).astype(BF16)
        cum = _dot(tril_ones, log_f_hi) + _dot(tril_ones, log_f_lo)
        chunk_decay = jnp.exp(cum[HGRN_CHUNK - 1:HGRN_CHUNK, :])
        k_back = (1.0 - f) * jnp.exp(-cum)
        q_dec = (q_s[rows, :] * jnp.exp(cum)).astype(BF16)
        k_dec = k_back.astype(BF16)
        k_tail = (k_back * chunk_decay).astype(BF16)
        val = i_s[rows, :].astype(BF16)
        gate = g_s[rows, :]
        for h in range(HGRN_HEADS):
            lanes = slice(h * HGRN_DK, (h + 1) * HGRN_DK)
            attn = jnp.where(causal, _dot_nt(q_dec[:, lanes], k_dec[:, lanes]), 0.0).astype(BF16)
            state_t = state_s[h]
            o = _dot(attn, val[:, lanes]) + _dot_nt(q_dec[:, lanes], state_t.astype(BF16))
            state_s[h] = state_t * chunk_decay[:, lanes] + _dot_tn(val[:, lanes], k_tail[:, lanes])
            rms = lax.rsqrt(jnp.mean(o * o, axis=-1, keepdims=True) + LN_EPS)
            y_b = o * rms * norm_g[:, lanes] * gate[:, lanes]
            y_s[rows, GMLP_WIDTH + h * HGRN_DK:GMLP_WIDTH + (h + 1) * HGRN_DK] = y_b.astype(BF16)

    for rows in (slice(0, tm // 2), slice(tm // 2, tm)):
        y = _dot(y_s[rows, :], w_out_ref[...])
        o_ref[0, rows, :] = _residual_norm(x_ref[0, rows, :], y, ln_g_ref[...], ln_b_ref[...])


def _even_mixer(x, lb_index, *params):
    bsz, t_len, d = x.shape
    tm = EVEN_ROW_TILE
    assert t_len % tm == 0
    kern = functools.partial(_even_mixer_kernel, lb_index=lb_index)
    return pl.pallas_call(
        kern,
        grid=(bsz, t_len // tm),
        in_specs=[pl.BlockSpec((1, tm, d), lambda b, t: (b, t, 0))] + [_resident(p) for p in params],
        out_specs=pl.BlockSpec((1, tm, d), lambda b, t: (b, t, 0)),
        out_shape=jax.ShapeDtypeStruct(x.shape, F32),
        scratch_shapes=[
            pltpu.VMEM((tm, HGRN_WIDTH), F32), pltpu.VMEM((tm, HGRN_WIDTH), F32),
            pltpu.VMEM((tm, HGRN_WIDTH), F32), pltpu.VMEM((tm, HGRN_WIDTH), F32),
            pltpu.VMEM((tm, GMLP_WIDTH + HGRN_WIDTH), BF16),
            pltpu.VMEM((HGRN_HEADS, HGRN_DK, HGRN_DK), F32),
            _bf16_copy(params[0]), _bf16_copy(params[1]),
        ],
        compiler_params=_params("arbitrary", "arbitrary"),
        name="even_mixer",
    )(x, *_stacks(*params))


def _mem_kv_kernel(mem_ref, w_f32_ref, o_ref, w_ref):
    _cast_weights_once(1, (w_f32_ref, w_ref))
    o_ref[...] = _dot(mem_ref[...].astype(BF16), w_ref[...]).astype(BF16)


def _mem_kv(mem2d, w_kv):
    n, d = mem2d.shape
    tm = MEM_ROW_TILE
    assert n % tm == 0
    width = w_kv[0].shape[-1]
    return pl.pallas_call(
        _mem_kv_kernel,
        grid=(n // tm,),
        in_specs=[pl.BlockSpec((tm, d), lambda i: (i, 0)), _resident(w_kv)],
        out_specs=pl.BlockSpec((tm, width), lambda i: (i, 0)),
        out_shape=jax.ShapeDtypeStruct((n, width), BF16),
        scratch_shapes=[_bf16_copy(w_kv)],
        compiler_params=_params("arbitrary"),
        name="mem_kv",
    )(mem2d, *_stacks(w_kv))


def _cross_attn_kernel(x_ref, kv_ref, wq_f32_ref, wo_f32_ref, ln_g_ref, ln_b_ref, o_ref, att_s, y_s,
                       wq_ref, wo_ref):
    _cast_weights_once(2, (wq_f32_ref, wq_ref), (wo_f32_ref, wo_ref))
    half = x_ref.shape[1] // 2
    slice_rows = half // MEM_HEADS

    def attend(rows):
        q = (_dot(x_ref[0, rows, :].astype(BF16), wq_ref[...]) * (MEM_HDIM ** -0.5)).astype(BF16)
        done = []
        for h in range(MEM_HEADS):
            lanes = slice(h * MEM_HDIM, (h + 1) * MEM_HDIM)
            k_h = kv_ref[0, :, h * MEM_HDIM:(h + 1) * MEM_HDIM]
            v_h = kv_ref[0, :, D_MODEL + h * MEM_HDIM:D_MODEL + (h + 1) * MEM_HDIM]
            s = _dot_nt(q[:, lanes], k_h)
            p = jnp.exp(s - jnp.max(s, axis=-1, keepdims=True))
            denom = jnp.sum(p, axis=-1, keepdims=True)
            weighted = _dot(p.astype(BF16), v_h)
            att_s[rows, lanes] = (weighted / denom).astype(BF16)
            done.append(weighted[0:1, 0:1])
        y_s[rows, :] = _dot(att_s[rows, :], wo_ref[...])
        return done

    def finish(rows, after=None):
        o_ref[0, rows, :] = _residual_norm(x_ref[0, rows, :], y_s[rows, :], ln_g_ref[...], ln_b_ref[...], after)

    attend(slice(0, half))
    done = attend(slice(half, 2 * half))
    for h in range(MEM_HEADS):
        finish(slice(h * slice_rows, (h + 1) * slice_rows), after=done[h])
    finish(slice(half, 2 * half))


def _cross_attn(x, kv, *params):
    bsz, t_len, d = x.shape
    tm = ROW_TILE
    return pl.pallas_call(
        _cross_attn_kernel,
        grid=(bsz, t_len // tm),
        in_specs=[
            pl.BlockSpec((1, tm, d), lambda b, t: (b, t, 0)),
            pl.BlockSpec((1,) + kv.shape[1:], lambda b, t: (b, 0, 0)),
        ] + [_resident(p) for p in params],
        out_specs=pl.BlockSpec((1, tm, d), lambda b, t: (b, t, 0)),
        out_shape=jax.ShapeDtypeStruct(x.shape, F32),
        scratch_shapes=[pltpu.VMEM((tm, d), BF16), pltpu.VMEM((tm, d), F32),
                        _bf16_copy(params[0]), _bf16_copy(params[1])],
        compiler_params=_params("arbitrary", "arbitrary"),
        name="cross_attn",
    )(x, kv, *_stacks(*params))


def _ffn_kernel(x_ref, w_in_ref, w_out_ref, ln_g_ref, ln_b_ref, o_ref, acc_s):
    half = x_ref.shape[0] // 2
    n_chunks = D_FF // FFN_CHUNK
    slice_rows = half // FFN_NORM_SLICES

    def chunk(rows, xb, c):
        cols = slice(c * FFN_CHUNK, (c + 1) * FFN_CHUNK)
        gate = _dot(xb, w_in_ref[:, cols])
        up = _dot(xb, w_in_ref[:, D_FF + c * FFN_CHUNK:D_FF + (c + 1) * FFN_CHUNK])
        act = (jax.nn.silu(gate) * up).astype(BF16)
        part = _dot(act, w_out_ref[cols, :])
        if c == 0:
            acc_s[rows, :] = part
        else:
            acc_s[rows, :] += part
        return part[0:1, :]

    def finish(rows, after=None):
        o_ref[rows, :] = _residual_norm(x_ref[rows, :], acc_s[rows, :], ln_g_ref[...], ln_b_ref[...], after)

    first, second = slice(0, half), slice(half, 2 * half)
    xb_first = x_ref[first, :].astype(BF16)
    xb_second = x_ref[second, :].astype(BF16)
    for c in range(n_chunks):
        chunk(first, xb_first, c)
    for c in range(n_chunks):
        done = chunk(second, xb_second, c)
        if c < FFN_NORM_SLICES:
            finish(slice(c * slice_rows, (c + 1) * slice_rows), after=done)
    finish(second)


def _ffn(x2d, *params):
    n, d = x2d.shape
    tm = ROW_TILE
    assert n % tm == 0
    return pl.pallas_call(
        _ffn_kernel,
        grid=(n // tm,),
        in_specs=[pl.BlockSpec((tm, d), lambda i: (i, 0))] + [_resident(p) for p in params],
        out_specs=pl.BlockSpec((tm, d), lambda i: (i, 0)),
        out_shape=jax.ShapeDtypeStruct((n, d), F32),
        scratch_shapes=[pltpu.VMEM((tm, d), F32)],
        compiler_params=_params("parallel"),
        name="ffn",
    )(x2d, *_stacks(*params))


LOG2_E = 1.4426950408889634
ALIBI_LANES = 3


def _moba_qkv_kernel(x_ref, w_f32_ref, fill_ref, q0_ref, q1_ref, k0_ref, k1_ref, vt_ref, kmean_ref, w_ref,
                     *, tiles_per_seq):
    _cast_weights_once(1, (w_f32_ref, w_ref))
    tm = x_ref.shape[0]
    xb = x_ref[...].astype(BF16)
    transposed = (((0,), (1,)), ((), ()))
    qt = lax.dot_general(w_ref[:, 0:D_MODEL], xb, transposed, preferred_element_type=F32)
    qt = qt * (MOBA_HDIM ** -0.5 * LOG2_E)
    k = _dot(xb, w_ref[:, D_MODEL:2 * D_MODEL])
    for i in range(tm // MOBA_BLOCK):
        kmean_ref[i] = jnp.mean(k[i * MOBA_BLOCK:(i + 1) * MOBA_BLOCK], axis=0, keepdims=True)
    vt_ref[0] = lax.dot_general(w_ref[:, 2 * D_MODEL:3 * D_MODEL], xb, transposed,
                                preferred_element_type=F32).astype(BF16)

    lane = lax.broadcasted_iota(jnp.int32, (tm, MOBA_PAIR), 1)
    pos = lax.broadcasted_iota(jnp.int32, (tm, MOBA_PAIR), 0) + (pl.program_id(0) % tiles_per_seq) * tm
    pos_block = ((pos // MOBA_BLOCK) * MOBA_BLOCK).astype(F32)
    pos_offset = (pos % MOBA_BLOCK).astype(F32)
    feature = lax.broadcasted_iota(jnp.int32, (MOBA_PAIR, tm), 0)
    for e, (q_ref, k_ref) in enumerate(((q0_ref, k0_ref), (q1_ref, k1_ref))):
        own = (lane < MOBA_HDIM) if e == 0 else (lane >= MOBA_HDIM)
        own_feature = (feature < MOBA_HDIM) if e == 0 else (feature >= MOBA_HDIM)
        partner = MOBA_HDIM * (1 - e)
        key_fill = jnp.where((lane >= partner) & (lane < partner + ALIBI_LANES), pos_block,
                             jnp.where((lane >= partner + ALIBI_LANES) & (lane < partner + 2 * ALIBI_LANES),
                                       pos_offset, 0.0))
        for p in range(MOBA_HEADS // 2):
            slab = slice(p * MOBA_PAIR, (p + 1) * MOBA_PAIR)
            q_fill = jnp.concatenate([fill_ref[e, slab, :]] * (tm // fill_ref.shape[-1]), axis=1)
            q_ref[0, slab, :] = jnp.where(own_feature, qt[slab, :], q_fill).astype(BF16)
            k_ref[:, slab] = jnp.where(own, k[:, slab], key_fill).astype(BF16)


def _moba_qkv(x2d, w_qkv, fill, bsz, t_len):
    n, d = x2d.shape
    tm = ROW_TILE
    tiles_per_seq = t_len // tm
    rows = pl.BlockSpec((tm, d), lambda i: (i, 0))
    feature_major = pl.BlockSpec((1, d, tm), lambda i: (i // tiles_per_seq, 0, i % tiles_per_seq))
    return pl.pallas_call(
        functools.partial(_moba_qkv_kernel, tiles_per_seq=tiles_per_seq),
        grid=(n // tm,),
        in_specs=[rows, _resident(w_qkv), _resident(fill)],
        out_specs=[
            feature_major, feature_major, rows, rows, feature_major,
            pl.BlockSpec((tm // MOBA_BLOCK, 1, d), lambda i: (i, 0, 0)),
        ],
        out_shape=[
            jax.ShapeDtypeStruct((bsz, d, t_len), BF16), jax.ShapeDtypeStruct((bsz, d, t_len), BF16),
            jax.ShapeDtypeStruct((n, d), BF16), jax.ShapeDtypeStruct((n, d), BF16),
            jax.ShapeDtypeStruct((bsz, d, t_len), BF16),
            jax.ShapeDtypeStruct((n // MOBA_BLOCK, 1, d), F32),
        ],
        scratch_shapes=[_bf16_copy(w_qkv)],
        compiler_params=_params("arbitrary"),
        name="moba_qkv",
    )(x2d, *_stacks(w_qkv, fill))


def _alibi_query_fill():
    slopes = jnp.asarray([2.0 ** (-8.0 * (h + 1) / MOBA_HEADS) for h in range(MOBA_HEADS)], F32) * LOG2_E
    pieces = []
    rest = slopes
    for _ in range(ALIBI_LANES):
        piece = rest.astype(BF16).astype(F32)
        pieces.append(piece)
        rest = rest - piece
    pieces = jnp.stack(pieces + pieces, axis=1)
    fill = jnp.zeros((2, MOBA_HEADS // 2, MOBA_PAIR), F32)
    for e in range(2):
        partner = MOBA_HDIM * (1 - e)
        fill = fill.at[e, :, partner:partner + 2 * ALIBI_LANES].set(pieces[e::2])
    return jnp.broadcast_to(fill.reshape(2, D_MODEL, 1), (2, D_MODEL, MOBA_PAIR))


def _moba_attn_kernel(x_ref, q0_ref, q1_ref, k0_ref, k1_ref, vt_ref, kmean_ref, w_out_ref, ln_g_ref, ln_b_ref,
                      o_ref, sa_s, sb_s, sha_s, shb_s, pa_s, pb_s, att_s, *, first_block, n_query_blocks):
    n_blocks = kmean_ref.shape[1]
    blk = MOBA_BLOCK
    q_refs = (q0_ref, q1_ref)
    k_refs = (k0_ref, k1_ref)
    blk_i = lax.broadcasted_iota(jnp.int32, (n_blocks, blk), 0)
    mean_lane = lax.broadcasted_iota(jnp.int32, (n_blocks, MOBA_PAIR), 1)

    def attend(n_past, query_rows, att_buf, buffers_read):
        n_keys = (n_past + 1) * blk
        select = n_past > MOBA_TOPK

        def pair_lanes(hp):
            return pl.ds(pl.multiple_of(hp * MOBA_PAIR, MOBA_PAIR), MOBA_PAIR)

        def scores_and_shifts(hp, s_buf, sh_buf, after=None):
            lanes = pair_lanes(hp)
            for e in range(2):
                cols = slice(e * blk, (e + 1) * blk)
                q_e = q_refs[e][0, lanes, query_rows]
                if after is not None:
                    q_e = (q_e.astype(F32) + _zero_after(after)).astype(BF16)
                if select:
                    own = (mean_lane < MOBA_HDIM) if e == 0 else (mean_lane >= MOBA_HDIM)
                    kmean = jnp.where(own, kmean_ref[0, :, lanes], 0.0)
                    kmean_hi = kmean.astype(BF16)
                    kmean_lo = (kmean - kmean_hi.astype(F32)).astype(BF16)
                    aff = _dot(kmean_hi, q_e) + _dot(kmean_lo, q_e)
                    rank = jnp.zeros((n_blocks, blk), F32)
                    for jp in range(n_past):
                        other = aff[jp:jp + 1, :]
                        beats = (other > aff) | ((other == aff) & (jp < blk_i))
                        rank = rank + jnp.where(beats, 1.0, 0.0)
                    chosen = rank < MOBA_TOPK
                m = None
                for j in range(n_past + 1):
                    rows = slice(j * blk, (j + 1) * blk)
                    t = _dot(k_refs[e][0, rows, lanes], q_e)
                    if j == n_past:
                        key_i = lax.broadcasted_iota(jnp.int32, (blk, blk), 0)
                        qry_i = lax.broadcasted_iota(jnp.int32, (blk, blk), 1)
                        t = jnp.where(key_i <= qry_i, t, MASKED)
                    s_buf[rows, cols] = t
                    m_j = jnp.max(t, axis=0, keepdims=True)
                    if select and j < n_past:
                        m_j = jnp.where(chosen[j:j + 1, :], m_j, MASKED)
                    m = m_j if m is None else jnp.maximum(m, m_j)
                shifts = jnp.broadcast_to(m, (n_blocks, blk))
                if select:
                    shifts = jnp.where(chosen | (blk_i == n_past), shifts, -MASKED)
                sh_buf[e] = shifts

        def probabilities(s_buf, sh_buf, p_buf, after=None):
            hold = 0.0 if after is None else _zero_after(after)
            for e in range(2):
                cols = slice(e * blk, (e + 1) * blk)
                for j in range(n_past + 1):
                    rows = slice(j * blk, (j + 1) * blk)
                    shift = sh_buf[e, j:j + 1, :] + hold
                    p_buf[e, rows, :] = jnp.exp2(s_buf[rows, cols] - shift).astype(BF16)

        def weighted_values(hp, p_buf):
            for e in range(2):
                feat = pl.ds(pl.multiple_of(hp * MOBA_PAIR + e * MOBA_HDIM, MOBA_HDIM), MOBA_HDIM)
                ones = jnp.ones((BF16_SUBLANES, n_keys), BF16)
                values = jnp.concatenate([vt_ref[0, feat, 0:n_keys], ones], axis=0)
                acc = _dot(values, p_buf[e, 0:n_keys, :])
                denom = acc[MOBA_HDIM:MOBA_HDIM + 1]
                att_buf[feat, :] = acc[0:MOBA_HDIM] / denom
            return denom[:, 0:1]

        n_pairs = MOBA_HEADS // 2
        read_a, read_b = buffers_read
        scores_and_shifts(0, sa_s, sha_s)
        scores_and_shifts(1, sb_s, shb_s, after=read_b)
        probabilities(sa_s, sha_s, pa_s, after=read_a)

        def two_pairs(i, carry):
            hp = 2 * i + 1
            scores_and_shifts(hp + 1, sa_s, sha_s)
            probabilities(sb_s, shb_s, pb_s)
            consumed = weighted_values(hp - 1, pa_s)
            scores_and_shifts(hp + 2, sb_s, shb_s)
            probabilities(sa_s, sha_s, pa_s, after=consumed)
            weighted_values(hp, pb_s)
            return carry

        lax.fori_loop(0, n_pairs // 2 - 1, two_pairs, 0)
        probabilities(sb_s, shb_s, pb_s)
        return weighted_values(n_pairs - 2, pa_s), weighted_values(n_pairs - 1, pb_s)

    buffers_read = (None, None)
    for j in range(n_query_blocks):
        query_rows = slice(j * blk, (j + 1) * blk)
        buffers_read = attend(first_block + j, query_rows, att_s.at[j], buffers_read)
        att = att_s[j].T.astype(BF16)
        y = _dot(att, w_out_ref[...])
        o_ref[0, query_rows, :] = _residual_norm(x_ref[0, query_rows, :], y, ln_g_ref[...], ln_b_ref[...])


def _moba_attn(x, first_block, n_query_blocks, q0, q1, k0, k1, vt, kmean, *params):
    bsz, t_len, d = x.shape
    blk = MOBA_BLOCK
    n_blocks = t_len // blk
    n_keys = (first_block + n_query_blocks) * blk
    assert first_block % n_query_blocks == 0
    query_rows = pl.BlockSpec((1, n_query_blocks * blk, d), lambda b: (b, first_block // n_query_blocks, 0))
    query_cols = pl.BlockSpec((1, d, n_query_blocks * blk), lambda b: (b, 0, first_block // n_query_blocks))
    visible_keys = pl.BlockSpec((1, n_keys, d), lambda b: (b, 0, 0))
    return pl.pallas_call(
        functools.partial(_moba_attn_kernel, first_block=first_block, n_query_blocks=n_query_blocks),
        grid=(bsz,),
        in_specs=[
            query_rows, query_cols, query_cols, visible_keys, visible_keys,
            pl.BlockSpec((1, d, n_keys), lambda b: (b, 0, 0)),
            pl.BlockSpec((1, n_blocks, d), lambda b: (b, 0, 0)),
        ] + [_resident(p) for p in params],
        out_specs=query_rows,
        out_shape=jax.ShapeDtypeStruct(x.shape, F32),
        input_output_aliases={0: 0},
        scratch_shapes=[
            pltpu.VMEM((n_keys, 2 * blk), F32),
            pltpu.VMEM((n_keys, 2 * blk), F32),
            pltpu.VMEM((2, n_blocks, blk), F32),
            pltpu.VMEM((2, n_blocks, blk), F32),
            pltpu.VMEM((2, n_keys, blk), BF16),
            pltpu.VMEM((2, n_keys, blk), BF16),
            pltpu.VMEM((n_query_blocks, d, blk), F32),
        ],
        compiler_params=_params("arbitrary"),
        name=f"moba_attn_from_block_{first_block}",
    )(x, q0, q1, k0, k1, vt, kmean, *_stacks(*params))


def kernel(x, mem, ln_g, ln_b, x_wq, x_wkv, x_wo, ffn_w_in, ffn_w_out, ev_w_in, ev_w_out, a_ws, a_bs,
           a_ln_g, a_ln_b, b_norm_g, hgrn_lb_logits, od_w_qkv, od_w_out):
    bsz, t_len, d = x.shape
    assert d == D_MODEL and t_len % ROW_TILE == 0 and t_len % MOBA_BLOCK == 0
    assert ROW_TILE % MOBA_BLOCK == 0 and t_len // MOBA_BLOCK > 1
    n = bsz * t_len
    mem2d = mem.reshape(bsz * mem.shape[1], d)

    def rows_of(v):
        return v.reshape(-1, 1, v.shape[-1])

    def seq(v):
        return v.reshape(bsz, t_len, d)

    ffn_w_in, ffn_w_out = ffn_w_in.astype(BF16), ffn_w_out.astype(BF16)
    od_w_out = od_w_out.astype(BF16)
    ln_g, ln_b = rows_of(ln_g), rows_of(ln_b)
    a_ln_g, a_ln_b, b_norm_g = rows_of(a_ln_g), rows_of(a_ln_b), rows_of(b_norm_g)
    a_bs_t = jnp.swapaxes(a_bs, 1, 2)
    lb_logits = hgrn_lb_logits[None]
    alibi_fill = _alibi_query_fill()[None]

    for layer in range(DEPTH):
        j = layer // 2
        norm = [(ln_g, 3 * layer), (ln_b, 3 * layer)]
        if layer % 2 == 0:
            x = _even_mixer(
                x, j, (ev_w_in, j), (ev_w_out, j), (a_ws, j), (a_bs_t, j), (a_ln_g, j), (a_ln_b, j),
                (b_norm_g, j), (lb_logits, 0), *norm)
        else:
            q0, q1, k0, k1, vt, kmean = _moba_qkv(x.reshape(n, d), (od_w_qkv, j), (alibi_fill, 0), bsz, t_len)
            attn_args = (q0, q1, seq(k0), seq(k1), vt,
                         kmean.reshape(bsz, t_len // MOBA_BLOCK, d), (od_w_out, j), *norm)
            n_blocks = t_len // MOBA_BLOCK
            for first_block in range(0, n_blocks, MOBA_BLOCKS_PER_CALL):
                x = _moba_attn(x, first_block, min(MOBA_BLOCKS_PER_CALL, n_blocks - first_block), *attn_args)
        kv = _mem_kv(mem2d, (x_wkv, layer)).reshape(bsz, mem.shape[1], 2 * d)
        x = _cross_attn(x, kv, (x_wq, layer), (x_wo, layer), (ln_g, 3 * layer + 1), (ln_b, 3 * layer + 1))
        x = _ffn(x.reshape(n, d), (ffn_w_in, layer), (ffn_w_out, layer),
                 (ln_g, 3 * layer + 2), (ln_b, 3 * layer + 2)).reshape(bsz, t_len, d)
    return x
```

```python
import functools
import math

import jax
import jax.numpy as jnp
from jax import lax
from jax.experimental import pallas as pl
from jax.experimental.pallas import tpu as pltpu

D_MODEL = 1024
DEPTH = 2
ALPHA = (2.0 * DEPTH) ** 0.25
LN_EPS = 1e-5

GMLP_WIDTH = D_MODEL // 2
GMLP_GROUPS = 4
GMLP_GDIM = GMLP_WIDTH // GMLP_GROUPS
GMLP_CHUNK = 128
HGRN_WIDTH = D_MODEL // 2
HGRN_HEADS = 4
HGRN_DK = HGRN_WIDTH // HGRN_HEADS
HGRN_CHUNK = 64
EVEN_IN_WIDTH = 2 * GMLP_WIDTH + 4 * HGRN_WIDTH

MOBA_HEADS = 16
MOBA_HDIM = D_MODEL // MOBA_HEADS
MOBA_BLOCK = 256
MOBA_TOPK = 3
MOBA_PAIR = 2 * MOBA_HDIM
MOBA_BLOCKS_PER_CALL = 2

MEM_HEADS = 4
MEM_HDIM = D_MODEL // MEM_HEADS

D_FF = int(math.ceil(8 * D_MODEL / 3 / 256)) * 256
FFN_CHUNK = 256

ROW_TILE = 1024
MEM_ROW_TILE = 512
EVEN_ROW_TILE = 512
FFN_NORM_SLICES = 8
V7X_VMEM_LIMIT = 56 * 1024 * 1024

MASKED = -1e30

BF16 = jnp.bfloat16
F32 = jnp.float32
BF16_SUBLANES = 16


def _dot(a, b):
    return jnp.dot(a, b, preferred_element_type=F32)


def _dot_nt(a, b):
    return lax.dot_general(a, b, (((1,), (1,)), ((), ())), preferred_element_type=F32)


def _dot_tn(a, b):
    return lax.dot_general(a, b, (((0,), (0,)), ((), ())), preferred_element_type=F32)


def _layer_norm(z, g, b):
    mu = jnp.mean(z, axis=-1, keepdims=True)
    zc = z - mu
    var = jnp.mean(zc * zc, axis=-1, keepdims=True)
    return zc * lax.rsqrt(var + LN_EPS) * g + b


def _zero_after(token):
    bits = lax.bitcast_convert_type(token, jnp.int32)
    return lax.shift_right_logical(lax.shift_right_logical(bits, 16), 16).astype(F32)


def _residual_norm(x, y, g, b, after=None):
    alpha = ALPHA if after is None else ALPHA + _zero_after(after)
    return _layer_norm(x * alpha + y, g, b)


def _resident(picked):
    stack, index = picked
    return pl.BlockSpec((None,) + stack.shape[1:], lambda *_: (index,) + (0,) * (stack.ndim - 1),
                        pipeline_mode=pl.Buffered(1))


def _stacks(*picked):
    return [stack for stack, _ in picked]


def _bf16_copy(picked):
    stack, _ = picked
    return pltpu.VMEM(stack.shape[1:], BF16)


def _cast_weights_once(n_grid_axes, *pairs):
    first = pl.program_id(0) == 0
    for axis in range(1, n_grid_axes):
        first = jnp.logical_and(first, pl.program_id(axis) == 0)

    @pl.when(first)
    def _():
        for src, dst in pairs:
            dst[...] = src[...].astype(BF16)


def _params(*semantics):
    return pltpu.CompilerParams(dimension_semantics=semantics, vmem_limit_bytes=V7X_VMEM_LIMIT)


def _even_mixer_kernel(x_ref, w_in_f32_ref, w_out_f32_ref, ws_ref, bs_ref, aln_g_ref, aln_b_ref, bnorm_ref,
                       lb_logits_ref, ln_g_ref, ln_b_ref, o_ref,
                       q_s, f_s, i_s, g_s, y_s, state_s, w_in_ref, w_out_ref, *, lb_index):
    tm = x_ref.shape[1]
    _cast_weights_once(2, (w_in_f32_ref, w_in_ref), (w_out_f32_ref, w_out_ref))
    x = x_ref[0]
    xb = x.astype(BF16)

    @pl.when(pl.program_id(1) == 0)
    def _():
        state_s[...] = jnp.zeros_like(state_s)

    base = 2 * GMLP_WIDTH
    u_pre = _dot(xb, w_in_ref[:, 0:GMLP_WIDTH])
    v_pre = _dot(xb, w_in_ref[:, GMLP_WIDTH:2 * GMLP_WIDTH])
    q_pre = _dot(xb, w_in_ref[:, base:base + HGRN_WIDTH])
    q_s[...] = q_pre
    f_pre = _dot(xb, w_in_ref[:, base + HGRN_WIDTH:base + 2 * HGRN_WIDTH])
    f_s[...] = f_pre
    i_s[...] = jax.nn.silu(_dot(xb, w_in_ref[:, base + 2 * HGRN_WIDTH:base + 3 * HGRN_WIDTH]))
    g_s[...] = jax.nn.silu(_dot(xb, w_in_ref[:, base + 3 * HGRN_WIDTH:base + 4 * HGRN_WIDTH]))

    u = jax.nn.gelu(u_pre + _zero_after(q_pre[0:1, 0:1]))
    v = jax.nn.gelu(v_pre + _zero_after(f_pre[0:1, 0:1]))
    row = lax.broadcasted_iota(jnp.int32, (GMLP_CHUNK, GMLP_CHUNK), 0)
    col = lax.broadcasted_iota(jnp.int32, (GMLP_CHUNK, GMLP_CHUNK), 1)
    for g in range(GMLP_GROUPS):
        lanes = slice(g * GMLP_GDIM, (g + 1) * GMLP_GDIM)
        vn = _layer_norm(v[:, lanes], aln_g_ref[:, lanes], aln_b_ref[:, lanes]).astype(BF16)
        wg = jnp.where(col <= row, ws_ref[g], 0.0).astype(BF16)
        bias = bs_ref[:, g:g + 1]
        for c in range(tm // GMLP_CHUNK):
            rows = slice(c * GMLP_CHUNK, (c + 1) * GMLP_CHUNK)
            s = _dot(wg, vn[rows]) + bias
            y_s[rows, lanes] = (u[rows, lanes] * s).astype(BF16)

    n_chunks = tm // HGRN_CHUNK
    logits = lb_logits_ref[...]
    e = jnp.exp(logits - jnp.max(logits, axis=0, keepdims=True))
    lb = jnp.sum(e[0:lb_index + 1], axis=0, keepdims=True) / jnp.sum(e, axis=0, keepdims=True)

    crow =lax.broadcasted_iota(jnp.int32, (HGRN_CHUNK, HGRN_CHUNK), 0)
    ccol = lax.broadcasted_iota(jnp.int32, (HGRN_CHUNK, HGRN_CHUNK), 1)
    causal = ccol <= crow
    tril_ones = jnp.where(causal, 1.0, 0.0).astype(BF16)
    norm_g = bnorm_ref[...]

    for c in range(n_chunks):
        rows = slice(c * HGRN_CHUNK, (c + 1) * HGRN_CHUNK)
        f = lb + (1.0 - lb) * jax.nn.sigmoid(f_s[rows, :])
        log_f = jnp.log(f)
        log_f_hi = log_f.astype(BF16)
        log_f_lo = (log_f - log_f_hi.astype(F32)).astype(BF16)
        cum = _dot(tril_ones, log_f_hi) + _dot(tril_ones, log_f_lo)
        chunk_decay = jnp.exp(cum[HGRN_CHUNK - 1:HGRN_CHUNK, :])
        k_back = (1.0 - f) * jnp.exp(-cum)
        q_dec = (q_s[rows, :] * jnp.exp(cum)).astype(BF16)
        k_dec = k_back.astype(BF16)
        k_tail = (k_back * chunk_decay).astype(BF16)
        val = i_s[rows, :].astype(BF16)
        gate = g_s[rows, :]
        for h in range(HGRN_HEADS):
            lanes = slice(h * HGRN_DK, (h + 1) * HGRN_DK)
            attn = jnp.where(causal, _dot_nt(q_dec[:, lanes], k_dec[:, lanes]), 0.0).astype(BF16)
            state_t = state_s[h]
            o = _dot(attn, val[:, lanes]) + _dot(q_dec[:, lanes], state_t.T.astype(BF16))
            state_s[h] = state_t * chunk_decay[:, lanes] + _dot_tn(val[:, lanes], k_tail[:, lanes])
            rms = lax.rsqrt(jnp.mean(o * o, axis=-1, keepdims=True) + LN_EPS)
            y_b = o * rms * norm_g[:, lanes] * gate[:, lanes]
            y_s[rows, GMLP_WIDTH + h * HGRN_DK:GMLP_WIDTH + (h + 1) * HGRN_DK] = y_b.astype(BF16)

    for rows in (slice(0, tm // 2), slice(tm // 2, tm)):
        y = _dot(y_s[rows, :], w_out_ref[...])
        o_ref[0, rows, :] = _residual_norm(x_ref[0, rows, :], y, ln_g_ref[...], ln_b_ref[...])


def _even_mixer(x, lb_index, *params):
    bsz, t_len, d = x.shape
    tm = EVEN_ROW_TILE
    assert t_len % tm == 0
    kern = functools.partial(_even_mixer_kernel, lb_index=lb_index)
    return pl.pallas_call(
        kern,
        grid=(bsz, t_len // tm),
        in_specs=[pl.BlockSpec((1, tm, d), lambda b, t: (b, t, 0))] + [_resident(p) for p in params],
        out_specs=pl.BlockSpec((1, tm, d), lambda b, t: (b, t, 0)),
        out_shape=jax.ShapeDtypeStruct(x.shape, F32),
        scratch_shapes=[
            pltpu.VMEM((tm, HGRN_WIDTH), F32), pltpu.VMEM((tm, HGRN_WIDTH), F32),
            pltpu.VMEM((tm, HGRN_WIDTH), F32), pltpu.VMEM((tm, HGRN_WIDTH), F32),
            pltpu.VMEM((tm, GMLP_WIDTH + HGRN_WIDTH), BF16),
            pltpu.VMEM((HGRN_HEADS, HGRN_DK, HGRN_DK), F32),
            _bf16_copy(params[0]), _bf16_copy(params[1]),
        ],
        compiler_params=_params("arbitrary", "arbitrary"),
        name="even_mixer",
    )(x, *_stacks(*params))


def _mem_kv_kernel(mem_ref, w_f32_ref, o_ref, w_ref):
    _cast_weights_once(1, (w_f32_ref, w_ref))
    o_ref[...] = _dot(mem_ref[...].astype(BF16), w_ref[...]).astype(BF16)


def _mem_kv(mem2d, w_kv):
    n, d = mem2d.shape
    tm = MEM_ROW_TILE
    assert n % tm == 0
    width = w_kv[0].shape[-1]
    return pl.pallas_call(
        _mem_kv_kernel,
        grid=(n // tm,),
        in_specs=[pl.BlockSpec((tm, d), lambda i: (i, 0)), _resident(w_kv)],
        out_specs=pl.BlockSpec((tm, width), lambda i: (i, 0)),
        out_shape=jax.ShapeDtypeStruct((n, width), BF16),
        scratch_shapes=[_bf16_copy(w_kv)],
        compiler_params=_params("arbitrary"),
        name="mem_kv",
    )(mem2d, *_stacks(w_kv))


def _cross_attn_kernel(x_ref, kv_ref, wq_f32_ref, wo_f32_ref, ln_g_ref, ln_b_ref, o_ref, att_s, y_s,
                       wq_ref, wo_ref):
    _cast_weights_once(2, (wq_f32_ref, wq_ref), (wo_f32_ref, wo_ref))
    half = x_ref.shape[1] // 2
    slice_rows = half // MEM_HEADS

    def attend(rows):
        q = (_dot(x_ref[0, rows, :].astype(BF16), wq_ref[...]) * (MEM_HDIM ** -0.5)).astype(BF16)
        done = []
        for h in range(MEM_HEADS):
            lanes = slice(h * MEM_HDIM, (h + 1) * MEM_HDIM)
            k_h = kv_ref[0, :, h * MEM_HDIM:(h + 1) * MEM_HDIM]
            v_h = kv_ref[0, :, D_MODEL + h * MEM_HDIM:D_MODEL + (h + 1) * MEM_HDIM]
            s = _dot_nt(q[:, lanes], k_h)
            p = jnp.exp(s - jnp.max(s, axis=-1, keepdims=True))
            denom = jnp.sum(p, axis=-1, keepdims=True)
            weighted = _dot(p.astype(BF16), v_h)
            att_s[rows, lanes] = (weighted / denom).astype(BF16)
            done.append(weighted[0:1, 0:1])
        y_s[rows, :] = _dot(att_s[rows, :], wo_ref[...])
        return done

    def finish(rows, after=None):
        o_ref[0, rows, :] = _residual_norm(x_ref[0, rows, :], y_s[rows, :], ln_g_ref[...], ln_b_ref[...], after)

    attend(slice(0, half))
    done = attend(slice(half, 2 * half))
    for h in range(MEM_HEADS):
        finish(slice(h * slice_rows, (h + 1) * slice_rows), after=done[h])
    finish(slice(half, 2 * half))


def _cross_attn(x, kv, *params):
    bsz, t_len, d = x.shape
    tm = ROW_TILE
    return pl.pallas_call(
        _cross_attn_kernel,
        grid=(bsz, t_len // tm),
        in_specs=[
            pl.BlockSpec((1, tm, d), lambda b, t: (b, t, 0)),
            pl.BlockSpec((1,) + kv.shape[1:], lambda b, t: (b, 0, 0)),
        ] + [_resident(p) for p in params],
        out_specs=pl.BlockSpec((1, tm, d), lambda b, t: (b, t, 0)),
        out_shape=jax.ShapeDtypeStruct(x.shape, F32),
        scratch_shapes=[pltpu.VMEM((tm, d), BF16), pltpu.VMEM((tm, d), F32),
                        _bf16_copy(params[0]), _bf16_copy(params[1])],
        compiler_params=_params("arbitrary", "arbitrary"),
        name="cross_attn",
    )(x, kv, *_stacks(*params))


def _ffn_kernel(x_ref, w_in_ref, w_out_ref, ln_g_ref, ln_b_ref, o_ref, acc_s):
    half = x_ref.shape[0] // 2
    n_chunks = D_FF // FFN_CHUNK
    slice_rows = half // FFN_NORM_SLICES

    def chunk(rows, xb, c):
        cols = slice(c * FFN_CHUNK, (c + 1) * FFN_CHUNK)
        gate = _dot(xb, w_in_ref[:, cols])
        up = _dot(xb, w_in_ref[:, D_FF + c * FFN_CHUNK:D_FF + (c + 1) * FFN_CHUNK])
        act = (jax.nn.silu(gate) * up).astype(BF16)
        part = _dot(act, w_out_ref[cols, :])
        if c == 0:
            acc_s[rows, :] = part
        else:
            acc_s[rows, :] += part
        return part[0:1, :]

    def finish(rows, after=None):
        o_ref[rows, :] = _residual_norm(x_ref[rows, :], acc_s[rows, :], ln_g_ref[...], ln_b_ref[...], after)

    first, second = slice(0, half), slice(half, 2 * half)
    xb_first = x_ref[first, :].astype(BF16)
    xb_second = x_ref[second, :].astype(BF16)
    for c in range(n_chunks):
        chunk(first, xb_first, c)
    for c in range(n_chunks):
        done = chunk(second, xb_second, c)
        if c < FFN_NORM_SLICES:
            finish(slice(c * slice_rows, (c + 1) * slice_rows), after=done)
    finish(second)


def _ffn(x2d, *params):
    n, d = x2d.shape
    tm = ROW_TILE
    assert n % tm == 0
    return pl.pallas_call(
        _ffn_kernel,
        grid=(n // tm,),
        in_specs=[pl.BlockSpec((tm, d), lambda i: (i, 0))] + [_resident(p) for p in params],
        out_specs=pl.BlockSpec((tm, d), lambda i: (i, 0)),
        out_shape=jax.ShapeDtypeStruct((n, d), F32),
        scratch_shapes=[pltpu.VMEM((tm, d), F32)],
        compiler_params=_params("parallel"),
        name="ffn",
    )(x2d, *_stacks(*params))


LOG2_E = 1.4426950408889634
ALIBI_LANES = 3


def _moba_qkv_kernel(x_ref, w_f32_ref, fill_ref, q0_ref, q1_ref, k0_ref, k1_ref, vt_ref, kmean_ref, w_ref,
                     *, tiles_per_seq):
    _cast_weights_once(1, (w_f32_ref, w_ref))
    tm = x_ref.shape[0]
    xb = x_ref[...].astype(BF16)
    transposed = (((0,), (1,)), ((), ()))
    qt = lax.dot_general(w_ref[:, 0:D_MODEL], xb, transposed, preferred_element_type=F32)
    qt = qt * (MOBA_HDIM ** -0.5 * LOG2_E)
    k = _dot(xb, w_ref[:, D_MODEL:2 * D_MODEL])
    for i in range(tm // MOBA_BLOCK):
        kmean_ref[i] = jnp.mean(k[i * MOBA_BLOCK:(i + 1) * MOBA_BLOCK], axis=0, keepdims=True)
    vt_ref[0] = lax.dot_general(w_ref[:, 2 * D_MODEL:3 * D_MODEL], xb, transposed,
                                preferred_element_type=F32).astype(BF16)

    lane = lax.broadcasted_iota(jnp.int32, (tm, MOBA_PAIR), 1)
    pos = lax.broadcasted_iota(jnp.int32, (tm, MOBA_PAIR), 0) + (pl.program_id(0) % tiles_per_seq) * tm
    pos_block = ((pos // MOBA_BLOCK) * MOBA_BLOCK).astype(F32)
    pos_offset = (pos % MOBA_BLOCK).astype(F32)
    feature = lax.broadcasted_iota(jnp.int32, (MOBA_PAIR, tm), 0)
    for e, (q_ref, k_ref) in enumerate(((q0_ref, k0_ref), (q1_ref, k1_ref))):
        own = (lane < MOBA_HDIM) if e == 0 else (lane >= MOBA_HDIM)
        own_feature = (feature < MOBA_HDIM) if e == 0 else (feature >= MOBA_HDIM)
        partner = MOBA_HDIM * (1 - e)
        key_fill = jnp.where((lane >= partner) & (lane < partner + ALIBI_LANES), pos_block,
                             jnp.where((lane >= partner + ALIBI_LANES) & (lane < partner + 2 * ALIBI_LANES),
                                       pos_offset, 0.0))
        for p in range(MOBA_HEADS // 2):
            slab = slice(p * MOBA_PAIR, (p + 1) * MOBA_PAIR)
            q_fill = jnp.concatenate([fill_ref[e, slab, :]] * (tm // fill_ref.shape[-1]), axis=1)
            q_ref[0, slab, :] = jnp.where(own_feature, qt[slab, :], q_fill).astype(BF16)
            k_ref[:, slab] = jnp.where(own, k[:, slab], key_fill).astype(BF16)


def _moba_qkv(x2d, w_qkv, fill, bsz, t_len):
    n, d = x2d.shape
    tm = ROW_TILE
    tiles_per_seq = t_len // tm
    rows = pl.BlockSpec((tm, d), lambda i: (i, 0))
    feature_major = pl.BlockSpec((1, d, tm), lambda i: (i // tiles_per_seq, 0, i % tiles_per_seq))
    return pl.pallas_call(
        functools.partial(_moba_qkv_kernel, tiles_per_seq=tiles_per_seq),
        grid=(n // tm,),
        in_specs=[rows, _resident(w_qkv), _resident(fill)],
        out_specs=[
            feature_major, feature_major, rows, rows, feature_major,
            pl.BlockSpec((tm // MOBA_BLOCK, 1, d), lambda i: (i, 0, 0)),
        ],
        out_shape=[
            jax.ShapeDtypeStruct((bsz, d, t_len), BF16), jax.ShapeDtypeStruct((bsz, d, t_len), BF16),
            jax.ShapeDtypeStruct((n, d), BF16), jax.ShapeDtypeStruct((n, d), BF16),
            jax.ShapeDtypeStruct((bsz, d, t_len), BF16),
            jax.ShapeDtypeStruct((n // MOBA_BLOCK, 1, d), F32),
        ],
        scratch_shapes=[_bf16_copy(w_qkv)],
        compiler_params=_params("arbitrary"),
        name="moba_qkv",
    )(x2d, *_stacks(w_qkv, fill))


def _alibi_query_fill():
    slopes = jnp.asarray([2.0 ** (-8.0 * (h + 1) / MOBA_HEADS) for h in range(MOBA_HEADS)], F32) * LOG2_E
    pieces = []
    rest = slopes
    for _ in range(ALIBI_LANES):
        piece = rest.astype(BF16).astype(F32)
        pieces.append(piece)
        rest = rest - piece
    pieces = jnp.stack(pieces + pieces, axis=1)
    fill = jnp.zeros((2, MOBA_HEADS // 2, MOBA_PAIR), F32)
    for e in range(2):
        partner = MOBA_HDIM * (1 - e)
        fill = fill.at[e, :, partner:partner + 2 * ALIBI_LANES].set(pieces[e::2])
    return jnp.broadcast_to(fill.reshape(2, D_MODEL, 1), (2, D_MODEL, MOBA_PAIR))


def _moba_attn_kernel(x_ref, q0_ref, q1_ref, k0_ref, k1_ref, vt_ref, kmean_ref, w_out_ref, ln_g_ref, ln_b_ref,
                      o_ref, sa_s, sb_s, sha_s, shb_s, pa_s, pb_s, att_s, *, first_block, n_query_blocks):
    n_blocks = kmean_ref.shape[1]
    blk = MOBA_BLOCK
    q_refs = (q0_ref, q1_ref)
    k_refs = (k0_ref, k1_ref)
    blk_i = lax.broadcasted_iota(jnp.int32, (n_blocks, blk), 0)
    mean_lane = lax.broadcasted_iota(jnp.int32, (n_blocks, MOBA_PAIR), 1)

    def attend(n_past, query_rows, att_buf, buffers_read):
        n_keys = (n_past + 1) * blk
        select = n_past > MOBA_TOPK

        def pair_lanes(hp):
            return pl.ds(pl.multiple_of(hp * MOBA_PAIR, MOBA_PAIR), MOBA_PAIR)

        def scores_and_shifts(hp, s_buf, sh_buf, after=None):
            lanes = pair_lanes(hp)
            for e in range(2):
                cols = slice(e * blk, (e + 1) * blk)
                q_e = q_refs[e][0, lanes, query_rows]
                if after is not None:
                    q_e = (q_e.astype(F32) + _zero_after(after)).astype(BF16)
                if select:
                    own = (mean_lane < MOBA_HDIM) if e == 0 else (mean_lane >= MOBA_HDIM)
                    kmean = jnp.where(own, kmean_ref[0, :, lanes], 0.0)
                    kmean_hi = kmean.astype(BF16)
                    kmean_lo = (kmean - kmean_hi.astype(F32)).astype(BF16)
                    aff = _dot(kmean_hi, q_e) + _dot(kmean_lo, q_e)
                    rank = jnp.zeros((n_blocks, blk), F32)
                    for jp in range(n_past):
                        other = aff[jp:jp + 1, :]
                        beats = (other > aff) | ((other == aff) & (jp < blk_i))
                        rank = rank + jnp.where(beats, 1.0, 0.0)
                    chosen = rank < MOBA_TOPK
                m = None
                for j in range(n_past + 1):
                    rows = slice(j * blk, (j + 1) * blk)
                    t = _dot(k_refs[e][0, rows, lanes], q_e)
                    if j == n_past:
                        key_i = lax.broadcasted_iota(jnp.int32, (blk, blk), 0)
                        qry_i = lax.broadcasted_iota(jnp.int32, (blk, blk), 1)
                        t = jnp.where(key_i <= qry_i, t, MASKED)
                    s_buf[rows, cols] = t
                    m_j = jnp.max(t, axis=0, keepdims=True)
                    if select and j < n_past:
                        m_j = jnp.where(chosen[j:j + 1, :], m_j, MASKED)
                    m = m_j if m is None else jnp.maximum(m, m_j)
                shifts = jnp.broadcast_to(m, (n_blocks, blk))
                if select:
                    shifts = jnp.where(chosen | (blk_i == n_past), shifts, -MASKED)
                sh_buf[e] = shifts

        def probabilities(s_buf, sh_buf, p_buf, after=None):
            hold = 0.0 if after is None else _zero_after(after)
            for e in range(2):
                cols = slice(e * blk, (e + 1) * blk)
                for j in range(n_past + 1):
                    rows = slice(j * blk, (j + 1) * blk)
                    shift = sh_buf[e, j:j + 1, :] + hold
                    p_buf[e, rows, :] = jnp.exp2(s_buf[rows, cols] - shift).astype(BF16)

        def weighted_values(hp, p_buf):
            for e in range(2):
                feat = pl.ds(pl.multiple_of(hp * MOBA_PAIR + e * MOBA_HDIM, MOBA_HDIM), MOBA_HDIM)
                ones = jnp.ones((BF16_SUBLANES, n_keys), BF16)
                values = jnp.concatenate([vt_ref[0, feat, 0:n_keys], ones], axis=0)
                acc = _dot(values, p_buf[e, 0:n_keys, :])
                denom = acc[MOBA_HDIM:MOBA_HDIM + 1]
                att_buf[feat, :] = acc[0:MOBA_HDIM] / denom
            return denom[:, 0:1]

        n_pairs = MOBA_HEADS // 2
        read_a, read_b = buffers_read
        scores_and_shifts(0, sa_s, sha_s)
        scores_and_shifts(1, sb_s, shb_s, after=read_b)
        probabilities(sa_s, sha_s, pa_s, after=read_a)

        def two_pairs(i, carry):
            hp = 2 * i + 1
            scores_and_shifts(hp + 1, sa_s, sha_s)
            probabilities(sb_s, shb_s, pb_s)
            consumed = weighted_values(hp - 1, pa_s)
            scores_and_shifts(hp + 2, sb_s, shb_s)
            probabilities(sa_s, sha_s, pa_s, after=consumed)
            weighted_values(hp, pb_s)
            return carry

        lax.fori_loop(0, n_pairs // 2 - 1, two_pairs, 0)
        probabilities(sb_s, shb_s, pb_s)
        return weighted_values(n_pairs - 2, pa_s), weighted_values(n_pairs - 1, pb_s)

    buffers_read = (None, None)
    for j in range(n_query_blocks):
        query_rows = slice(j * blk, (j + 1) * blk)
        buffers_read = attend(first_block + j, query_rows, att_s.at[j], buffers_read)
        att = att_s[j].T.astype(BF16)
        y = _dot(att, w_out_ref[...])
        o_ref[0, query_rows, :] = _residual_norm(x_ref[0, query_rows, :], y, ln_g_ref[...], ln_b_ref[...])


def _moba_attn(x, first_block, n_query_blocks, q0, q1, k0, k1, vt, kmean, *params):
    bsz, t_len, d = x.shape
    blk = MOBA_BLOCK
    n_blocks = t_len // blk
    n_keys = (first_block + n_query_blocks) * blk
    assert first_block % n_query_blocks == 0
    query_rows = pl.BlockSpec((1, n_query_blocks * blk, d), lambda b: (b, first_block // n_query_blocks, 0))
    query_cols = pl.BlockSpec((1, d, n_query_blocks * blk), lambda b: (b, 0, first_block // n_query_blocks))
    visible_keys = pl.BlockSpec((1, n_keys, d), lambda b: (b, 0, 0))
    return pl.pallas_call(
        functools.partial(_moba_attn_kernel, first_block=first_block, n_query_blocks=n_query_blocks),
        grid=(bsz,),
        in_specs=[
            query_rows, query_cols, query_cols, visible_keys, visible_keys,
            pl.BlockSpec((1, d, n_keys), lambda b: (b, 0, 0)),
            pl.BlockSpec((1, n_blocks, d), lambda b: (b, 0, 0)),
        ] + [_resident(p) for p in params],
        out_specs=query_rows,
        out_shape=jax.ShapeDtypeStruct(x.shape, F32),
        input_output_aliases={0: 0},
        scratch_shapes=[
            pltpu.VMEM((n_keys, 2 * blk), F32),
            pltpu.VMEM((n_keys, 2 * blk), F32),
            pltpu.VMEM((2, n_blocks, blk), F32),
            pltpu.VMEM((2, n_blocks, blk), F32),
            pltpu.VMEM((2, n_keys, blk), BF16),
            pltpu.VMEM((2, n_keys, blk), BF16),
            pltpu.VMEM((n_query_blocks, d, blk), F32),
        ],
        compiler_params=_params("arbitrary"),
        name=f"moba_attn_from_block_{first_block}",
    )(x, q0, q1, k0, k1, vt, kmean, *_stacks(*params))


def kernel(x, mem, ln_g, ln_b, x_wq, x_wkv, x_wo, ffn_w_in, ffn_w_out, ev_w_in, ev_w_out, a_ws, a_bs,
           a_ln_g, a_ln_b, b_norm_g, hgrn_lb_logits, od_w_qkv, od_w_out):
    bsz, t_len, d = x.shape
    assert d == D_MODEL and t_len % ROW_TILE == 0 and t_len % MOBA_BLOCK == 0
    assert ROW_TILE % MOBA_BLOCK == 0 and t_len // MOBA_BLOCK > 1
    n = bsz * t_len
    mem2d = mem.reshape(bsz * mem.shape[1], d)

    def rows_of(v):
        return v.reshape(-1, 1, v.shape[-1])

    def seq(v):
        return v.reshape(bsz, t_len, d)

    ffn_w_in, ffn_w_out = ffn_w_in.astype(BF16), ffn_w_out.astype(BF16)
    od_w_out = od_w_out.astype(BF16)
    ln_g, ln_b = rows_of(ln_g), rows_of(ln_b)
    a_ln_g, a_ln_b, b_norm_g = rows_of(a_ln_g), rows_of(a_ln_b), rows_of(b_norm_g)
    a_bs_t = jnp.swapaxes(a_bs, 1, 2)
    lb_logits = hgrn_lb_logits[None]
    alibi_fill = _alibi_query_fill()[None]

    for layer in range(DEPTH):
        j = layer // 2
        norm = [(ln_g, 3 * layer), (ln_b, 3 * layer)]
        if layer % 2 == 0:
            x = _even_mixer(
                x, j, (ev_w_in, j), (ev_w_out, j), (a_ws, j), (a_bs_t, j), (a_ln_g, j), (a_ln_b, j),
                (b_norm_g, j), (lb_logits, 0), *norm)
        else:
            q0, q1, k0, k1, vt, kmean = _moba_qkv(x.reshape(n, d), (od_w_qkv, j), (alibi_fill, 0), bsz, t_len)
            attn_args = (q0, q1, seq(k0), seq(k1), vt,
                         kmean.reshape(bsz, t_len // MOBA_BLOCK, d), (od_w_out, j), *norm)
            n_blocks = t_len // MOBA_BLOCK
            for first_block in range(0, n_blocks, MOBA_BLOCKS_PER_CALL):
                x = _moba_attn(x, first_block, min(MOBA_BLOCKS_PER_CALL, n_blocks - first_block), *attn_args)
        kv = _mem_kv(mem2d, (x_wkv, layer)).reshape(bsz, mem.shape[1], 2 * d)
        x = _cross_attn(x, kv, (x_wq, layer), (x_wo, layer), (ln_g, 3 * layer + 1), (ln_b, 3 * layer + 1))
        x = _ffn(x.reshape(n, d), (ffn_w_in, layer), (ffn_w_out, layer),
                 (ln_g, 3 * layer + 2), (ln_b, 3 * layer + 2)).reshape(bsz, t_len, d)
    return x
```

```python
import functools
import math

import jax
import jax.numpy as jnp
from jax import lax
from jax.experimental import pallas as pl
from jax.experimental.pallas import tpu as pltpu

D_MODEL = 1024
DEPTH = 2
ALPHA = (2.0 * DEPTH) ** 0.25
LN_EPS = 1e-5

GMLP_WIDTH = D_MODEL // 2
GMLP_GROUPS = 4
GMLP_GDIM = GMLP_WIDTH // GMLP_GROUPS
GMLP_CHUNK = 128
HGRN_WIDTH = D_MODEL // 2
HGRN_HEADS = 4
HGRN_DK = HGRN_WIDTH // HGRN_HEADS
HGRN_CHUNK = 64
EVEN_IN_WIDTH = 2 * GMLP_WIDTH + 4 * HGRN_WIDTH

MOBA_HEADS = 16
MOBA_HDIM = D_MODEL // MOBA_HEADS
MOBA_BLOCK = 256
MOBA_TOPK = 3
MOBA_PAIR = 2 * MOBA_HDIM
MOBA_BLOCKS_PER_CALL = 2

MEM_HEADS = 4
MEM_HDIM = D_MODEL // MEM_HEADS

D_FF = int(math.ceil(8 * D_MODEL / 3 / 256)) * 256
FFN_CHUNK = 256

ROW_TILE = 1024
MEM_ROW_TILE = 512
EVEN_ROW_TILE = 512
FFN_NORM_SLICES = 8
V7X_VMEM_LIMIT = 56 * 1024 * 1024

MASKED = -1e30

BF16 = jnp.bfloat16
F32 = jnp.float32
BF16_SUBLANES = 16


def _dot(a, b):
    return jnp.dot(a, b, preferred_element_type=F32)


def _dot_nt(a, b):
    return lax.dot_general(a, b, (((1,), (1,)), ((), ())), preferred_element_type=F32)


def _dot_tn(a, b):
    return lax.dot_general(a, b, (((0,), (0,)), ((), ())), preferred_element_type=F32)


def _layer_norm(z, g, b):
    mu = jnp.mean(z, axis=-1, keepdims=True)
    zc = z - mu
    var = jnp.mean(zc * zc, axis=-1, keepdims=True)
    return zc * lax.rsqrt(var + LN_EPS) * g + b


def _zero_after(token):
    bits = lax.bitcast_convert_type(token, jnp.int32)
    return lax.shift_right_logical(lax.shift_right_logical(bits, 16), 16).astype(F32)


def _residual_norm(x, y, g, b, after=None):
    alpha = ALPHA if after is None else ALPHA + _zero_after(after)
    return _layer_norm(x * alpha + y, g, b)


def _resident(picked):
    stack, index = picked
    return pl.BlockSpec((None,) + stack.shape[1:], lambda *_: (index,) + (0,) * (stack.ndim - 1),
                        pipeline_mode=pl.Buffered(1))


def _stacks(*picked):
    return [stack for stack, _ in picked]


def _bf16_copy(picked):
    stack, _ = picked
    return pltpu.VMEM(stack.shape[1:], BF16)


def _cast_weights_once(n_grid_axes, *pairs):
    first = pl.program_id(0) == 0
    for axis in range(1, n_grid_axes):
        first = jnp.logical_and(first, pl.program_id(axis) == 0)

    @pl.when(first)
    def _():
        for src, dst in pairs:
            dst[...] = src[...].astype(BF16)


def _params(*semantics):
    return pltpu.CompilerParams(dimension_semantics=semantics, vmem_limit_bytes=V7X_VMEM_LIMIT)


def _even_mixer_kernel(x_ref, w_in_f32_ref, w_out_f32_ref, ws_ref, bs_ref, aln_g_ref, aln_b_ref, bnorm_ref,
                       lb_logits_ref, ln_g_ref, ln_b_ref, o_ref,
                       q_s, f_s, i_s, g_s, y_s, state_s, w_in_ref, w_out_ref, *, lb_index):
    tm = x_ref.shape[1]
    _cast_weights_once(2, (w_in_f32_ref, w_in_ref), (w_out_f32_ref, w_out_ref))
    x = x_ref[0]
    xb = x.astype(BF16)

    @pl.when(pl.program_id(1) == 0)
    def _():
        state_s[...] = jnp.zeros_like(state_s)

    base = 2 * GMLP_WIDTH
    u_pre = _dot(xb, w_in_ref[:, 0:GMLP_WIDTH])
    v_pre = _dot(xb, w_in_ref[:, GMLP_WIDTH:2 * GMLP_WIDTH])
    q_pre = _dot(xb, w_in_ref[:, base:base + HGRN_WIDTH])
    q_s[...] = q_pre
    f_pre = _dot(xb, w_in_ref[:, base + HGRN_WIDTH:base + 2 * HGRN_WIDTH])
    f_s[...] = f_pre
    i_s[...] = jax.nn.silu(_dot(xb, w_in_ref[:, base + 2 * HGRN_WIDTH:base + 3 * HGRN_WIDTH]))
    g_s[...] = jax.nn.silu(_dot(xb, w_in_ref[:, base + 3 * HGRN_WIDTH:base + 4 * HGRN_WIDTH]))

    u = jax.nn.gelu(u_pre + _zero_after(q_pre[0:1, 0:1]))
    v = jax.nn.gelu(v_pre + _zero_after(f_pre[0:1, 0:1]))
    row = lax.broadcasted_iota(jnp.int32, (GMLP_CHUNK, GMLP_CHUNK), 0)
    col = lax.broadcasted_iota(jnp.int32, (GMLP_CHUNK, GMLP_CHUNK), 1)
    for g in range(GMLP_GROUPS):
        lanes = slice(g * GMLP_GDIM, (g + 1) * GMLP_GDIM)
        vn = _layer_norm(v[:, lanes], aln_g_ref[:, lanes], aln_b_ref[:, lanes]).astype(BF16)
        wg = jnp.where(col <= row, ws_ref[g], 0.0).astype(BF16)
        bias = bs_ref[:, g:g + 1]
        for c in range(tm // GMLP_CHUNK):
            rows = slice(c * GMLP_CHUNK, (c + 1) * GMLP_CHUNK)
            s = _dot(wg, vn[rows]) + bias
            y_s[rows, lanes] = (u[rows, lanes] * s).astype(BF16)

    n_chunks = tm // HGRN_CHUNK
    logits = lb_logits_ref[...]
    e = jnp.exp(logits - jnp.max(logits, axis=0, keepdims=True))
    lb = jnp.sum(e[0:lb_index + 1], axis=0, keepdims=True) / jnp.sum(e, axis=0, keepdims=True)

    crow =lax.broadcasted_iota(jnp.int32, (HGRN_CHUNK, HGRN_CHUNK), 0)
    ccol = lax.broadcasted_iota(jnp.int32, (HGRN_CHUNK, HGRN_CHUNK), 1)
    causal = ccol <= crow
    tril_ones = jnp.where(causal, 1.0, 0.0).astype(BF16)
    norm_g = bnorm_ref[...]

    for c in range(n_chunks):
        rows = slice(c * HGRN_CHUNK, (c + 1) * HGRN_CHUNK)
        f = lb + (1.0 - lb) * jax.nn.sigmoid(f_s[rows, :])
        log_f = jnp.log(f)
        log_f_hi = log_f.astype(BF16)
        log_f_lo = (log_f - log_f_hi.astype(F32)).astype(BF16)
        cum = _dot(tril_ones, log_f_hi) + _dot(tril_ones, log_f_lo)
        chunk_decay = jnp.exp(cum[HGRN_CHUNK - 1:HGRN_CHUNK, :])
        k_back = (1.0 - f) * jnp.exp(-cum)
        q_dec = (q_s[rows, :] * jnp.exp(cum)).astype(BF16)
        k_dec = k_back.astype(BF16)
        k_tail = (k_back * chunk_decay).astype(BF16)
        val = i_s[rows, :].astype(BF16)
        gate = g_s[rows, :]
        for h in range(HGRN_HEADS):
            lanes = slice(h * HGRN_DK, (h + 1) * HGRN_DK)
            attn = jnp.where(causal, _dot_nt(q_dec[:, lanes], k_dec[:, lanes]), 0.0).astype(BF16)
            state_t = state_s[h]
            o = _dot(attn, val[:, lanes]) + _dot(q_dec[:, lanes], state_t.T.astype(BF16))
            state_s[h] = state_t * chunk_decay[:, lanes] + _dot_tn(val[:, lanes], k_tail[:, lanes])
            rms = lax.rsqrt(jnp.mean(o * o, axis=-1, keepdims=True) + LN_EPS)
            y_b = o * rms * norm_g[:, lanes] * gate[:, lanes]
            y_s[rows, GMLP_WIDTH + h * HGRN_DK:GMLP_WIDTH + (h + 1) * HGRN_DK] = y_b.astype(BF16)

    for rows in (slice(0, tm // 2), slice(tm // 2, tm)):
        y = _dot(y_s[rows, :], w_out_ref[...])
        o_ref[0, rows, :] = _residual_norm(x_ref[0, rows, :], y, ln_g_ref[...], ln_b_ref[...])


def _even_mixer(x, lb_index, *params):
    bsz, t_len, d = x.shape
    tm = EVEN_ROW_TILE
    assert t_len % tm == 0
    kern = functools.partial(_even_mixer_kernel, lb_index=lb_index)
    return pl.pallas_call(
        kern,
        grid=(bsz, t_len // tm),
        in_specs=[pl.BlockSpec((1, tm, d), lambda b, t: (b, t, 0))] + [_resident(p) for p in params],
        out_specs=pl.BlockSpec((1, tm, d), lambda b, t: (b, t, 0)),
        out_shape=jax.ShapeDtypeStruct(x.shape, F32),
        scratch_shapes=[
            pltpu.VMEM((tm, HGRN_WIDTH), F32), pltpu.VMEM((tm, HGRN_WIDTH), F32),
            pltpu.VMEM((tm, HGRN_WIDTH), F32), pltpu.VMEM((tm, HGRN_WIDTH), F32),
            pltpu.VMEM((tm, GMLP_WIDTH + HGRN_WIDTH), BF16),
            pltpu.VMEM((HGRN_HEADS, HGRN_DK, HGRN_DK), F32),
            _bf16_copy(params[0]), _bf16_copy(params[1]),
        ],
        compiler_params=_params("arbitrary", "arbitrary"),
        name="even_mixer",
    )(x, *_stacks(*params))


def _mem_kv_kernel(mem_ref, w_f32_ref, o_ref, w_ref):
    _cast_weights_once(1, (w_f32_ref, w_ref))
    o_ref[...] = _dot(mem_ref[...].astype(BF16), w_ref[...]).astype(BF16)


def _mem_kv(mem2d, w_kv):
    n, d = mem2d.shape
    tm = MEM_ROW_TILE
    assert n % tm == 0
    width = w_kv[0].shape[-1]
    return pl.pallas_call(
        _mem_kv_kernel,
        grid=(n // tm,),
        in_specs=[pl.BlockSpec((tm, d), lambda i: (i, 0)), _resident(w_kv)],
        out_specs=pl.BlockSpec((tm, width), lambda i: (i, 0)),
        out_shape=jax.ShapeDtypeStruct((n, width), BF16),
        scratch_shapes=[_bf16_copy(w_kv)],
        compiler_params=_params("arbitrary"),
        name="mem_kv",
    )(mem2d, *_stacks(w_kv))


def _cross_attn_kernel(x_ref, kv_ref, wq_f32_ref, wo_f32_ref, ln_g_ref, ln_b_ref, o_ref, att_s, y_s,
                       wq_ref, wo_ref):
    _cast_weights_once(2, (wq_f32_ref, wq_ref), (wo_f32_ref, wo_ref))
    half = x_ref.shape[1] // 2
    slice_rows = half // MEM_HEADS

    def attend(rows):
        q = (_dot(x_ref[0, rows, :].astype(BF16), wq_ref[...]) * (MEM_HDIM ** -0.5)).astype(BF16)
        done = []
        for h in range(MEM_HEADS):
            lanes = slice(h * MEM_HDIM, (h + 1) * MEM_HDIM)
            k_h = kv_ref[0, :, h * MEM_HDIM:(h + 1) * MEM_HDIM]
            v_h = kv_ref[0, :, D_MODEL + h * MEM_HDIM:D_MODEL + (h + 1) * MEM_HDIM]
            s = _dot_nt(q[:, lanes], k_h)
            p = jnp.exp(s - jnp.max(s, axis=-1, keepdims=True))
            denom = jnp.sum(p, axis=-1, keepdims=True)
            weighted = _dot(p.astype(BF16), v_h)
            att_s[rows, lanes] = (weighted / denom).astype(BF16)
            done.append(weighted[0:1, 0:1])
        y_s[rows, :] = _dot(att_s[rows, :], wo_ref[...])
        return done

    def finish(rows, after=None):
        o_ref[0, rows, :] = _residual_norm(x_ref[0, rows, :], y_s[rows, :], ln_g_ref[...], ln_b_ref[...], after)

    attend(slice(0, half))
    done = attend(slice(half, 2 * half))
    for h in range(MEM_HEADS):
        finish(slice(h * slice_rows, (h + 1) * slice_rows), after=done[h])
    finish(slice(half, 2 * half))


def _cross_attn(x, kv, *params):
    bsz, t_len, d = x.shape
    tm = ROW_TILE
    return pl.pallas_call(
        _cross_attn_kernel,
        grid=(bsz, t_len // tm),
        in_specs=[
            pl.BlockSpec((1, tm, d), lambda b, t: (b, t, 0)),
            pl.BlockSpec((1,) + kv.shape[1:], lambda b, t: (b, 0, 0)),
        ] + [_resident(p) for p in params],
        out_specs=pl.BlockSpec((1, tm, d), lambda b, t: (b, t, 0)),
        out_shape=jax.ShapeDtypeStruct(x.shape, F32),
        scratch_shapes=[pltpu.VMEM((tm, d), BF16), pltpu.VMEM((tm, d), F32),
                        _bf16_copy(params[0]), _bf16_copy(params[1])],
        compiler_params=_params("arbitrary", "arbitrary"),
        name="cross_attn",
    )(x, kv, *_stacks(*params))


def _ffn_kernel(x_ref, w_in_ref, w_out_ref, ln_g_ref, ln_b_ref, o_ref, acc_s):
    half = x_ref.shape[0] // 2
    n_chunks = D_FF // FFN_CHUNK
    slice_rows = half // FFN_NORM_SLICES

    def chunk(rows, xb, c):
        cols = slice(c * FFN_CHUNK, (c + 1) * FFN_CHUNK)
        gate = _dot(xb, w_in_ref[:, cols])
        up = _dot(xb, w_in_ref[:, D_FF + c * FFN_CHUNK:D_FF + (c + 1) * FFN_CHUNK])
        act = (jax.nn.silu(gate) * up).astype(BF16)
        part = _dot(act, w_out_ref[cols, :])
        if c == 0:
            acc_s[rows, :] = part
        else:
            acc_s[rows, :] += part
        return part[0:1, :]

    def finish(rows, after=None):
        o_ref[rows, :] = _residual_norm(x_ref[rows, :], acc_s[rows, :], ln_g_ref[...], ln_b_ref[...], after)

    first, second = slice(0, half), slice(half, 2 * half)
    xb_first = x_ref[first, :].astype(BF16)
    xb_second = x_ref[second, :].astype(BF16)
    for c in range(n_chunks):
        chunk(first, xb_first, c)
    for c in range(n_chunks):
        done = chunk(second, xb_second, c)
        if c < FFN_NORM_SLICES:
            finish(slice(c * slice_rows, (c + 1) * slice_rows), after=done)
    finish(second)


def _ffn(x2d, *params):
    n, d = x2d.shape
    tm = ROW_TILE
    assert n % tm == 0
    return pl.pallas_call(
        _ffn_kernel,
        grid=(n // tm,),
        in_specs=[pl.BlockSpec((tm, d), lambda i: (i, 0))] + [_resident(p) for p in params],
        out_specs=pl.BlockSpec((tm, d), lambda i: (i, 0)),
        out_shape=jax.ShapeDtypeStruct((n, d), F32),
        scratch_shapes=[pltpu.VMEM((tm, d), F32)],
        compiler_params=_params("parallel"),
        name="ffn",
    )(x2d, *_stacks(*params))


LOG2_E = 1.4426950408889634
ALIBI_LANES = 3


def _moba_qkv_kernel(x_ref, w_f32_ref, fill_ref, q0_ref, q1_ref, k0_ref, k1_ref, vt_ref, kmean_ref, w_ref,
                     *, tiles_per_seq):
    @pl.when(pl.program_id(0) == 0)
    def _():
        w_ref[:, 0:D_MODEL] = w_f32_ref[:, 0:D_MODEL].T.astype(BF16)
        w_ref[:, D_MODEL:2 * D_MODEL] = w_f32_ref[:, D_MODEL:2 * D_MODEL].astype(BF16)
        w_ref[:, 2 * D_MODEL:3 * D_MODEL] = w_f32_ref[:, 2 * D_MODEL:3 * D_MODEL].T.astype(BF16)

    tm = x_ref.shape[0]
    x = x_ref[...]
    xb = x.astype(BF16)
    xt = x.T.astype(BF16)
    qt = _dot(w_ref[:, 0:D_MODEL], xt) * (MOBA_HDIM ** -0.5 * LOG2_E)
    k = _dot(xb, w_ref[:, D_MODEL:2 * D_MODEL])
    for i in range(tm // MOBA_BLOCK):
        kmean_ref[i] = jnp.mean(k[i * MOBA_BLOCK:(i + 1) * MOBA_BLOCK], axis=0, keepdims=True)
    vt_ref[0] = _dot(w_ref[:, 2 * D_MODEL:3 * D_MODEL], xt).astype(BF16)

    lane = lax.broadcasted_iota(jnp.int32, (tm, MOBA_PAIR), 1)
    pos = lax.broadcasted_iota(jnp.int32, (tm, MOBA_PAIR), 0) + (pl.program_id(0) % tiles_per_seq) * tm
    pos_block = ((pos // MOBA_BLOCK) * MOBA_BLOCK).astype(F32)
    pos_offset = (pos % MOBA_BLOCK).astype(F32)
    feature = lax.broadcasted_iota(jnp.int32, (MOBA_PAIR, tm), 0)
    for e, (q_ref, k_ref) in enumerate(((q0_ref, k0_ref), (q1_ref, k1_ref))):
        own = (lane < MOBA_HDIM) if e == 0 else (lane >= MOBA_HDIM)
        own_feature = (feature < MOBA_HDIM) if e == 0 else (feature >= MOBA_HDIM)
        partner = MOBA_HDIM * (1 - e)
        key_fill = jnp.where((lane >= partner) & (lane < partner + ALIBI_LANES), pos_block,
                             jnp.where((lane >= partner + ALIBI_LANES) & (lane < partner + 2 * ALIBI_LANES),
                                       pos_offset, 0.0))
        for p in range(MOBA_HEADS // 2):
            slab = slice(p * MOBA_PAIR, (p + 1) * MOBA_PAIR)
            q_fill = jnp.concatenate([fill_ref[e, slab, :]] * (tm // fill_ref.shape[-1]), axis=1)
            q_ref[0, slab, :] = jnp.where(own_feature, qt[slab, :], q_fill).astype(BF16)
            k_ref[:, slab] = jnp.where(own, k[:, slab], key_fill).astype(BF16)


def _moba_qkv(x2d, w_qkv, fill, bsz, t_len):
    n, d = x2d.shape
    tm = ROW_TILE
    tiles_per_seq = t_len // tm
    rows = pl.BlockSpec((tm, d), lambda i: (i, 0))
    feature_major = pl.BlockSpec((1, d, tm), lambda i: (i // tiles_per_seq, 0, i % tiles_per_seq))
    return pl.pallas_call(
        functools.partial(_moba_qkv_kernel, tiles_per_seq=tiles_per_seq),
        grid=(n // tm,),
        in_specs=[rows, _resident(w_qkv), _resident(fill)],
        out_specs=[
            feature_major, feature_major, rows, rows, feature_major,
            pl.BlockSpec((tm // MOBA_BLOCK, 1, d), lambda i: (i, 0, 0)),
        ],
        out_shape=[
            jax.ShapeDtypeStruct((bsz, d, t_len), BF16), jax.ShapeDtypeStruct((bsz, d, t_len), BF16),
            jax.ShapeDtypeStruct((n, d), BF16), jax.ShapeDtypeStruct((n, d), BF16),
            jax.ShapeDtypeStruct((bsz, d, t_len), BF16),
            jax.ShapeDtypeStruct((n // MOBA_BLOCK, 1, d), F32),
        ],
        scratch_shapes=[_bf16_copy(w_qkv)],
        compiler_params=_params("arbitrary"),
        name="moba_qkv",
    )(x2d, *_stacks(w_qkv, fill))


def _alibi_query_fill():
    slopes = jnp.asarray([2.0 ** (-8.0 * (h + 1) / MOBA_HEADS) for h in range(MOBA_HEADS)], F32) * LOG2_E
    pieces = []
    rest = slopes
    for _ in range(ALIBI_LANES):
        piece = rest.astype(BF16).astype(F32)
        pieces.append(piece)
        rest = rest - piece
    pieces = jnp.stack(pieces + pieces, axis=1)
    fill = jnp.zeros((2, MOBA_HEADS // 2, MOBA_PAIR), F32)
    for e in range(2):
        partner = MOBA_HDIM * (1 - e)
        fill = fill.at[e, :, partner:partner + 2 * ALIBI_LANES].set(pieces[e::2])
    return jnp.broadcast_to(fill.reshape(2, D_MODEL, 1), (2, D_MODEL, MOBA_PAIR))


def _moba_attn_kernel(x_ref, q0_ref, q1_ref, k0_ref, k1_ref, vt_ref, kmean_ref, w_out_ref, ln_g_ref, ln_b_ref,
                      o_ref, sa_s, sb_s, sha_s, shb_s, pa_s, pb_s, att_s, *, first_block, n_query_blocks):
    n_blocks = kmean_ref.shape[1]
    blk = MOBA_BLOCK
    q_refs = (q0_ref, q1_ref)
    k_refs = (k0_ref, k1_ref)
    blk_i = lax.broadcasted_iota(jnp.int32, (n_blocks, blk), 0)
    mean_lane = lax.broadcasted_iota(jnp.int32, (n_blocks, MOBA_PAIR), 1)

    def attend(n_past, query_rows, att_buf, buffers_read):
        n_keys = (n_past + 1) * blk
        select = n_past > MOBA_TOPK

        def pair_lanes(hp):
            return pl.ds(pl.multiple_of(hp * MOBA_PAIR, MOBA_PAIR), MOBA_PAIR)

        def scores_and_shifts(hp, s_buf, sh_buf, after=None):
            lanes = pair_lanes(hp)
            for e in range(2):
                cols = slice(e * blk, (e + 1) * blk)
                q_e = q_refs[e][0, lanes, query_rows]
                if after is not None:
                    q_e = (q_e.astype(F32) + _zero_after(after)).astype(BF16)
                if select:
                    own = (mean_lane < MOBA_HDIM) if e == 0 else (mean_lane >= MOBA_HDIM)
                    kmean = jnp.where(own, kmean_ref[0, :, lanes], 0.0)
                    kmean_hi = kmean.astype(BF16)
                    kmean_lo = (kmean - kmean_hi.astype(F32)).astype(BF16)
                    aff = _dot(kmean_hi, q_e) + _dot(kmean_lo, q_e)
                    rank = jnp.zeros((n_blocks, blk), F32)
                    for jp in range(n_past):
                        other = aff[jp:jp + 1, :]
                        beats = (other > aff) | ((other == aff) & (jp < blk_i))
                        rank = rank + jnp.where(beats, 1.0, 0.0)
                    chosen = rank < MOBA_TOPK
                m = None
                for j in range(n_past + 1):
                    rows = slice(j * blk, (j + 1) * blk)
                    t = _dot(k_refs[e][0, rows, lanes], q_e)
                    if j == n_past:
                        key_i = lax.broadcasted_iota(jnp.int32, (blk, blk), 0)
                        qry_i = lax.broadcasted_iota(jnp.int32, (blk, blk), 1)
                        t = jnp.where(key_i <= qry_i, t, MASKED)
                    s_buf[rows, cols] = t
                    m_j = jnp.max(t, axis=0, keepdims=True)
                    if select and j < n_past:
                        m_j = jnp.where(chosen[j:j + 1, :], m_j, MASKED)
                    m = m_j if m is None else jnp.maximum(m, m_j)
                shifts = jnp.broadcast_to(m, (n_blocks, blk))
                if select:
                    shifts = jnp.where(chosen | (blk_i == n_past), shifts, -MASKED)
                sh_buf[e] = shifts

        def probabilities(s_buf, sh_buf, p_buf, after=None):
            hold = 0.0 if after is None else _zero_after(after)
            for e in range(2):
                cols = slice(e * blk, (e + 1) * blk)
                for j in range(n_past + 1):
                    rows = slice(j * blk, (j + 1) * blk)
                    shift = sh_buf[e, j:j + 1, :] + hold
                    p_buf[e, rows, :] = jnp.exp2(s_buf[rows, cols] - shift).astype(BF16)

        def weighted_values(hp, p_buf):
            for e in range(2):
                feat = pl.ds(pl.multiple_of(hp * MOBA_PAIR + e * MOBA_HDIM, MOBA_HDIM), MOBA_HDIM)
                ones = jnp.ones((BF16_SUBLANES, n_keys), BF16)
                values = jnp.concatenate([vt_ref[0, feat, 0:n_keys], ones], axis=0)
                acc = _dot(values, p_buf[e, 0:n_keys, :])
                denom = acc[MOBA_HDIM:MOBA_HDIM + 1]
                att_buf[feat, :] = acc[0:MOBA_HDIM] / denom
            return denom[:, 0:1]

        n_pairs = MOBA_HEADS // 2
        read_a, read_b = buffers_read
        scores_and_shifts(0, sa_s, sha_s)
        scores_and_shifts(1, sb_s, shb_s, after=read_b)
        probabilities(sa_s, sha_s, pa_s, after=read_a)

        def two_pairs(i, carry):
            hp = 2 * i + 1
            scores_and_shifts(hp + 1, sa_s, sha_s)
            probabilities(sb_s, shb_s, pb_s)
            consumed = weighted_values(hp - 1, pa_s)
            scores_and_shifts(hp + 2, sb_s, shb_s)
            probabilities(sa_s, sha_s, pa_s, after=consumed)
            weighted_values(hp, pb_s)
            return carry

        lax.fori_loop(0, n_pairs // 2 - 1, two_pairs, 0)
        probabilities(sb_s, shb_s, pb_s)
        return weighted_values(n_pairs - 2, pa_s), weighted_values(n_pairs - 1, pb_s)

    buffers_read = (None, None)
    for j in range(n_query_blocks):
        query_rows = slice(j * blk, (j + 1) * blk)
        buffers_read = attend(first_block + j, query_rows, att_s.at[j], buffers_read)
        att = att_s[j].T.astype(BF16)
        y = _dot(att, w_out_ref[...])
        o_ref[0, query_rows, :] = _residual_norm(x_ref[0, query_rows, :], y, ln_g_ref[...], ln_b_ref[...])


def _moba_attn(x, first_block, n_query_blocks, q0, q1, k0, k1, vt, kmean, *params):
    bsz, t_len, d = x.shape
    blk = MOBA_BLOCK
    n_blocks = t_len // blk
    n_keys = (first_block + n_query_blocks) * blk
    assert first_block % n_query_blocks == 0
    query_rows = pl.BlockSpec((1, n_query_blocks * blk, d), lambda b: (b, first_block // n_query_blocks, 0))
    query_cols = pl.BlockSpec((1, d, n_query_blocks * blk), lambda b: (b, 0, first_block // n_query_blocks))
    visible_keys = pl.BlockSpec((1, n_keys, d), lambda b: (b, 0, 0))
    return pl.pallas_call(
        functools.partial(_moba_attn_kernel, first_block=first_block, n_query_blocks=n_query_blocks),
        grid=(bsz,),
        in_specs=[
            query_rows, query_cols, query_cols, visible_keys, visible_keys,
            pl.BlockSpec((1, d, n_keys), lambda b: (b, 0, 0)),
            pl.BlockSpec((1, n_blocks, d), lambda b: (b, 0, 0)),
        ] + [_resident(p) for p in params],
        out_specs=query_rows,
        out_shape=jax.ShapeDtypeStruct(x.shape, F32),
        input_output_aliases={0: 0},
        scratch_shapes=[
            pltpu.VMEM((n_keys, 2 * blk), F32),
            pltpu.VMEM((n_keys, 2 * blk), F32),
            pltpu.VMEM((2, n_blocks, blk), F32),
            pltpu.VMEM((2, n_blocks, blk), F32),
            pltpu.VMEM((2, n_keys, blk), BF16),
            pltpu.VMEM((2, n_keys, blk), BF16),
            pltpu.VMEM((n_query_blocks, d, blk), F32),
        ],
        compiler_params=_params("arbitrary"),
        name=f"moba_attn_from_block_{first_block}",
    )(x, q0, q1, k0, k1, vt, kmean, *_stacks(*params))


def kernel(x, mem, ln_g, ln_b, x_wq, x_wkv, x_wo, ffn_w_in, ffn_w_out, ev_w_in, ev_w_out, a_ws, a_bs,
           a_ln_g, a_ln_b, b_norm_g, hgrn_lb_logits, od_w_qkv, od_w_out):
    bsz, t_len, d = x.shape
    assert d == D_MODEL and t_len % ROW_TILE == 0 and t_len % MOBA_BLOCK == 0
    assert ROW_TILE % MOBA_BLOCK == 0 and t_len // MOBA_BLOCK > 1
    n = bsz * t_len
    mem2d = mem.reshape(bsz * mem.shape[1], d)

    def rows_of(v):
        return v.reshape(-1, 1, v.shape[-1])

    def seq(v):
        return v.reshape(bsz, t_len, d)

    ffn_w_in, ffn_w_out = ffn_w_in.astype(BF16), ffn_w_out.astype(BF16)
    od_w_out = od_w_out.astype(BF16)
    ln_g, ln_b = rows_of(ln_g), rows_of(ln_b)
    a_ln_g, a_ln_b, b_norm_g = rows_of(a_ln_g), rows_of(a_ln_b), rows_of(b_norm_g)
    a_bs_t = jnp.swapaxes(a_bs, 1, 2)
    lb_logits = hgrn_lb_logits[None]
    alibi_fill = _alibi_query_fill()[None]

    for layer in range(DEPTH):
        j = layer // 2
        norm = [(ln_g, 3 * layer), (ln_b, 3 * layer)]
        if layer % 2 == 0:
            x = _even_mixer(
                x, j, (ev_w_in, j), (ev_w_out, j), (a_ws, j), (a_bs_t, j), (a_ln_g, j), (a_ln_b, j),
                (b_norm_g, j), (lb_logits, 0), *norm)
        else:
            q0, q1, k0, k1, vt, kmean = _moba_qkv(x.reshape(n, d), (od_w_qkv, j), (alibi_fill, 0), bsz, t_len)
            attn_args = (q0, q1, seq(k0), seq(k1), vt,
                         kmean.reshape(bsz, t_len // MOBA_BLOCK, d), (od_w_out, j), *norm)
            n_blocks = t_len // MOBA_BLOCK
            for first_block in range(0, n_blocks, MOBA_BLOCKS_PER_CALL):
                x = _moba_attn(x, first_block, min(MOBA_BLOCKS_PER_CALL, n_blocks - first_block), *attn_args)
        kv = _mem_kv(mem2d, (x_wkv, layer)).reshape(bsz, mem.shape[1], 2 * d)
        x = _cross_attn(x, kv, (x_wq, layer), (x_wo, layer), (ln_g, 3 * layer + 1), (ln_b, 3 * layer + 1))
        x = _ffn(x.reshape(n, d), (ffn_w_in, layer), (ffn_w_out, layer),
                 (ln_g, 3 * layer + 2), (ln_b, 3 * layer + 2)).reshape(bsz, t_len, d)
    return x
```

```python
import functools
import math

import jax
import jax.numpy as jnp
from jax import lax
from jax.experimental import pallas as pl
from jax.experimental.pallas import tpu as pltpu

D_MODEL = 1024
DEPTH = 2
ALPHA = (2.0 * DEPTH) ** 0.25
LN_EPS = 1e-5

GMLP_WIDTH = D_MODEL // 2
GMLP_GROUPS = 4
GMLP_GDIM = GMLP_WIDTH // GMLP_GROUPS
GMLP_CHUNK = 128
HGRN_WIDTH = D_MODEL // 2
HGRN_HEADS = 4
HGRN_DK = HGRN_WIDTH // HGRN_HEADS
HGRN_CHUNK = 64
EVEN_IN_WIDTH = 2 * GMLP_WIDTH + 4 * HGRN_WIDTH

MOBA_HEADS = 16
MOBA_HDIM = D_MODEL // MOBA_HEADS
MOBA_BLOCK = 256
MOBA_TOPK = 3
MOBA_PAIR = 2 * MOBA_HDIM
MOBA_BLOCKS_PER_CALL = 2

MEM_HEADS = 4
MEM_HDIM = D_MODEL // MEM_HEADS

D_FF = int(math.ceil(8 * D_MODEL / 3 / 256)) * 256
FFN_CHUNK = 256

ROW_TILE = 1024
MEM_ROW_TILE = 512
EVEN_ROW_TILE = 512
FFN_NORM_SLICES = 8
V7X_VMEM_LIMIT = 56 * 1024 * 1024

MASKED = -1e30

BF16 = jnp.bfloat16
F32 = jnp.float32
BF16_SUBLANES = 16


def _dot(a, b):
    return jnp.dot(a, b, preferred_element_type=F32)


def _dot_nt(a, b):
    return lax.dot_general(a, b, (((1,), (1,)), ((), ())), preferred_element_type=F32)


def _dot_tn(a, b):
    return lax.dot_general(a, b, (((0,), (0,)), ((), ())), preferred_element_type=F32)


def _layer_norm(z, g, b):
    mu = jnp.mean(z, axis=-1, keepdims=True)
    zc = z - mu
    var = jnp.mean(zc * zc, axis=-1, keepdims=True)
    return zc * lax.rsqrt(var + LN_EPS) * g + b


def _zero_after(token):
    bits = lax.bitcast_convert_type(token, jnp.int32)
    return lax.shift_right_logical(lax.shift_right_logical(bits, 16), 16).astype(F32)


def _residual_norm(x, y, g, b, after=None):
    alpha = ALPHA if after is None else ALPHA + _zero_after(after)
    return _layer_norm(x * alpha + y, g, b)


def _resident(picked):
    stack, index = picked
    return pl.BlockSpec((None,) + stack.shape[1:], lambda *_: (index,) + (0,) * (stack.ndim - 1),
                        pipeline_mode=pl.Buffered(1))


def _stacks(*picked):
    return [stack for stack, _ in picked]


def _bf16_copy(picked):
    stack, _ = picked
    return pltpu.VMEM(stack.shape[1:], BF16)


def _cast_weights_once(n_grid_axes, *pairs):
    first = pl.program_id(0) == 0
    for axis in range(1, n_grid_axes):
        first = jnp.logical_and(first, pl.program_id(axis) == 0)

    @pl.when(first)
    def _():
        for src, dst in pairs:
            dst[...] = src[...].astype(BF16)


def _params(*semantics):
    return pltpu.CompilerParams(dimension_semantics=semantics, vmem_limit_bytes=V7X_VMEM_LIMIT)


def _even_mixer_kernel(x_ref, w_in_f32_ref, w_out_f32_ref, ws_ref, bs_ref, aln_g_ref, aln_b_ref, bnorm_ref,
                       lb_logits_ref, ln_g_ref, ln_b_ref, o_ref,
                       q_s, f_s, i_s, g_s, y_s, state_s, w_in_ref, w_out_ref, *, lb_index):
    tm = x_ref.shape[1]
    _cast_weights_once(2, (w_in_f32_ref, w_in_ref), (w_out_f32_ref, w_out_ref))
    x = x_ref[0]
    xb = x.astype(BF16)

    @pl.when(pl.program_id(1) == 0)
    def _():
        state_s[...] = jnp.zeros_like(state_s)

    base = 2 * GMLP_WIDTH
    u_pre = _dot(xb, w_in_ref[:, 0:GMLP_WIDTH])
    v_pre = _dot(xb, w_in_ref[:, GMLP_WIDTH:2 * GMLP_WIDTH])
    q_pre = _dot(xb, w_in_ref[:, base:base + HGRN_WIDTH])
    q_s[...] = q_pre
    f_pre = _dot(xb, w_in_ref[:, base + HGRN_WIDTH:base + 2 * HGRN_WIDTH])
    f_s[...] = f_pre
    i_s[...] = jax.nn.silu(_dot(xb, w_in_ref[:, base + 2 * HGRN_WIDTH:base + 3 * HGRN_WIDTH]))
    g_s[...] = jax.nn.silu(_dot(xb, w_in_ref[:, base + 3 * HGRN_WIDTH:base + 4 * HGRN_WIDTH]))

    u = jax.nn.gelu(u_pre + _zero_after(q_pre[0:1, 0:1]))
    v = jax.nn.gelu(v_pre + _zero_after(f_pre[0:1, 0:1]))
    row = lax.broadcasted_iota(jnp.int32, (GMLP_CHUNK, GMLP_CHUNK), 0)
    col = lax.broadcasted_iota(jnp.int32, (GMLP_CHUNK, GMLP_CHUNK), 1)
    for g in range(GMLP_GROUPS):
        lanes = slice(g * GMLP_GDIM, (g + 1) * GMLP_GDIM)
        vn = _layer_norm(v[:, lanes], aln_g_ref[:, lanes], aln_b_ref[:, lanes]).astype(BF16)
        wg = jnp.where(col <= row, ws_ref[g], 0.0).astype(BF16)
        bias = bs_ref[:, g:g + 1]
        for c in range(tm // GMLP_CHUNK):
            rows = slice(c * GMLP_CHUNK, (c + 1) * GMLP_CHUNK)
            s = _dot(wg, vn[rows]) + bias
            y_s[rows, lanes] = (u[rows, lanes] * s).astype(BF16)

    n_chunks = tm // HGRN_CHUNK
    logits = lb_logits_ref[...]
    e = jnp.exp(logits - jnp.max(logits, axis=0, keepdims=True))
    lb = jnp.sum(e[0:lb_index + 1], axis=0, keepdims=True) / jnp.sum(e, axis=0, keepdims=True)

    crow =lax.broadcasted_iota(jnp.int32, (HGRN_CHUNK, HGRN_CHUNK), 0)
    ccol = lax.broadcasted_iota(jnp.int32, (HGRN_CHUNK, HGRN_CHUNK), 1)
    causal = ccol <= crow
    tril_ones = jnp.where(causal, 1.0, 0.0).astype(BF16)
    norm_g = bnorm_ref[...]

    for c in range(n_chunks):
        rows = slice(c * HGRN_CHUNK, (c + 1) * HGRN_CHUNK)
        f = lb + (1.0 - lb) * jax.nn.sigmoid(f_s[rows, :])
        log_f = jnp.log(f)
        log_f_hi = log_f.astype(BF16)
        log_f_lo = (log_f - log_f_hi.astype(F32)).astype(BF16)
        cum = _dot(tril_ones, log_f_hi) + _dot(tril_ones, log_f_lo)
        chunk_decay = jnp.exp(cum[HGRN_CHUNK - 1:HGRN_CHUNK, :])
        k_back = (1.0 - f) * jnp.exp(-cum)
        q_dec = (q_s[rows, :] * jnp.exp(cum)).astype(BF16)
        k_dec = k_back.astype(BF16)
        k_tail = (k_back * chunk_decay).astype(BF16)
        val = i_s[rows, :].astype(BF16)
        gate = g_s[rows, :]
        for h in range(HGRN_HEADS):
            lanes = slice(h * HGRN_DK, (h + 1) * HGRN_DK)
            attn = jnp.where(causal, _dot_nt(q_dec[:, lanes], k_dec[:, lanes]), 0.0).astype(BF16)
            state_t = state_s[h]
            o = _dot(jnp.concatenate([q_dec[:, lanes], attn], axis=1),
                     jnp.concatenate([state_t.T.astype(BF16), val[:, lanes]], axis=0))
            state_s[h] = state_t * chunk_decay[:, lanes] + _dot_tn(val[:, lanes], k_tail[:, lanes])
            rms = lax.rsqrt(jnp.mean(o * o, axis=-1, keepdims=True) + LN_EPS)
            y_b = o * rms * norm_g[:, lanes] * gate[:, lanes]
            y_s[rows, GMLP_WIDTH + h * HGRN_DK:GMLP_WIDTH + (h + 1) * HGRN_DK] = y_b.astype(BF16)

    for rows in (slice(0, tm // 2), slice(tm // 2, tm)):
        y = _dot(y_s[rows, :], w_out_ref[...])
        o_ref[0, rows, :] = _residual_norm(x_ref[0, rows, :], y, ln_g_ref[...], ln_b_ref[...])


def _even_mixer(x, lb_index, *params):
    bsz, t_len, d = x.shape
    tm = EVEN_ROW_TILE
    assert t_len % tm == 0
    kern = functools.partial(_even_mixer_kernel, lb_index=lb_index)
    return pl.pallas_call(
        kern,
        grid=(bsz, t_len // tm),
        in_specs=[pl.BlockSpec((1, tm, d), lambda b, t: (b, t, 0))] + [_resident(p) for p in params],
        out_specs=pl.BlockSpec((1, tm, d), lambda b, t: (b, t, 0)),
        out_shape=jax.ShapeDtypeStruct(x.shape, F32),
        scratch_shapes=[
            pltpu.VMEM((tm, HGRN_WIDTH), F32), pltpu.VMEM((tm, HGRN_WIDTH), F32),
            pltpu.VMEM((tm, HGRN_WIDTH), F32), pltpu.VMEM((tm, HGRN_WIDTH), F32),
            pltpu.VMEM((tm, GMLP_WIDTH + HGRN_WIDTH), BF16),
            pltpu.VMEM((HGRN_HEADS, HGRN_DK, HGRN_DK), F32),
            _bf16_copy(params[0]), _bf16_copy(params[1]),
        ],
        compiler_params=_params("arbitrary", "arbitrary"),
        name="even_mixer",
    )(x, *_stacks(*params))


def _mem_kv_kernel(mem_ref, w_f32_ref, o_ref, w_ref):
    _cast_weights_once(1, (w_f32_ref, w_ref))
    o_ref[...] = _dot(mem_ref[...].astype(BF16), w_ref[...]).astype(BF16)


def _mem_kv(mem2d, w_kv):
    n, d = mem2d.shape
    tm = MEM_ROW_TILE
    assert n % tm == 0
    width = w_kv[0].shape[-1]
    return pl.pallas_call(
        _mem_kv_kernel,
        grid=(n // tm,),
        in_specs=[pl.BlockSpec((tm, d), lambda i: (i, 0)), _resident(w_kv)],
        out_specs=pl.BlockSpec((tm, width), lambda i: (i, 0)),
        out_shape=jax.ShapeDtypeStruct((n, width), BF16),
        scratch_shapes=[_bf16_copy(w_kv)],
        compiler_params=_params("arbitrary"),
        name="mem_kv",
    )(mem2d, *_stacks(w_kv))


def _cross_attn_kernel(x_ref, kv_ref, wq_f32_ref, wo_f32_ref, ln_g_ref, ln_b_ref, o_ref, att_s, y_s,
                       wq_ref, wo_ref):
    _cast_weights_once(2, (wq_f32_ref, wq_ref), (wo_f32_ref, wo_ref))
    half = x_ref.shape[1] // 2
    slice_rows = half // MEM_HEADS

    def attend(rows):
        q = (_dot(x_ref[0, rows, :].astype(BF16), wq_ref[...]) * (MEM_HDIM ** -0.5)).astype(BF16)
        done = []
        for h in range(MEM_HEADS):
            lanes = slice(h * MEM_HDIM, (h + 1) * MEM_HDIM)
            k_h = kv_ref[0, :, h * MEM_HDIM:(h + 1) * MEM_HDIM]
            v_h = kv_ref[0, :, D_MODEL + h * MEM_HDIM:D_MODEL + (h + 1) * MEM_HDIM]
            s = _dot_nt(q[:, lanes], k_h)
            p = jnp.exp(s - jnp.max(s, axis=-1, keepdims=True))
            denom = jnp.sum(p, axis=-1, keepdims=True)
            weighted = _dot(p.astype(BF16), v_h)
            att_s[rows, lanes] = (weighted / denom).astype(BF16)
            done.append(weighted[0:1, 0:1])
        y_s[rows, :] = _dot(att_s[rows, :], wo_ref[...])
        return done

    def finish(rows, after=None):
        o_ref[0, rows, :] = _residual_norm(x_ref[0, rows, :], y_s[rows, :], ln_g_ref[...], ln_b_ref[...], after)

    attend(slice(0, half))
    done = attend(slice(half, 2 * half))
    for h in range(MEM_HEADS):
        finish(slice(h * slice_rows, (h + 1) * slice_rows), after=done[h])
    finish(slice(half, 2 * half))


def _cross_attn(x, kv, *params):
    bsz, t_len, d = x.shape
    tm = ROW_TILE
    return pl.pallas_call(
        _cross_attn_kernel,
        grid=(bsz, t_len // tm),
        in_specs=[
            pl.BlockSpec((1, tm, d), lambda b, t: (b, t, 0)),
            pl.BlockSpec((1,) + kv.shape[1:], lambda b, t: (b, 0, 0)),
        ] + [_resident(p) for p in params],
        out_specs=pl.BlockSpec((1, tm, d), lambda b, t: (b, t, 0)),
        out_shape=jax.ShapeDtypeStruct(x.shape, F32),
        scratch_shapes=[pltpu.VMEM((tm, d), BF16), pltpu.VMEM((tm, d), F32),
                        _bf16_copy(params[0]), _bf16_copy(params[1])],
        compiler_params=_params("arbitrary", "arbitrary"),
        name="cross_attn",
    )(x, kv, *_stacks(*params))


def _ffn_kernel(x_ref, w_in_ref, w_out_ref, ln_g_ref, ln_b_ref, o_ref, acc_s):
    half = x_ref.shape[0] // 2
    n_chunks = D_FF // FFN_CHUNK
    slice_rows = half // FFN_NORM_SLICES

    def chunk(rows, xb, c):
        cols = slice(c * FFN_CHUNK, (c + 1) * FFN_CHUNK)
        gate = _dot(xb, w_in_ref[:, cols])
        up = _dot(xb, w_in_ref[:, D_FF + c * FFN_CHUNK:D_FF + (c + 1) * FFN_CHUNK])
        act = (jax.nn.silu(gate) * up).astype(BF16)
        part = _dot(act, w_out_ref[cols, :])
        if c == 0:
            acc_s[rows, :] = part
        else:
            acc_s[rows, :] += part
        return part[0:1, :]

    def finish(rows, after=None):
        o_ref[rows, :] = _residual_norm(x_ref[rows, :], acc_s[rows, :], ln_g_ref[...], ln_b_ref[...], after)

    first, second = slice(0, half), slice(half, 2 * half)
    xb_first = x_ref[first, :].astype(BF16)
    xb_second = x_ref[second, :].astype(BF16)
    for c in range(n_chunks):
        chunk(first, xb_first, c)
    for c in range(n_chunks):
        done = chunk(second, xb_second, c)
        if c < FFN_NORM_SLICES:
            finish(slice(c * slice_rows, (c + 1) * slice_rows), after=done)
    finish(second)


def _ffn(x2d, *params):
    n, d = x2d.shape
    tm = ROW_TILE
    assert n % tm == 0
    return pl.pallas_call(
        _ffn_kernel,
        grid=(n // tm,),
        in_specs=[pl.BlockSpec((tm, d), lambda i: (i, 0))] + [_resident(p) for p in params],
        out_specs=pl.BlockSpec((tm, d), lambda i: (i, 0)),
        out_shape=jax.ShapeDtypeStruct((n, d), F32),
        scratch_shapes=[pltpu.VMEM((tm, d), F32)],
        compiler_params=_params("parallel"),
        name="ffn",
    )(x2d, *_stacks(*params))


LOG2_E = 1.4426950408889634
ALIBI_LANES = 3


def _moba_qkv_kernel(x_ref, w_f32_ref, fill_ref, q0_ref, q1_ref, k0_ref, k1_ref, vt_ref, kmean_ref, w_ref,
                     *, tiles_per_seq):
    _cast_weights_once(1, (w_f32_ref, w_ref))
    tm = x_ref.shape[0]
    xb = x_ref[...].astype(BF16)
    transposed = (((0,), (1,)), ((), ()))
    qt = lax.dot_general(w_ref[:, 0:D_MODEL], xb, transposed, preferred_element_type=F32)
    qt = qt * (MOBA_HDIM ** -0.5 * LOG2_E)
    k = _dot(xb, w_ref[:, D_MODEL:2 * D_MODEL])
    for i in range(tm // MOBA_BLOCK):
        kmean_ref[i] = jnp.mean(k[i * MOBA_BLOCK:(i + 1) * MOBA_BLOCK], axis=0, keepdims=True)
    vt_ref[0] = lax.dot_general(w_ref[:, 2 * D_MODEL:3 * D_MODEL], xb, transposed,
                                preferred_element_type=F32).astype(BF16)

    lane = lax.broadcasted_iota(jnp.int32, (tm, MOBA_PAIR), 1)
    pos = lax.broadcasted_iota(jnp.int32, (tm, MOBA_PAIR), 0) + (pl.program_id(0) % tiles_per_seq) * tm
    pos_block = ((pos // MOBA_BLOCK) * MOBA_BLOCK).astype(F32)
    pos_offset = (pos % MOBA_BLOCK).astype(F32)
    feature = lax.broadcasted_iota(jnp.int32, (MOBA_PAIR, tm), 0)
    for e, (q_ref, k_ref) in enumerate(((q0_ref, k0_ref), (q1_ref, k1_ref))):
        own = (lane < MOBA_HDIM) if e == 0 else (lane >= MOBA_HDIM)
        own_feature = (feature < MOBA_HDIM) if e == 0 else (feature >= MOBA_HDIM)
        partner = MOBA_HDIM * (1 - e)
        key_fill = jnp.where((lane >= partner) & (lane < partner + ALIBI_LANES), pos_block,
                             jnp.where((lane >= partner + ALIBI_LANES) & (lane < partner + 2 * ALIBI_LANES),
                                       pos_offset, 0.0))
        for p in range(MOBA_HEADS // 2):
            slab = slice(p * MOBA_PAIR, (p + 1) * MOBA_PAIR)
            q_fill = jnp.concatenate([fill_ref[e, slab, :]] * (tm // fill_ref.shape[-1]), axis=1)
            q_ref[0, slab, :] = jnp.where(own_feature, qt[slab, :], q_fill).astype(BF16)
            k_ref[:, slab] = jnp.where(own, k[:, slab], key_fill).astype(BF16)


def _moba_qkv(x2d, w_qkv, fill, bsz, t_len):
    n, d = x2d.shape
    tm = ROW_TILE
    tiles_per_seq = t_len // tm
    rows = pl.BlockSpec((tm, d), lambda i: (i, 0))
    feature_major = pl.BlockSpec((1, d, tm), lambda i: (i // tiles_per_seq, 0, i % tiles_per_seq))
    return pl.pallas_call(
        functools.partial(_moba_qkv_kernel, tiles_per_seq=tiles_per_seq),
        grid=(n // tm,),
        in_specs=[rows, _resident(w_qkv), _resident(fill)],
        out_specs=[
            feature_major, feature_major, rows, rows, feature_major,
            pl.BlockSpec((tm // MOBA_BLOCK, 1, d), lambda i: (i, 0, 0)),
        ],
        out_shape=[
            jax.ShapeDtypeStruct((bsz, d, t_len), BF16), jax.ShapeDtypeStruct((bsz, d, t_len), BF16),
            jax.ShapeDtypeStruct((n, d), BF16), jax.ShapeDtypeStruct((n, d), BF16),
            jax.ShapeDtypeStruct((bsz, d, t_len), BF16),
            jax.ShapeDtypeStruct((n // MOBA_BLOCK, 1, d), F32),
        ],
        scratch_shapes=[_bf16_copy(w_qkv)],
        compiler_params=_params("arbitrary"),
        name="moba_qkv",
    )(x2d, *_stacks(w_qkv, fill))


def _alibi_query_fill():
    slopes = jnp.asarray([2.0 ** (-8.0 * (h + 1) / MOBA_HEADS) for h in range(MOBA_HEADS)], F32) * LOG2_E
    pieces = []
    rest = slopes
    for _ in range(ALIBI_LANES):
        piece = rest.astype(BF16).astype(F32)
        pieces.append(piece)
        rest = rest - piece
    pieces = jnp.stack(pieces + pieces, axis=1)
    fill = jnp.zeros((2, MOBA_HEADS // 2, MOBA_PAIR), F32)
    for e in range(2):
        partner = MOBA_HDIM * (1 - e)
        fill = fill.at[e, :, partner:partner + 2 * ALIBI_LANES].set(pieces[e::2])
    return jnp.broadcast_to(fill.reshape(2, D_MODEL, 1), (2, D_MODEL, MOBA_PAIR))


def _moba_attn_kernel(x_ref, q0_ref, q1_ref, k0_ref, k1_ref, vt_ref, kmean_ref, w_out_ref, ln_g_ref, ln_b_ref,
                      o_ref, sa_s, sb_s, sha_s, shb_s, pa_s, pb_s, att_s, *, first_block, n_query_blocks):
    n_blocks = kmean_ref.shape[1]
    blk = MOBA_BLOCK
    q_refs = (q0_ref, q1_ref)
    k_refs = (k0_ref, k1_ref)
    blk_i = lax.broadcasted_iota(jnp.int32, (n_blocks, blk), 0)
    mean_lane = lax.broadcasted_iota(jnp.int32, (n_blocks, MOBA_PAIR), 1)

    def attend(n_past, query_rows, att_buf, buffers_read):
        n_keys = (n_past + 1) * blk
        select = n_past > MOBA_TOPK

        def pair_lanes(hp):
            return pl.ds(pl.multiple_of(hp * MOBA_PAIR, MOBA_PAIR), MOBA_PAIR)

        def scores_and_shifts(hp, s_buf, sh_buf, after=None):
            lanes = pair_lanes(hp)
            for e in range(2):
                cols = slice(e * blk, (e + 1) * blk)
                q_e = q_refs[e][0, lanes, query_rows]
                if after is not None:
                    q_e = (q_e.astype(F32) + _zero_after(after)).astype(BF16)
                if select:
                    own = (mean_lane < MOBA_HDIM) if e == 0 else (mean_lane >= MOBA_HDIM)
                    kmean = jnp.where(own, kmean_ref[0, :, lanes], 0.0)
                    kmean_hi = kmean.astype(BF16)
                    kmean_lo = (kmean - kmean_hi.astype(F32)).astype(BF16)
                    aff = _dot(kmean_hi, q_e) + _dot(kmean_lo, q_e)
                    rank = jnp.zeros((n_blocks, blk), F32)
                    for jp in range(n_past):
                        other = aff[jp:jp + 1, :]
                        beats = (other > aff) | ((other == aff) & (jp < blk_i))
                        rank = rank + jnp.where(beats, 1.0, 0.0)
                    chosen = rank < MOBA_TOPK
                m = None
                for j in range(n_past + 1):
                    rows = slice(j * blk, (j + 1) * blk)
                    t = _dot(k_refs[e][0, rows, lanes], q_e)
                    if j == n_past:
                        key_i = lax.broadcasted_iota(jnp.int32, (blk, blk), 0)
                        qry_i = lax.broadcasted_iota(jnp.int32, (blk, blk), 1)
                        t = jnp.where(key_i <= qry_i, t, MASKED)
                    s_buf[rows, cols] = t
                    m_j = jnp.max(t, axis=0, keepdims=True)
                    if select and j < n_past:
                        m_j = jnp.where(chosen[j:j + 1, :], m_j, MASKED)
                    m = m_j if m is None else jnp.maximum(m, m_j)
                shifts = jnp.broadcast_to(m, (n_blocks, blk))
                if select:
                    shifts = jnp.where(chosen | (blk_i == n_past), shifts, -MASKED)
                sh_buf[e] = shifts

        def probabilities(s_buf, sh_buf, p_buf, after=None):
            hold = 0.0 if after is None else _zero_after(after)
            for e in range(2):
                cols = slice(e * blk, (e + 1) * blk)
                for j in range(n_past + 1):
                    rows = slice(j * blk, (j + 1) * blk)
                    shift = sh_buf[e, j:j + 1, :] + hold
                    p_buf[e, rows, :] = jnp.exp2(s_buf[rows, cols] - shift).astype(BF16)

        def weighted_values(hp, p_buf):
            for e in range(2):
                feat = pl.ds(pl.multiple_of(hp * MOBA_PAIR + e * MOBA_HDIM, MOBA_HDIM), MOBA_HDIM)
                ones = jnp.ones((BF16_SUBLANES, n_keys), BF16)
                values = jnp.concatenate([vt_ref[0, feat, 0:n_keys], ones], axis=0)
                acc = _dot(values, p_buf[e, 0:n_keys, :])
                denom = acc[MOBA_HDIM:MOBA_HDIM + 1]
                att_buf[feat, :] = acc[0:MOBA_HDIM] / denom
            return denom[:, 0:1]

        n_pairs = MOBA_HEADS // 2
        read_a, read_b = buffers_read
        scores_and_shifts(0, sa_s, sha_s)
        scores_and_shifts(1, sb_s, shb_s, after=read_b)
        probabilities(sa_s, sha_s, pa_s, after=read_a)

        def two_pairs(i, carry):
            hp = 2 * i + 1
            scores_and_shifts(hp + 1, sa_s, sha_s)
            probabilities(sb_s, shb_s, pb_s)
            consumed = weighted_values(hp - 1, pa_s)
            scores_and_shifts(hp + 2, sb_s, shb_s)
            probabilities(sa_s, sha_s, pa_s, after=consumed)
            weighted_values(hp, pb_s)
            return carry

        lax.fori_loop(0, n_pairs // 2 - 1, two_pairs, 0)
        probabilities(sb_s, shb_s, pb_s)
        return weighted_values(n_pairs - 2, pa_s), weighted_values(n_pairs - 1, pb_s)

    buffers_read = (None, None)
    for j in range(n_query_blocks):
        query_rows = slice(j * blk, (j + 1) * blk)
        buffers_read = attend(first_block + j, query_rows, att_s.at[j], buffers_read)
        att = att_s[j].T.astype(BF16)
        y = _dot(att, w_out_ref[...])
        o_ref[0, query_rows, :] = _residual_norm(x_ref[0, query_rows, :], y, ln_g_ref[...], ln_b_ref[...])


def _moba_attn(x, first_block, n_query_blocks, q0, q1, k0, k1, vt, kmean, *params):
    bsz, t_len, d = x.shape
    blk = MOBA_BLOCK
    n_blocks = t_len // blk
    n_keys = (first_block + n_query_blocks) * blk
    assert first_block % n_query_blocks == 0
    query_rows = pl.BlockSpec((1, n_query_blocks * blk, d), lambda b: (b, first_block // n_query_blocks, 0))
    query_cols = pl.BlockSpec((1, d, n_query_blocks * blk), lambda b: (b, 0, first_block // n_query_blocks))
    visible_keys = pl.BlockSpec((1, n_keys, d), lambda b: (b, 0, 0))
    return pl.pallas_call(
        functools.partial(_moba_attn_kernel, first_block=first_block, n_query_blocks=n_query_blocks),
        grid=(bsz,),
        in_specs=[
            query_rows, query_cols, query_cols, visible_keys, visible_keys,
            pl.BlockSpec((1, d, n_keys), lambda b: (b, 0, 0)),
            pl.BlockSpec((1, n_blocks, d), lambda b: (b, 0, 0)),
        ] + [_resident(p) for p in params],
        out_specs=query_rows,
        out_shape=jax.ShapeDtypeStruct(x.shape, F32),
        input_output_aliases={0: 0},
        scratch_shapes=[
            pltpu.VMEM((n_keys, 2 * blk), F32),
            pltpu.VMEM((n_keys, 2 * blk), F32),
            pltpu.VMEM((2, n_blocks, blk), F32),
            pltpu.VMEM((2, n_blocks, blk), F32),
            pltpu.VMEM((2, n_keys, blk), BF16),
            pltpu.VMEM((2, n_keys, blk), BF16),
            pltpu.VMEM((n_query_blocks, d, blk), F32),
        ],
        compiler_params=_params("arbitrary"),
        name=f"moba_attn_from_block_{first_block}",
    )(x, q0, q1, k0, k1, vt, kmean, *_stacks(*params))


def kernel(x, mem, ln_g, ln_b, x_wq, x_wkv, x_wo, ffn_w_in, ffn_w_out, ev_w_in, ev_w_out, a_ws, a_bs,
           a_ln_g, a_ln_b, b_norm_g, hgrn_lb_logits, od_w_qkv, od_w_out):
    bsz, t_len, d = x.shape
    assert d == D_MODEL and t_len % ROW_TILE == 0 and t_len % MOBA_BLOCK == 0
    assert ROW_TILE % MOBA_BLOCK == 0 and t_len // MOBA_BLOCK > 1
    n = bsz * t_len
    mem2d = mem.reshape(bsz * mem.shape[1], d)

    def rows_of(v):
        return v.reshape(-1, 1, v.shape[-1])

    def seq(v):
        return v.reshape(bsz, t_len, d)

    ffn_w_in, ffn_w_out = ffn_w_in.astype(BF16), ffn_w_out.astype(BF16)
    od_w_out = od_w_out.astype(BF16)
    ln_g, ln_b = rows_of(ln_g), rows_of(ln_b)
    a_ln_g, a_ln_b, b_norm_g = rows_of(a_ln_g), rows_of(a_ln_b), rows_of(b_norm_g)
    a_bs_t = jnp.swapaxes(a_bs, 1, 2)
    lb_logits = hgrn_lb_logits[None]
    alibi_fill = _alibi_query_fill()[None]

    for layer in range(DEPTH):
        j = layer // 2
        norm = [(ln_g, 3 * layer), (ln_b, 3 * layer)]
        if layer % 2 == 0:
            x = _even_mixer(
                x, j, (ev_w_in, j), (ev_w_out, j), (a_ws, j), (a_bs_t, j), (a_ln_g, j), (a_ln_b, j),
                (b_norm_g, j), (lb_logits, 0), *norm)
        else:
            q0, q1, k0, k1, vt, kmean = _moba_qkv(x.reshape(n, d), (od_w_qkv, j), (alibi_fill, 0), bsz, t_len)
            attn_args = (q0, q1, seq(k0), seq(k1), vt,
                         kmean.reshape(bsz, t_len // MOBA_BLOCK, d), (od_w_out, j), *norm)
            n_blocks = t_len // MOBA_BLOCK
            for first_block in range(0, n_blocks, MOBA_BLOCKS_PER_CALL):
                x = _moba_attn(x, first_block, min(MOBA_BLOCKS_PER_CALL, n_blocks - first_block), *attn_args)
        kv = _mem_kv(mem2d, (x_wkv, layer)).reshape(bsz, mem.shape[1], 2 * d)
        x = _cross_attn(x, kv, (x_wq, layer), (x_wo, layer), (ln_g, 3 * layer + 1), (ln_b, 3 * layer + 1))
        x = _ffn(x.reshape(n, d), (ffn_w_in, layer), (ffn_w_out, layer),
                 (ln_g, 3 * layer + 2), (ln_b, 3 * layer + 2)).reshape(bsz, t_len, d)
    return x
```

```python
import functools
import math

import jax
import jax.numpy as jnp
from jax import lax
from jax.experimental import pallas as pl
from jax.experimental.pallas import tpu as pltpu

D_MODEL = 1024
DEPTH = 2
ALPHA = (2.0 * DEPTH) ** 0.25
LN_EPS = 1e-5

GMLP_WIDTH = D_MODEL // 2
GMLP_GROUPS = 4
GMLP_GDIM = GMLP_WIDTH // GMLP_GROUPS
GMLP_CHUNK = 128
HGRN_WIDTH = D_MODEL // 2
HGRN_HEADS = 4
HGRN_DK = HGRN_WIDTH // HGRN_HEADS
HGRN_CHUNK = 64
EVEN_IN_WIDTH = 2 * GMLP_WIDTH + 4 * HGRN_WIDTH

MOBA_HEADS = 16
MOBA_HDIM = D_MODEL // MOBA_HEADS
MOBA_BLOCK = 256
MOBA_TOPK = 3
MOBA_PAIR = 2 * MOBA_HDIM
MOBA_BLOCKS_PER_CALL = 2

MEM_HEADS = 4
MEM_HDIM = D_MODEL // MEM_HEADS

D_FF = int(math.ceil(8 * D_MODEL / 3 / 256)) * 256
FFN_CHUNK = 256

ROW_TILE = 1024
MEM_ROW_TILE = 512
EVEN_ROW_TILE = 512
FFN_NORM_SLICES = 8
V7X_VMEM_LIMIT = 56 * 1024 * 1024

MASKED = -1e30

BF16 = jnp.bfloat16
F32 = jnp.float32
BF16_SUBLANES = 16


def _dot(a, b):
    return jnp.dot(a, b, preferred_element_type=F32)


def _dot_nt(a, b):
    return lax.dot_general(a, b, (((1,), (1,)), ((), ())), preferred_element_type=F32)


def _dot_tn(a, b):
    return lax.dot_general(a, b, (((0,), (0,)), ((), ())), preferred_element_type=F32)


def _layer_norm(z, g, b):
    mu = jnp.mean(z, axis=-1, keepdims=True)
    zc = z - mu
    var = jnp.mean(zc * zc, axis=-1, keepdims=True)
    return zc * lax.rsqrt(var + LN_EPS) * g + b


def _zero_after(token):
    bits = lax.bitcast_convert_type(token, jnp.int32)
    return lax.shift_right_logical(lax.shift_right_logical(bits, 16), 16).astype(F32)


def _residual_norm(x, y, g, b, after=None):
    alpha = ALPHA if after is None else ALPHA + _zero_after(after)
    return _layer_norm(x * alpha + y, g, b)


def _resident(picked):
    stack, index = picked
    return pl.BlockSpec((None,) + stack.shape[1:], lambda *_: (index,) + (0,) * (stack.ndim - 1),
                        pipeline_mode=pl.Buffered(1))


def _stacks(*picked):
    return [stack for stack, _ in picked]


def _bf16_copy(picked):
    stack, _ = picked
    return pltpu.VMEM(stack.shape[1:], BF16)


def _cast_weights_once(n_grid_axes, *pairs):
    first = pl.program_id(0) == 0
    for axis in range(1, n_grid_axes):
        first = jnp.logical_and(first, pl.program_id(axis) == 0)

    @pl.when(first)
    def _():
        for src, dst in pairs:
            dst[...] = src[...].astype(BF16)


def _params(*semantics):
    return pltpu.CompilerParams(dimension_semantics=semantics, vmem_limit_bytes=V7X_VMEM_LIMIT)


def _even_mixer_kernel(x_ref, w_in_f32_ref, w_out_f32_ref, ws_ref, bs_ref, aln_g_ref, aln_b_ref, bnorm_ref,
                       lb_logits_ref, ln_g_ref, ln_b_ref, o_ref,
                       q_s, f_s, i_s, g_s, y_s, state_s, w_in_ref, w_out_ref, *, lb_index):
    tm = x_ref.shape[1]
    _cast_weights_once(2, (w_in_f32_ref, w_in_ref), (w_out_f32_ref, w_out_ref))
    x = x_ref[0]
    xb = x.astype(BF16)

    @pl.when(pl.program_id(1) == 0)
    def _():
        state_s[...] = jnp.zeros_like(state_s)

    base = 2 * GMLP_WIDTH
    u_pre = _dot(xb, w_in_ref[:, 0:GMLP_WIDTH])
    v_pre = _dot(xb, w_in_ref[:, GMLP_WIDTH:2 * GMLP_WIDTH])
    q_pre = _dot(xb, w_in_ref[:, base:base + HGRN_WIDTH])
    q_s[...] = q_pre
    f_pre = _dot(xb, w_in_ref[:, base + HGRN_WIDTH:base + 2 * HGRN_WIDTH])
    f_s[...] = f_pre
    i_s[...] = jax.nn.silu(_dot(xb, w_in_ref[:, base + 2 * HGRN_WIDTH:base + 3 * HGRN_WIDTH]))
    g_s[...] = jax.nn.silu(_dot(xb, w_in_ref[:, base + 3 * HGRN_WIDTH:base + 4 * HGRN_WIDTH]))

    u = jax.nn.gelu(u_pre + _zero_after(q_pre[0:1, 0:1]))
    v = jax.nn.gelu(v_pre + _zero_after(f_pre[0:1, 0:1]))
    row = lax.broadcasted_iota(jnp.int32, (GMLP_CHUNK, GMLP_CHUNK), 0)
    col = lax.broadcasted_iota(jnp.int32, (GMLP_CHUNK, GMLP_CHUNK), 1)
    for g in range(GMLP_GROUPS):
        lanes = slice(g * GMLP_GDIM, (g + 1) * GMLP_GDIM)
        vn = _layer_norm(v[:, lanes], aln_g_ref[:, lanes], aln_b_ref[:, lanes]).astype(BF16)
        wg = jnp.where(col <= row, ws_ref[g], 0.0).astype(BF16)
        bias = bs_ref[:, g:g + 1]
        for c in range(tm // GMLP_CHUNK):
            rows = slice(c * GMLP_CHUNK, (c + 1) * GMLP_CHUNK)
            s = _dot(wg, vn[rows]) + bias
            y_s[rows, lanes] = (u[rows, lanes] * s).astype(BF16)

    n_chunks = tm // HGRN_CHUNK
    logits = lb_logits_ref[...]
    e = jnp.exp(logits - jnp.max(logits, axis=0, keepdims=True))
    lb = jnp.sum(e[0:lb_index + 1], axis=0, keepdims=True) / jnp.sum(e, axis=0, keepdims=True)

    crow =lax.broadcasted_iota(jnp.int32, (HGRN_CHUNK, HGRN_CHUNK), 0)
    ccol = lax.broadcasted_iota(jnp.int32, (HGRN_CHUNK, HGRN_CHUNK), 1)
    causal = ccol <= crow
    tril_ones = jnp.where(causal, 1.0, 0.0).astype(BF16)
    tril_twice = jnp.concatenate([tril_ones, tril_ones], axis=1)
    norm_g = bnorm_ref[...]

    for c in range(n_chunks):
        rows = slice(c * HGRN_CHUNK, (c + 1) * HGRN_CHUNK)
        f = lb + (1.0 - lb) * jax.nn.sigmoid(f_s[rows, :])
        log_f = jnp.log(f)
        log_f_hi = log_f.astype(BF16)
        log_f_lo = (log_f - log_f_hi.astype(F32)).astype(BF16)
        cum = _dot(tril_twice, jnp.concatenate([log_f_hi, log_f_lo], axis=0))
        chunk_decay = jnp.exp(cum[HGRN_CHUNK - 1:HGRN_CHUNK, :])
        k_back = (1.0 - f) * jnp.exp(-cum)
        q_dec = (q_s[rows, :] * jnp.exp(cum)).astype(BF16)
        k_dec = k_back.astype(BF16)
        k_tail = (k_back * chunk_decay).astype(BF16)
        val = i_s[rows, :].astype(BF16)
        gate = g_s[rows, :]
        for h in range(HGRN_HEADS):
            lanes = slice(h * HGRN_DK, (h + 1) * HGRN_DK)
            attn = jnp.where(causal, _dot_nt(q_dec[:, lanes], k_dec[:, lanes]), 0.0).astype(BF16)
            state_t = state_s[h]
            o = _dot(jnp.concatenate([q_dec[:, lanes], attn], axis=1),
                     jnp.concatenate([state_t.T.astype(BF16), val[:, lanes]], axis=0))
            state_s[h] = state_t * chunk_decay[:, lanes] + _dot_tn(val[:, lanes], k_tail[:, lanes])
            rms = lax.rsqrt(jnp.mean(o * o, axis=-1, keepdims=True) + LN_EPS)
            y_b = o * rms * norm_g[:, lanes] * gate[:, lanes]
            y_s[rows, GMLP_WIDTH + h * HGRN_DK:GMLP_WIDTH + (h + 1) * HGRN_DK] = y_b.astype(BF16)

    for rows in (slice(0, tm // 2), slice(tm // 2, tm)):
        y = _dot(y_s[rows, :], w_out_ref[...])
        o_ref[0, rows, :] = _residual_norm(x_ref[0, rows, :], y, ln_g_ref[...], ln_b_ref[...])


def _even_mixer(x, lb_index, *params):
    bsz, t_len, d = x.shape
    tm = EVEN_ROW_TILE
    assert t_len % tm == 0
    kern = functools.partial(_even_mixer_kernel, lb_index=lb_index)
    return pl.pallas_call(
        kern,
        grid=(bsz, t_len // tm),
        in_specs=[pl.BlockSpec((1, tm, d), lambda b, t: (b, t, 0))] + [_resident(p) for p in params],
        out_specs=pl.BlockSpec((1, tm, d), lambda b, t: (b, t, 0)),
        out_shape=jax.ShapeDtypeStruct(x.shape, F32),
        scratch_shapes=[
            pltpu.VMEM((tm, HGRN_WIDTH), F32), pltpu.VMEM((tm, HGRN_WIDTH), F32),
            pltpu.VMEM((tm, HGRN_WIDTH), F32), pltpu.VMEM((tm, HGRN_WIDTH), F32),
            pltpu.VMEM((tm, GMLP_WIDTH + HGRN_WIDTH), BF16),
            pltpu.VMEM((HGRN_HEADS, HGRN_DK, HGRN_DK), F32),
            _bf16_copy(params[0]), _bf16_copy(params[1]),
        ],
        compiler_params=_params("arbitrary", "arbitrary"),
        name="even_mixer",
    )(x, *_stacks(*params))


def _mem_kv_kernel(mem_ref, w_f32_ref, o_ref, w_ref):
    _cast_weights_once(1, (w_f32_ref, w_ref))
    o_ref[...] = _dot(mem_ref[...].astype(BF16), w_ref[...]).astype(BF16)


def _mem_kv(mem2d, w_kv):
    n, d = mem2d.shape
    tm = MEM_ROW_TILE
    assert n % tm == 0
    width = w_kv[0].shape[-1]
    return pl.pallas_call(
        _mem_kv_kernel,
        grid=(n // tm,),
        in_specs=[pl.BlockSpec((tm, d), lambda i: (i, 0)), _resident(w_kv)],
        out_specs=pl.BlockSpec((tm, width), lambda i: (i, 0)),
        out_shape=jax.ShapeDtypeStruct((n, width), BF16),
        scratch_shapes=[_bf16_copy(w_kv)],
        compiler_params=_params("arbitrary"),
        name="mem_kv",
    )(mem2d, *_stacks(w_kv))


def _cross_attn_kernel(x_ref, kv_ref, wq_f32_ref, wo_f32_ref, ln_g_ref, ln_b_ref, o_ref, att_s, y_s,
                       wq_ref, wo_ref):
    _cast_weights_once(2, (wq_f32_ref, wq_ref), (wo_f32_ref, wo_ref))
    half = x_ref.shape[1] // 2
    slice_rows = half // MEM_HEADS

    def attend(rows):
        q = (_dot(x_ref[0, rows, :].astype(BF16), wq_ref[...]) * (MEM_HDIM ** -0.5)).astype(BF16)
        done = []
        for h in range(MEM_HEADS):
            lanes = slice(h * MEM_HDIM, (h + 1) * MEM_HDIM)
            k_h = kv_ref[0, :, h * MEM_HDIM:(h + 1) * MEM_HDIM]
            v_h = kv_ref[0, :, D_MODEL + h * MEM_HDIM:D_MODEL + (h + 1) * MEM_HDIM]
            s = _dot_nt(q[:, lanes], k_h)
            p = jnp.exp(s - jnp.max(s, axis=-1, keepdims=True))
            denom = jnp.sum(p, axis=-1, keepdims=True)
            weighted = _dot(p.astype(BF16), v_h)
            att_s[rows, lanes] = (weighted / denom).astype(BF16)
            done.append(weighted[0:1, 0:1])
        y_s[rows, :] = _dot(att_s[rows, :], wo_ref[...])
        return done

    def finish(rows, after=None):
        o_ref[0, rows, :] = _residual_norm(x_ref[0, rows, :], y_s[rows, :], ln_g_ref[...], ln_b_ref[...], after)

    attend(slice(0, half))
    done = attend(slice(half, 2 * half))
    for h in range(MEM_HEADS):
        finish(slice(h * slice_rows, (h + 1) * slice_rows), after=done[h])
    finish(slice(half, 2 * half))


def _cross_attn(x, kv, *params):
    bsz, t_len, d = x.shape
    tm = ROW_TILE
    return pl.pallas_call(
        _cross_attn_kernel,
        grid=(bsz, t_len // tm),
        in_specs=[
            pl.BlockSpec((1, tm, d), lambda b, t: (b, t, 0)),
            pl.BlockSpec((1,) + kv.shape[1:], lambda b, t: (b, 0, 0)),
        ] + [_resident(p) for p in params],
        out_specs=pl.BlockSpec((1, tm, d), lambda b, t: (b, t, 0)),
        out_shape=jax.ShapeDtypeStruct(x.shape, F32),
        scratch_shapes=[pltpu.VMEM((tm, d), BF16), pltpu.VMEM((tm, d), F32),
                        _bf16_copy(params[0]), _bf16_copy(params[1])],
        compiler_params=_params("arbitrary", "arbitrary"),
        name="cross_attn",
    )(x, kv, *_stacks(*params))


def _ffn_kernel(x_ref, w_in_ref, w_out_ref, ln_g_ref, ln_b_ref, o_ref, acc_s):
    half = x_ref.shape[0] // 2
    n_chunks = D_FF // FFN_CHUNK
    slice_rows = half // FFN_NORM_SLICES

    def chunk(rows, xb, c):
        cols = slice(c * FFN_CHUNK, (c + 1) * FFN_CHUNK)
        gate = _dot(xb, w_in_ref[:, cols])
        up = _dot(xb, w_in_ref[:, D_FF + c * FFN_CHUNK:D_FF + (c + 1) * FFN_CHUNK])
        act = (jax.nn.silu(gate) * up).astype(BF16)
        part = _dot(act, w_out_ref[cols, :])
        if c == 0:
            acc_s[rows, :] = part
        else:
            acc_s[rows, :] += part
        return part[0:1, :]

    def finish(rows, after=None):
        o_ref[rows, :] = _residual_norm(x_ref[rows, :], acc_s[rows, :], ln_g_ref[...], ln_b_ref[...], after)

    first, second = slice(0, half), slice(half, 2 * half)
    xb_first = x_ref[first, :].astype(BF16)
    xb_second = x_ref[second, :].astype(BF16)
    for c in range(n_chunks):
        chunk(first, xb_first, c)
    for c in range(n_chunks):
        done = chunk(second, xb_second, c)
        if c < FFN_NORM_SLICES:
            finish(slice(c * slice_rows, (c + 1) * slice_rows), after=done)
    finish(second)


def _ffn(x2d, *params):
    n, d = x2d.shape
    tm = ROW_TILE
    assert n % tm == 0
    return pl.pallas_call(
        _ffn_kernel,
        grid=(n // tm,),
        in_specs=[pl.BlockSpec((tm, d), lambda i: (i, 0))] + [_resident(p) for p in params],
        out_specs=pl.BlockSpec((tm, d), lambda i: (i, 0)),
        out_shape=jax.ShapeDtypeStruct((n, d), F32),
        scratch_shapes=[pltpu.VMEM((tm, d), F32)],
        compiler_params=_params("parallel"),
        name="ffn",
    )(x2d, *_stacks(*params))


LOG2_E = 1.4426950408889634
ALIBI_LANES = 3


def _moba_qkv_kernel(x_ref, w_f32_ref, fill_ref, q0_ref, q1_ref, k0_ref, k1_ref, vt_ref, kmean_ref, w_ref,
                     *, tiles_per_seq):
    _cast_weights_once(1, (w_f32_ref, w_ref))
    tm = x_ref.shape[0]
    xb = x_ref[...].astype(BF16)
    transposed = (((0,), (1,)), ((), ()))
    qt = lax.dot_general(w_ref[:, 0:D_MODEL], xb, transposed, preferred_element_type=F32)
    qt = qt * (MOBA_HDIM ** -0.5 * LOG2_E)
    k = _dot(xb, w_ref[:, D_MODEL:2 * D_MODEL])
    for i in range(tm // MOBA_BLOCK):
        kmean_ref[i] = jnp.mean(k[i * MOBA_BLOCK:(i + 1) * MOBA_BLOCK], axis=0, keepdims=True)
    vt_ref[0] = lax.dot_general(w_ref[:, 2 * D_MODEL:3 * D_MODEL], xb, transposed,
                                preferred_element_type=F32).astype(BF16)

    lane = lax.broadcasted_iota(jnp.int32, (tm, MOBA_PAIR), 1)
    pos = lax.broadcasted_iota(jnp.int32, (tm, MOBA_PAIR), 0) + (pl.program_id(0) % tiles_per_seq) * tm
    pos_block = ((pos // MOBA_BLOCK) * MOBA_BLOCK).astype(F32)
    pos_offset = (pos % MOBA_BLOCK).astype(F32)
    feature = lax.broadcasted_iota(jnp.int32, (MOBA_PAIR, tm), 0)
    for e, (q_ref, k_ref) in enumerate(((q0_ref, k0_ref), (q1_ref, k1_ref))):
        own = (lane < MOBA_HDIM) if e == 0 else (lane >= MOBA_HDIM)
        own_feature = (feature < MOBA_HDIM) if e == 0 else (feature >= MOBA_HDIM)
        partner = MOBA_HDIM * (1 - e)
        key_fill = jnp.where((lane >= partner) & (lane < partner + ALIBI_LANES), pos_block,
                             jnp.where((lane >= partner + ALIBI_LANES) & (lane < partner + 2 * ALIBI_LANES),
                                       pos_offset, 0.0))
        for p in range(MOBA_HEADS // 2):
            slab = slice(p * MOBA_PAIR, (p + 1) * MOBA_PAIR)
            q_fill = jnp.concatenate([fill_ref[e, slab, :]] * (tm // fill_ref.shape[-1]), axis=1)
            q_ref[0, slab, :] = jnp.where(own_feature, qt[slab, :], q_fill).astype(BF16)
            k_ref[:, slab] = jnp.where(own, k[:, slab], key_fill).astype(BF16)


def _moba_qkv(x2d, w_qkv, fill, bsz, t_len):
    n, d = x2d.shape
    tm = ROW_TILE
    tiles_per_seq = t_len // tm
    rows = pl.BlockSpec((tm, d), lambda i: (i, 0))
    feature_major = pl.BlockSpec((1, d, tm), lambda i: (i // tiles_per_seq, 0, i % tiles_per_seq))
    return pl.pallas_call(
        functools.partial(_moba_qkv_kernel, tiles_per_seq=tiles_per_seq),
        grid=(n // tm,),
        in_specs=[rows, _resident(w_qkv), _resident(fill)],
        out_specs=[
            feature_major, feature_major, rows, rows, feature_major,
            pl.BlockSpec((tm // MOBA_BLOCK, 1, d), lambda i: (i, 0, 0)),
        ],
        out_shape=[
            jax.ShapeDtypeStruct((bsz, d, t_len), BF16), jax.ShapeDtypeStruct((bsz, d, t_len), BF16),
            jax.ShapeDtypeStruct((n, d), BF16), jax.ShapeDtypeStruct((n, d), BF16),
            jax.ShapeDtypeStruct((bsz, d, t_len), BF16),
            jax.ShapeDtypeStruct((n // MOBA_BLOCK, 1, d), F32),
        ],
        scratch_shapes=[_bf16_copy(w_qkv)],
        compiler_params=_params("arbitrary"),
        name="moba_qkv",
    )(x2d, *_stacks(w_qkv, fill))


def _alibi_query_fill():
    slopes = jnp.asarray([2.0 ** (-8.0 * (h + 1) / MOBA_HEADS) for h in range(MOBA_HEADS)], F32) * LOG2_E
    pieces = []
    rest = slopes
    for _ in range(ALIBI_LANES):
        piece = rest.astype(BF16).astype(F32)
        pieces.append(piece)
        rest = rest - piece
    pieces = jnp.stack(pieces + pieces, axis=1)
    fill = jnp.zeros((2, MOBA_HEADS // 2, MOBA_PAIR), F32)
    for e in range(2):
        partner = MOBA_HDIM * (1 - e)
        fill = fill.at[e, :, partner:partner + 2 * ALIBI_LANES].set(pieces[e::2])
    return jnp.broadcast_to(fill.reshape(2, D_MODEL, 1), (2, D_MODEL, MOBA_PAIR))


def _moba_attn_kernel(x_ref, q0_ref, q1_ref, k0_ref, k1_ref, vt_ref, kmean_ref, w_out_ref, ln_g_ref, ln_b_ref,
                      o_ref, sa_s, sb_s, sha_s, shb_s, pa_s, pb_s, att_s, *, first_block, n_query_blocks):
    n_blocks = kmean_ref.shape[1]
    blk = MOBA_BLOCK
    q_refs = (q0_ref, q1_ref)
    k_refs = (k0_ref, k1_ref)
    blk_i = lax.broadcasted_iota(jnp.int32, (n_blocks, blk), 0)
    mean_lane = lax.broadcasted_iota(jnp.int32, (n_blocks, MOBA_PAIR), 1)

    def attend(n_past, query_rows, att_buf, buffers_read):
        n_keys = (n_past + 1) * blk
        select = n_past > MOBA_TOPK

        def pair_lanes(hp):
            return pl.ds(pl.multiple_of(hp * MOBA_PAIR, MOBA_PAIR), MOBA_PAIR)

        def scores_and_shifts(hp, s_buf, sh_buf, after=None):
            lanes = pair_lanes(hp)
            for e in range(2):
                cols = slice(e * blk, (e + 1) * blk)
                q_e = q_refs[e][0, lanes, query_rows]
                if after is not None:
                    q_e = (q_e.astype(F32) + _zero_after(after)).astype(BF16)
                if select:
                    own = (mean_lane < MOBA_HDIM) if e == 0 else (mean_lane >= MOBA_HDIM)
                    kmean = jnp.where(own, kmean_ref[0, :, lanes], 0.0)
                    kmean_hi = kmean.astype(BF16)
                    kmean_lo = (kmean - kmean_hi.astype(F32)).astype(BF16)
                    aff = _dot(kmean_hi, q_e) + _dot(kmean_lo, q_e)
                    rank = jnp.zeros((n_blocks, blk), F32)
                    for jp in range(n_past):
                        other = aff[jp:jp + 1, :]
                        beats = (other > aff) | ((other == aff) & (jp < blk_i))
                        rank = rank + jnp.where(beats, 1.0, 0.0)
                    chosen = rank < MOBA_TOPK
                m = None
                for j in range(n_past + 1):
                    rows = slice(j * blk, (j + 1) * blk)
                    t = _dot(k_refs[e][0, rows, lanes], q_e)
                    if j == n_past:
                        key_i = lax.broadcasted_iota(jnp.int32, (blk, blk), 0)
                        qry_i = lax.broadcasted_iota(jnp.int32, (blk, blk), 1)
                        t = jnp.where(key_i <= qry_i, t, MASKED)
                    s_buf[rows, cols] = t
                    m_j = jnp.max(t, axis=0, keepdims=True)
                    if select and j < n_past:
                        m_j = jnp.where(chosen[j:j + 1, :], m_j, MASKED)
                    m = m_j if m is None else jnp.maximum(m, m_j)
                shifts = jnp.broadcast_to(m, (n_blocks, blk))
                if select:
                    shifts = jnp.where(chosen | (blk_i == n_past), shifts, -MASKED)
                sh_buf[e] = shifts

        def probabilities(s_buf, sh_buf, p_buf, after=None):
            hold = 0.0 if after is None else _zero_after(after)
            for e in range(2):
                cols = slice(e * blk, (e + 1) * blk)
                for j in range(n_past + 1):
                    rows = slice(j * blk, (j + 1) * blk)
                    shift = sh_buf[e, j:j + 1, :] + hold
                    p_buf[e, rows, :] = jnp.exp2(s_buf[rows, cols] - shift).astype(BF16)

        def weighted_values(hp, p_buf):
            for e in range(2):
                feat = pl.ds(pl.multiple_of(hp * MOBA_PAIR + e * MOBA_HDIM, MOBA_HDIM), MOBA_HDIM)
                ones = jnp.ones((BF16_SUBLANES, n_keys), BF16)
                values = jnp.concatenate([vt_ref[0, feat, 0:n_keys], ones], axis=0)
                acc = _dot(values, p_buf[e, 0:n_keys, :])
                denom = acc[MOBA_HDIM:MOBA_HDIM + 1]
                att_buf[feat, :] = acc[0:MOBA_HDIM] / denom
            return denom[:, 0:1]

        n_pairs = MOBA_HEADS // 2
        read_a, read_b = buffers_read
        scores_and_shifts(0, sa_s, sha_s)
        scores_and_shifts(1, sb_s, shb_s, after=read_b)
        probabilities(sa_s, sha_s, pa_s, after=read_a)

        def two_pairs(i, carry):
            hp = 2 * i + 1
            scores_and_shifts(hp + 1, sa_s, sha_s)
            probabilities(sb_s, shb_s, pb_s)
            consumed = weighted_values(hp - 1, pa_s)
            scores_and_shifts(hp + 2, sb_s, shb_s)
            probabilities(sa_s, sha_s, pa_s, after=consumed)
            weighted_values(hp, pb_s)
            return carry

        lax.fori_loop(0, n_pairs // 2 - 1, two_pairs, 0)
        probabilities(sb_s, shb_s, pb_s)
        return weighted_values(n_pairs - 2, pa_s), weighted_values(n_pairs - 1, pb_s)

    buffers_read = (None, None)
    for j in range(n_query_blocks):
        query_rows = slice(j * blk, (j + 1) * blk)
        buffers_read = attend(first_block + j, query_rows, att_s.at[j], buffers_read)
        att = att_s[j].T.astype(BF16)
        y = _dot(att, w_out_ref[...])
        o_ref[0, query_rows, :] = _residual_norm(x_ref[0, query_rows, :], y, ln_g_ref[...], ln_b_ref[...])


def _moba_attn(x, first_block, n_query_blocks, q0, q1, k0, k1, vt, kmean, *params):
    bsz, t_len, d = x.shape
    blk = MOBA_BLOCK
    n_blocks = t_len // blk
    n_keys = (first_block + n_query_blocks) * blk
    assert first_block % n_query_blocks == 0
    query_rows = pl.BlockSpec((1, n_query_blocks * blk, d), lambda b: (b, first_block // n_query_blocks, 0))
    query_cols = pl.BlockSpec((1, d, n_query_blocks * blk), lambda b: (b, 0, first_block // n_query_blocks))
    visible_keys = pl.BlockSpec((1, n_keys, d), lambda b: (b, 0, 0))
    return pl.pallas_call(
        functools.partial(_moba_attn_kernel, first_block=first_block, n_query_blocks=n_query_blocks),
        grid=(bsz,),
        in_specs=[
            query_rows, query_cols, query_cols, visible_keys, visible_keys,
            pl.BlockSpec((1, d, n_keys), lambda b: (b, 0, 0)),
            pl.BlockSpec((1, n_blocks, d), lambda b: (b, 0, 0)),
        ] + [_resident(p) for p in params],
        out_specs=query_rows,
        out_shape=jax.ShapeDtypeStruct(x.shape, F32),
        input_output_aliases={0: 0},
        scratch_shapes=[
            pltpu.VMEM((n_keys, 2 * blk), F32),
            pltpu.VMEM((n_keys, 2 * blk), F32),
            pltpu.VMEM((2, n_blocks, blk), F32),
            pltpu.VMEM((2, n_blocks, blk), F32),
            pltpu.VMEM((2, n_keys, blk), BF16),
            pltpu.VMEM((2, n_keys, blk), BF16),
            pltpu.VMEM((n_query_blocks, d, blk), F32),
        ],
        compiler_params=_params("arbitrary"),
        name=f"moba_attn_from_block_{first_block}",
    )(x, q0, q1, k0, k1, vt, kmean, *_stacks(*params))


def kernel(x, mem, ln_g, ln_b, x_wq, x_wkv, x_wo, ffn_w_in, ffn_w_out, ev_w_in, ev_w_out, a_ws, a_bs,
           a_ln_g, a_ln_b, b_norm_g, hgrn_lb_logits, od_w_qkv, od_w_out):
    bsz, t_len, d = x.shape
    assert d == D_MODEL and t_len % ROW_TILE == 0 and t_len % MOBA_BLOCK == 0
    assert ROW_TILE % MOBA_BLOCK == 0 and t_len // MOBA_BLOCK > 1
    n = bsz * t_len
    mem2d = mem.reshape(bsz * mem.shape[1], d)

    def rows_of(v):
        return v.reshape(-1, 1, v.shape[-1])

    def seq(v):
        return v.reshape(bsz, t_len, d)

    ffn_w_in, ffn_w_out = ffn_w_in.astype(BF16), ffn_w_out.astype(BF16)
    od_w_out = od_w_out.astype(BF16)
    ln_g, ln_b = rows_of(ln_g), rows_of(ln_b)
    a_ln_g, a_ln_b, b_norm_g = rows_of(a_ln_g), rows_of(a_ln_b), rows_of(b_norm_g)
    a_bs_t = jnp.swapaxes(a_bs, 1, 2)
    lb_logits = hgrn_lb_logits[None]
    alibi_fill = _alibi_query_fill()[None]

    for layer in range(DEPTH):
        j = layer // 2
        norm = [(ln_g, 3 * layer), (ln_b, 3 * layer)]
        if layer % 2 == 0:
            x = _even_mixer(
                x, j, (ev_w_in, j), (ev_w_out, j), (a_ws, j), (a_bs_t, j), (a_ln_g, j), (a_ln_b, j),
                (b_norm_g, j), (lb_logits, 0), *norm)
        else:
            q0, q1, k0, k1, vt, kmean = _moba_qkv(x.reshape(n, d), (od_w_qkv, j), (alibi_fill, 0), bsz, t_len)
            attn_args = (q0, q1, seq(k0), seq(k1), vt,
                         kmean.reshape(bsz, t_len // MOBA_BLOCK, d), (od_w_out, j), *norm)
            n_blocks = t_len // MOBA_BLOCK
            for first_block in range(0, n_blocks, MOBA_BLOCKS_PER_CALL):
                x = _moba_attn(x, first_block, min(MOBA_BLOCKS_PER_CALL, n_blocks - first_block), *attn_args)
        kv = _mem_kv(mem2d, (x_wkv, layer)).reshape(bsz, mem.shape[1], 2 * d)
        x = _cross_attn(x, kv, (x_wq, layer), (x_wo, layer), (ln_g, 3 * layer + 1), (ln_b, 3 * layer + 1))
        x = _ffn(x.reshape(n, d), (ffn_w_in, layer), (ffn_w_out, layer),
                 (ln_g, 3 * layer + 2), (ln_b, 3 * layer + 2)).reshape(bsz, t_len, d)
    return x
```
